```python
import jax
import jax.numpy as jnp
from jax import lax
import numpy as np

D_MODEL = 1024
BATCH = 8
SEQ = 4096
DEPTH = 2

GRID_W = 64
CTX_LEN = 256

CONV_DIM = 256
ATT_HEADS = 8
ATT_KV_HEADS = 2
ATT_HEAD_DIM = 64
ATT_WINDOW = 128
ATT_BLOCK = 128
ROPE_BASE = 10000.0
MLSTM_HEADS = 4
MLSTM_HEAD_DIM = 64
MLSTM_DIM = MLSTM_HEADS * MLSTM_HEAD_DIM
MLSTM_CHUNK = 64
N_BRANCHES = 3
N_GROUPS = 4
EXPERTS_PER_GROUP = 8
N_EXPERTS = N_GROUPS * EXPERTS_PER_GROUP
TOP_K = 2
D_EXPERT = D_MODEL // 2
EXPERT_BLOCK = 128
DEEPNORM_ALPHA = (2 * DEPTH) ** 0.25
DEEPNORM_BETA = (8 * DEPTH) ** -0.25
LN_EPS = 1e-6
NEG_INF = -1e30

SPLIT_SIZES = (
    CONV_DIM, CONV_DIM, CONV_DIM,
    ATT_HEADS * ATT_HEAD_DIM,
    ATT_KV_HEADS * ATT_HEAD_DIM,
    ATT_KV_HEADS * ATT_HEAD_DIM,
    MLSTM_DIM, MLSTM_DIM, MLSTM_DIM, MLSTM_DIM,
    4 * MLSTM_HEADS,
    N_BRANCHES * D_MODEL,
)
IN_COLS = sum(SPLIT_SIZES)

kernel_name = 'hybrid_flow_block'


def _layer_norm(x, gain=None, bias=None):
    xf = x.astype(jnp.float32)
    mu = jnp.mean(xf, axis=-1, keepdims=True)
    var = jnp.mean(jnp.square(xf - mu), axis=-1, keepdims=True)
    y = ((xf - mu) * lax.rsqrt(var + LN_EPS)).astype(x.dtype)
    if gain is not None:
        y = y * gain + bias
    return y


def _modulate(h, shift, scale):
    return h * (1 + scale) + shift


def _split_cols(z):
    offs = np.cumsum(SPLIT_SIZES)[:-1].tolist()
    return jnp.split(z, offs, axis=-1)


def _short_conv(u, w):
    up = jnp.pad(u, ((0, 0), (1, 1), (0, 0)))
    return up[:, :-2] * w[0] + up[:, 1:-1] * w[1] + up[:, 2:] * w[2]


def _rope_tables(n_tokens):
    rows = n_tokens // GRID_W
    pos_r = jnp.repeat(jnp.arange(rows, dtype=jnp.float32), GRID_W)
    pos_c = jnp.tile(jnp.arange(GRID_W, dtype=jnp.float32), rows)
    nf = ATT_HEAD_DIM // 4
    inv = ROPE_BASE ** (-jnp.arange(nf, dtype=jnp.float32) / nf)
    ang_r = pos_r[:, None] * inv
    ang_c = pos_c[:, None] * inv
    return (jnp.cos(ang_r), jnp.sin(ang_r), jnp.cos(ang_c), jnp.sin(ang_c))


def _rotate_half(x, cos, sin):
    x1, x2 = jnp.split(x, 2, axis=-1)
    cos = cos[None, :, None, :].astype(x.dtype)
    sin = sin[None, :, None, :].astype(x.dtype)
    return jnp.concatenate([x1 * cos - x2 * sin, x2 * cos + x1 * sin], axis=-1)


def _rope_2d(x, tabs):
    cr, sr, cc, sc = tabs
    xr, xc = jnp.split(x, 2, axis=-1)
    return jnp.concatenate([_rotate_half(xr, cr, sr), _rotate_half(xc, cc, sc)], axis=-1)


def _window_attention(q, k, v, kc, vc, sink):
    B, L, H, dh = q.shape
    G = H // ATT_KV_HEADS
    nb = L // ATT_BLOCK
    n_ctx = kc.shape[1]
    scale = dh ** -0.5
    qb = q.reshape(B, nb, ATT_BLOCK, ATT_KV_HEADS, G, dh)
    pad = ((0, 0), (ATT_BLOCK, ATT_BLOCK), (0, 0), (0, 0))
    kp = jnp.pad(k, pad).reshape(B, nb + 2, ATT_BLOCK, ATT_KV_HEADS, dh)
    vp = jnp.pad(v, pad).reshape(B, nb + 2, ATT_BLOCK, ATT_KV_HEADS, dh)
    kw = jnp.concatenate([kp[:, :-2], kp[:, 1:-1], kp[:, 2:]], axis=2)
    vw = jnp.concatenate([vp[:, :-2], vp[:, 1:-1], vp[:, 2:]], axis=2)
    sink_l = jnp.broadcast_to(
        sink.astype(jnp.float32).reshape(ATT_KV_HEADS, G)[None, :, :, None, None],
        (B, ATT_KV_HEADS, G, ATT_BLOCK, 1))
    offs_q = jnp.arange(ATT_BLOCK)
    offs_k = jnp.arange(3 * ATT_BLOCK) - ATT_BLOCK

    def block(args):
        qi, ki, vi, bi = args
        qpos = bi * ATT_BLOCK + offs_q
        kpos = bi * ATT_BLOCK + offs_k
        valid = ((jnp.abs(qpos[:, None] - kpos[None, :]) <= ATT_WINDOW)
                 & (kpos >= 0)[None, :] & (kpos < L)[None, :])
        s_w = jnp.einsum('bqkgd,bskd->bkgqs', qi, ki, preferred_element_type=jnp.float32) * scale
        s_w = jnp.where(valid, s_w, NEG_INF)
        s_c = jnp.einsum('bqkgd,bskd->bkgqs', qi, kc, preferred_element_type=jnp.float32) * scale
        p = jax.nn.softmax(jnp.concatenate([s_c, s_w, sink_l], axis=-1), axis=-1)
        o = (jnp.einsum('bkgqs,bskd->bqkgd', p[..., :n_ctx].astype(vc.dtype), vc)
             + jnp.einsum('bkgqs,bskd->bqkgd', p[..., n_ctx:-1].astype(vi.dtype), vi))
        return o

    xs = (jnp.moveaxis(qb, 1, 0), jnp.moveaxis(kw, 1, 0), jnp.moveaxis(vw, 1, 0), jnp.arange(nb))
    o = lax.map(block, xs)
    return jnp.moveaxis(o, 0, 1).reshape(B, L, H * dh)


def _context_attention(q, k, v, sink):
    B, C, H, dh = q.shape
    G = H // ATT_KV_HEADS
    qg = q.reshape(B, C, ATT_KV_HEADS, G, dh)
    s = jnp.einsum('bqkgd,bskd->bkgqs', qg, k, preferred_element_type=jnp.float32) * dh ** -0.5
    s_sink = jnp.broadcast_to(
        sink.astype(jnp.float32).reshape(ATT_KV_HEADS, G)[None, :, :, None, None],
        (B, ATT_KV_HEADS, G, C, 1))
    p = jax.nn.softmax(jnp.concatenate([s, s_sink], axis=-1), axis=-1)[..., :-1]
    o = jnp.einsum('bkgqs,bskd->bqkgd', p.astype(v.dtype), v)
    return o.reshape(B, C, H * dh)


def _mlstm_chunkwise(q, k, v, li, lf, skip_chunks):
    B, T, H, dh = q.shape
    nc = T // MLSTM_CHUNK
    f32 = jnp.float32

    def chunks(t):
        return jnp.moveaxis(t.astype(f32).reshape(B, nc, MLSTM_CHUNK, H, -1), 3, 1)

    def gchunks(t):
        return jnp.moveaxis(t.astype(f32).reshape(B, nc, MLSTM_CHUNK, H), 3, 1)

    q, k, v = chunks(q), chunks(k) * dh ** -0.5, chunks(v)
    li, lf = gchunks(li), gchunks(lf)
    bcum = jnp.cumsum(lf, axis=-1)
    g = bcum[..., -1]
    w_loc = g[..., None] - bcum + li
    m_loc = jnp.max(w_loc, axis=-1)
    e_loc = jnp.exp(w_loc - m_loc[..., None])
    c_loc = jnp.einsum('bhncd,bhnce->bhnde', k * e_loc[..., None], v)
    n_loc = jnp.einsum('bhncd,bhnc->bhnd', k, e_loc)

    def step(carry, inp):
        c_st, n_st, m_st = carry
        g_j, c_j, n_j, m_j = inp
        m_new = jnp.maximum(g_j + m_st, m_j)
        a = jnp.exp(g_j + m_st - m_new)
        b = jnp.exp(m_j - m_new)
        c_new = a[..., None, None] * c_st + b[..., None, None] * c_j
        n_new = a[..., None] * n_st + b[..., None] * n_j
        return (c_new, n_new, m_new), (c_st, n_st, m_st)

    init = (jnp.zeros((B, H, dh, dh), f32), jnp.zeros((B, H, dh), f32), jnp.zeros((B, H), f32))
    xs = (jnp.moveaxis(g, 2, 0), jnp.moveaxis(c_loc, 2, 0), jnp.moveaxis(n_loc, 2, 0), jnp.moveaxis(m_loc, 2, 0))
    _, (c_prev, n_prev, m_prev) = lax.scan(step, init, xs)
    c_prev = jnp.moveaxis(c_prev, 0, 2)
    n_prev = jnp.moveaxis(n_prev, 0, 2)
    m_prev = jnp.moveaxis(m_prev, 0, 2)

    s = skip_chunks
    q, k, v, li, bcum = q[:, :, s:], k[:, :, s:], v[:, :, s:], li[:, :, s:], bcum[:, :, s:]
    c_prev, n_prev, m_prev = c_prev[:, :, s:], n_prev[:, :, s:], m_prev[:, :, s:]

    a = bcum + m_prev[..., None]
    dmat = bcum[..., :, None] - bcum[..., None, :] + li[..., None, :]
    lower = jnp.tril(jnp.ones((MLSTM_CHUNK, MLSTM_CHUNK), dtype=bool))
    dmat = jnp.where(lower, dmat, -jnp.inf)
    m = jnp.maximum(a, jnp.max(dmat, axis=-1))
    w_intra = jnp.exp(dmat - m[..., None])
    e_inter = jnp.exp(a - m)
    s_qk = jnp.einsum('bhnsd,bhnrd->bhnsr', q, k) * w_intra
    num = (jnp.einsum('bhnsr,bhnre->bhnse', s_qk, v)
           + e_inter[..., None] * jnp.einsum('bhnsd,bhnde->bhnse', q, c_prev))
    den = jnp.sum(s_qk, axis=-1) + e_inter * jnp.einsum('bhnsd,bhnd->bhns', q, n_prev)
    h = num / jnp.maximum(jnp.abs(den), jnp.exp(-m))[..., None]
    return jnp.moveaxis(h, 1, 3).reshape(B, -1, H, dh)


def _head_norm(h, w):
    hf = h.astype(jnp.float32)
    mu = jnp.mean(hf, axis=-1, keepdims=True)
    var = jnp.mean(jnp.square(hf - mu), axis=-1, keepdims=True)
    y = (hf - mu) * lax.rsqrt(var + LN_EPS)
    return y.reshape(h.shape[0], h.shape[1], -1) * w.astype(jnp.float32)


def _token_mixer(hl, hc, w_in, conv_w, attn_sink, gate_b, norm_w, w_pa, w_pb, w_pc, w_o, rope, need_ctx):
    B, L, _ = hl.shape
    n_ctx = hc.shape[1]
    zl = _split_cols(hl @ w_in)
    zc = _split_cols(hc @ w_in)

    def conv_mixer(z):
        return z[0] * _short_conv(z[1] * z[2], conv_w)

    ql = _rope_2d(zl[3].reshape(B, L, ATT_HEADS, ATT_HEAD_DIM), rope)
    kl = _rope_2d(zl[4].reshape(B, L, ATT_KV_HEADS, ATT_HEAD_DIM), rope)
    vl = zl[5].reshape(B, L, ATT_KV_HEADS, ATT_HEAD_DIM)
    qc = zc[3].reshape(B, n_ctx, ATT_HEADS, ATT_HEAD_DIM)
    kc = zc[4].reshape(B, n_ctx, ATT_KV_HEADS, ATT_HEAD_DIM)
    vc = zc[5].reshape(B, n_ctx, ATT_KV_HEADS, ATT_HEAD_DIM)

    def heads(t):
        return t.reshape(t.shape[0], t.shape[1], MLSTM_HEADS, MLSTM_HEAD_DIM)

    def seq(tc, tl, rev):
        if rev:
            return jnp.concatenate([jnp.flip(tc, axis=1), jnp.flip(tl, axis=1)], axis=1)
        return jnp.concatenate([tc, tl], axis=1)

    gates_l = zl[10].astype(jnp.float32) + gate_b.astype(jnp.float32)
    gates_c = zc[10].astype(jnp.float32) + gate_b.astype(jnp.float32)
    skip = 0 if need_ctx else n_ctx // MLSTM_CHUNK
    outs = []
    for d, rev in enumerate((False, True)):
        i_sl = slice(2 * d * MLSTM_HEADS, (2 * d + 1) * MLSTM_HEADS)
        f_sl = slice((2 * d + 1) * MLSTM_HEADS, (2 * d + 2) * MLSTM_HEADS)
        q_d = seq(heads(zc[6]), heads(zl[6]), rev)
        k_d = seq(heads(zc[7]), heads(zl[7]), rev)
        v_d = seq(heads(zc[8]), heads(zl[8]), rev)
        li_d = seq(gates_c[..., i_sl], gates_l[..., i_sl], rev)
        lf_d = jax.nn.log_sigmoid(seq(gates_c[..., f_sl], gates_l[..., f_sl], rev))
        outs.append(_mlstm_chunkwise(q_d, k_d, v_d, li_d, lf_d, skip))
    h_fwd, h_bwd = outs
    hm_l = h_fwd[:, -L:] + jnp.flip(h_bwd[:, -L:], axis=1)
    ym_l = jax.nn.sigmoid(zl[9]) * _head_norm(hm_l, norm_w).astype(hl.dtype)

    def merge(ya, yb, yc, g):
        g_a, g_b, g_c = jnp.split(jax.nn.sigmoid(g), N_BRANCHES, axis=-1)
        return (g_a * (ya @ w_pa) + g_b * (yb @ w_pb) + g_c * (yc @ w_pc)) @ w_o

    yl = merge(conv_mixer(zl), _window_attention(ql, kl, vl, kc, vc, attn_sink), ym_l, zl[11])
    if not need_ctx:
        return yl, None
    hm_c = h_fwd[:, :n_ctx] + jnp.flip(h_bwd[:, :n_ctx], axis=1)
    ym_c = jax.nn.sigmoid(zc[9]) * _head_norm(hm_c, norm_w).astype(hc.dtype)
    yc = merge(conv_mixer(zc), _context_attention(qc, kc, vc, attn_sink), ym_c, zc[11])
    return yl, yc


def _hier_moe(h, w_rg, b_rg, w_re, b_re, w_ei, w_eo):
    N, D = h.shape
    A = N * TOP_K
    P = -(-(A + N_EXPERTS * EXPERT_BLOCK) // EXPERT_BLOCK) * EXPERT_BLOCK
    NB = P // EXPERT_BLOCK
    rows = jnp.arange(N)
    g_logit = (h @ w_rg + b_rg).astype(jnp.float32)
    g_prob = jax.nn.softmax(g_logit, axis=-1)
    g_sel = jnp.argmax(g_logit, axis=-1).astype(jnp.int32)
    e_logit = (h @ w_re + b_re).astype(jnp.float32).reshape(N, N_GROUPS, EXPERTS_PER_GROUP)[rows, g_sel]
    top_l, top_i = lax.top_k(e_logit, TOP_K)
    wts = (jax.nn.softmax(top_l, axis=-1) * g_prob[rows, g_sel][:, None]).reshape(A)
    eid = (g_sel[:, None] * EXPERTS_PER_GROUP + top_i).reshape(A)
    tok = jnp.repeat(rows, TOP_K)
    order = jnp.argsort(eid)
    eid_s, tok_s, w_s = eid[order], tok[order], wts[order]
    counts = jnp.bincount(eid, length=N_EXPERTS)
    padded = (counts + EXPERT_BLOCK - 1) // EXPERT_BLOCK * EXPERT_BLOCK
    pad_end = jnp.cumsum(padded)
    pad_start = pad_end - padded
    start = jnp.cumsum(counts) - counts
    pos = pad_start[eid_s] + jnp.arange(A) - start[eid_s]
    buf_tok = jnp.full((P,), N, dtype=jnp.int32).at[pos].set(tok_s)
    buf_w = jnp.zeros((P,), h.dtype).at[pos].set(w_s.astype(h.dtype))
    blk_e = jnp.minimum(jnp.searchsorted(pad_end, jnp.arange(NB) * EXPERT_BLOCK, side='right'), N_EXPERTS - 1)
    h_pad = jnp.concatenate([h, jnp.zeros((1, D), h.dtype)], axis=0)
    xb = h_pad[buf_tok].reshape(NB, EXPERT_BLOCK, D)

    def expert_block(args):
        xblk, e = args
        gt, up = jnp.split(xblk @ w_ei[e], 2, axis=-1)
        return (jax.nn.silu(gt) * up) @ w_eo[e]

    yb = lax.map(expert_block, (xb, blk_e)).reshape(P, D)
    out = jax.ops.segment_sum(yb * buf_w[:, None], buf_tok, num_segments=N + 1)
    return out[:N]


def setup_inputs(seed: int = 0) -> dict:
    key = jax.random.key(seed)
    ks = jax.random.split(key, 26)

    def nrm(k, shape):
        return jax.random.normal(k, shape, jnp.float32)

    D = D_MODEL
    gate_offset = jnp.concatenate([
        jnp.zeros((MLSTM_HEADS,), jnp.float32), jnp.linspace(3.0, 6.0, MLSTM_HEADS, dtype=jnp.float32),
        jnp.zeros((MLSTM_HEADS,), jnp.float32), jnp.linspace(3.0, 6.0, MLSTM_HEADS, dtype=jnp.float32)])
    return {
        'x': nrm(ks[0], (BATCH, SEQ, D)),
        'c': nrm(ks[1], (BATCH, D)),
        'ctx': nrm(ks[2], (BATCH, CTX_LEN, D)),
        'c_ctx': nrm(ks[3], (D,)),
        'w_ada': nrm(ks[4], (DEPTH, D, 6 * D)) * (0.5 * D ** -0.5),
        'b_ada': 0.01 * nrm(ks[5], (DEPTH, 6 * D)),
        'w_in': nrm(ks[6], (DEPTH, D, IN_COLS)) * D ** -0.5,
        'conv_w': nrm(ks[7], (DEPTH, 3, CONV_DIM)) * 3 ** -0.5,
        'attn_sink': 0.5 * nrm(ks[8], (DEPTH, ATT_HEADS)),
        'mlstm_gate_b': gate_offset[None, :] + 0.1 * nrm(ks[9], (DEPTH, 4 * MLSTM_HEADS)),
        'mlstm_norm_w': 1.0 + 0.02 * nrm(ks[10], (DEPTH, MLSTM_DIM)),
        'w_proj_a': nrm(ks[11], (DEPTH, CONV_DIM, D)) * (CONV_DIM ** -0.5 * DEEPNORM_BETA),
        'w_proj_b': nrm(ks[12], (DEPTH, ATT_HEADS * ATT_HEAD_DIM, D)) * ((ATT_HEADS * ATT_HEAD_DIM) ** -0.5 * DEEPNORM_BETA),
        'w_proj_c': nrm(ks[13], (DEPTH, MLSTM_DIM, D)) * (MLSTM_DIM ** -0.5 * DEEPNORM_BETA),
        'w_out': nrm(ks[14], (DEPTH, D, D)) * (D ** -0.5 * DEEPNORM_BETA),
        'ln1_g': 1.0 + 0.02 * nrm(ks[15], (DEPTH, D)),
        'ln1_b': 0.02 * nrm(ks[16], (DEPTH, D)),
        'w_route_group': nrm(ks[17], (DEPTH, D, N_GROUPS)) * D ** -0.5,
        'b_route_group': 0.01 * nrm(ks[18], (DEPTH, N_GROUPS)),
        'w_route_expert': nrm(ks[19], (DEPTH, D, N_EXPERTS)) * D ** -0.5,
        'b_route_expert': 0.01 * nrm(ks[20], (DEPTH, N_EXPERTS)),
        'w_expert_in': nrm(ks[21], (DEPTH, N_EXPERTS, D, 2 * D_EXPERT)) * D ** -0.5,
        'w_expert_out': nrm(ks[22], (DEPTH, N_EXPERTS, D_EXPERT, D)) * (D_EXPERT ** -0.5 * DEEPNORM_BETA),
        'ln2_g': 1.0 + 0.02 * nrm(ks[23], (DEPTH, D)),
        'ln2_b': 0.02 * nrm(ks[24], (DEPTH, D)),
    }


def reference(x, c, ctx, c_ctx, w_ada, b_ada, w_in, conv_w, attn_sink, mlstm_gate_b, mlstm_norm_w,
              w_proj_a, w_proj_b, w_proj_c, w_out, ln1_g, ln1_b, w_route_group, b_route_group,
              w_route_expert, b_route_expert, w_expert_in, w_expert_out, ln2_g, ln2_b):
    B, L, D = x.shape
    n_ctx = ctx.shape[1]
    rope = _rope_tables(L)
    s_lat = jax.nn.silu(c)
    s_ctx = jax.nn.silu(c_ctx)
    for i in range(DEPTH):
        need_ctx = i < DEPTH - 1
        mod_l = jnp.split((s_lat @ w_ada[i] + b_ada[i])[:, None, :], 6, axis=-1)
        mod_c = jnp.split(s_ctx @ w_ada[i] + b_ada[i], 6, axis=-1)
        hl = _modulate(_layer_norm(x), mod_l[0], mod_l[1])
        hc = _modulate(_layer_norm(ctx), mod_c[0], mod_c[1])
        yl, yc = _token_mixer(hl, hc, w_in[i], conv_w[i], attn_sink[i], mlstm_gate_b[i], mlstm_norm_w[i],
                              w_proj_a[i], w_proj_b[i], w_proj_c[i], w_out[i], rope, need_ctx)
        x = _layer_norm(DEEPNORM_ALPHA * x + mod_l[2] * yl, ln1_g[i], ln1_b[i])
        hl2 = _modulate(_layer_norm(x), mod_l[3], mod_l[4]).reshape(B * L, D)
        if need_ctx:
            ctx = _layer_norm(DEEPNORM_ALPHA * ctx + mod_c[2] * yc, ln1_g[i], ln1_b[i])
            hc2 = _modulate(_layer_norm(ctx), mod_c[3], mod_c[4]).reshape(B * n_ctx, D)
            f = _hier_moe(jnp.concatenate([hl2, hc2], axis=0), w_route_group[i], b_route_group[i],
                          w_route_expert[i], b_route_expert[i], w_expert_in[i], w_expert_out[i])
            fl = f[:B * L].reshape(B, L, D)
            fc = f[B * L:].reshape(B, n_ctx, D)
            ctx = _layer_norm(DEEPNORM_ALPHA * ctx + mod_c[5] * fc, ln2_g[i], ln2_b[i])
        else:
            fl = _hier_moe(hl2, w_route_group[i], b_route_group[i], w_route_expert[i],
                           b_route_expert[i], w_expert_in[i], w_expert_out[i]).reshape(B, L, D)
        x = _layer_norm(DEEPNORM_ALPHA * x + mod_l[5] * fl, ln2_g[i], ln2_b[i])
    return x
```

```python
import functools

import jax
import jax.numpy as jnp
import numpy as np
from jax import lax
from jax.experimental import pallas as pl
from jax.experimental.pallas import tpu as pltpu

D_MODEL = 1024
GRID_W = 64
CONV_DIM = 256
ATT_HEADS = 8
ATT_KV_HEADS = 2
ATT_HEAD_DIM = 64
ATT_WINDOW = 128
ROPE_BASE = 10000.0
MLSTM_HEADS = 4
MLSTM_HEAD_DIM = 64
MLSTM_DIM = MLSTM_HEADS * MLSTM_HEAD_DIM
N_GROUPS = 4
EXPERTS_PER_GROUP = 8
N_EXPERTS = N_GROUPS * EXPERTS_PER_GROUP
TOP_K = 2
D_EXPERT = D_MODEL // 2
LN_EPS = 1e-6
NEG_INF = -1e30

ATT_Q_DIM = ATT_HEADS * ATT_HEAD_DIM
ATT_KV_DIM = ATT_KV_HEADS * ATT_HEAD_DIM
N_GATE = 4 * MLSTM_HEADS
MIX_COLS = 3 * CONV_DIM + ATT_Q_DIM + 2 * ATT_KV_DIM + 4 * MLSTM_DIM
N_ROUTE = N_GROUPS + N_EXPERTS

LANES = 128
VMEM_LIMIT = 56 * 1024 * 1024
TOK_TILE = 256
ATT_QB = 128
MLSTM_CH = 128
EXP_BLK = 256
HALO = 16

F32 = jnp.float32
BF16 = jnp.bfloat16


def _cparams(sem):
    return pltpu.CompilerParams(dimension_semantics=sem, vmem_limit_bytes=VMEM_LIMIT)


def _ln(x):
    mu = jnp.mean(x, axis=-1, keepdims=True)
    xc = x - mu
    var = jnp.mean(xc * xc, axis=-1, keepdims=True)
    return xc * lax.rsqrt(var + LN_EPS)


def _sigmoid(x):
    return 1.0 / (1.0 + jnp.exp(-x))


def _split3(x):
    hi = x.astype(BF16)
    r1 = x - hi.astype(F32)
    mid = r1.astype(BF16)
    lo = (r1 - mid.astype(F32)).astype(BF16)
    return hi, mid, lo


def _dot(a, b):
    return jnp.dot(a, b, preferred_element_type=F32)


def _dot_nt(a, b):
    return lax.dot_general(a, b, (((1,), (1,)), ((), ())), preferred_element_type=F32)


def _dot_tn(a, b):
    return lax.dot_general(a, b, (((0,), (0,)), ((), ())), preferred_element_type=F32)


def _ada_kernel(c_ref, w_ref, b_ref, o_ref):
    cv = c_ref[...]
    s = cv * _sigmoid(cv)
    w = w_ref[0]
    acc = jnp.zeros((s.shape[0], w.shape[1]), F32)
    w3 = _split3(w)
    for sp in _split3(s):
        for wp in w3:
            acc = acc + _dot(sp, wp)
    o_ref[0] = acc + b_ref[0]


def _ada(cond, w_ada, b_ada):
    depth, d, n6 = w_ada.shape
    rows = cond.shape[0]
    nt = n6 // d
    return pl.pallas_call(
        _ada_kernel,
        grid=(depth, nt),
        in_specs=[
            pl.BlockSpec((rows, d), lambda l, j: (0, 0)),
            pl.BlockSpec((1, d, d), lambda l, j: (l, 0, j)),
            pl.BlockSpec((1, 1, d), lambda l, j: (l, 0, j)),
        ],
        out_specs=pl.BlockSpec((1, rows, d), lambda l, j: (l, 0, j)),
        out_shape=jax.ShapeDtypeStruct((depth, rows, n6), F32),
        compiler_params=_cparams(("arbitrary", "arbitrary")),
        name="ada_mod",
    )(cond, w_ada, b_ada.reshape(depth, 1, n6))


def _rope(x, cos, sin_signed, lane):
    swapped = jnp.where(lane % 32 < 16,
                        pltpu.roll(x, LANES - 16, axis=1),
                        pltpu.roll(x, 16, axis=1))
    return x * cos + swapped * sin_signed


def _inproj_kernel(x_ref, mod_ref, w_ref, cos_ref, sin_ref,
                   a0u_ref, q_ref, kv_ref, mqkv_ref, mo_ref, g_ref):
    x = x_ref[0]
    shift = mod_ref[0, 0:1, :]
    scale = mod_ref[0, 1:2, :]
    h = (_ln(x) * (1.0 + scale) + shift).astype(BF16)
    cos = cos_ref[...]
    sin = sin_ref[...]
    lane = lax.broadcasted_iota(jnp.int32, cos.shape, 1)

    def cols(lo, n):
        return _dot(h, w_ref[:, lo:lo + n])

    za = cols(0, 3 * CONV_DIM)
    a0u_ref[0, :, 0:CONV_DIM] = za[:, 0:CONV_DIM].astype(BF16)
    a0u_ref[0, :, CONV_DIM:2 * CONV_DIM] = (
        za[:, CONV_DIM:2 * CONV_DIM] * za[:, 2 * CONV_DIM:3 * CONV_DIM]).astype(BF16)
    off = 3 * CONV_DIM
    zq = cols(off, ATT_Q_DIM)
    qscale = ATT_HEAD_DIM ** -0.5
    for j in range(ATT_Q_DIM // LANES):
        piece = _rope(zq[:, j * LANES:(j + 1) * LANES], cos, sin, lane)
        q_ref[0, :, j * LANES:(j + 1) * LANES] = (piece * qscale).astype(BF16)
    off += ATT_Q_DIM
    zkv = cols(off, 2 * ATT_KV_DIM)
    kv_ref[0, :, 0:ATT_KV_DIM] = _rope(zkv[:, 0:ATT_KV_DIM], cos, sin, lane).astype(BF16)
    kv_ref[0, :, ATT_KV_DIM:] = zkv[:, ATT_KV_DIM:].astype(BF16)
    off += 2 * ATT_KV_DIM
    zm = cols(off, 3 * MLSTM_DIM)
    mqkv_ref[0, :, 0:MLSTM_DIM] = zm[:, 0:MLSTM_DIM].astype(BF16)
    mqkv_ref[0, :, MLSTM_DIM:2 * MLSTM_DIM] = (
        zm[:, MLSTM_DIM:2 * MLSTM_DIM] * (MLSTM_HEAD_DIM ** -0.5)).astype(BF16)
    mqkv_ref[0, :, 2 * MLSTM_DIM:] = zm[:, 2 * MLSTM_DIM:].astype(BF16)
    off += 3 * MLSTM_DIM
    mo_ref[0] = cols(off, MLSTM_DIM).astype(BF16)
    off += MLSTM_DIM
    g_ref[0] = cols(off, LANES)


def _inproj(xa, mod, w_mix, cos_t, sin_t, n_ctx):
    B, T, D = xa.shape
    tm = TOK_TILE
    nct = n_ctx // tm
    wcols = w_mix.shape[1]

    def tok(width):
        return pl.BlockSpec((1, tm, width), lambda b, i: (b, i, 0))

    outs = [(2 * CONV_DIM, BF16), (ATT_Q_DIM, BF16), (2 * ATT_KV_DIM, BF16),
            (3 * MLSTM_DIM, BF16), (MLSTM_DIM, BF16), (LANES, F32)]
    return pl.pallas_call(
        _inproj_kernel,
        grid=(B, T // tm),
        in_specs=[
            tok(D),
            pl.BlockSpec((1, 6, D), lambda b, i: (jnp.where(i < nct, B, b), 0, 0)),
            pl.BlockSpec((D, wcols), lambda b, i: (0, 0)),
            pl.BlockSpec((tm, LANES), lambda b, i: (i, 0)),
            pl.BlockSpec((tm, LANES), lambda b, i: (i, 0)),
        ],
        out_specs=[tok(w) for w, _ in outs],
        out_shape=[jax.ShapeDtypeStruct((B, T, w), dt) for w, dt in outs],
        compiler_params=_cparams(("parallel", "arbitrary")),
        name="in_proj",
    )(xa, mod, w_mix, cos_t, sin_t)


def _attn_kernel(q_ref, kvc_ref, kvp_ref, kvm_ref, kvn_ref, sink_ref, o_ref, *, n_ctx_blk):
    j = pl.program_id(1)
    nq = pl.num_programs(1)
    is_lat = j >= n_ctx_blk
    has_prev = j >= n_ctx_blk + 1
    has_next = j < nq - 1
    n_ctx = kvc_ref.shape[1]
    qb = q_ref.shape[1]
    nk = n_ctx + 3 * qb

    kv_all = jnp.concatenate([kvc_ref[0], kvp_ref[0], kvm_ref[0], kvn_ref[0]], axis=0)
    qi = lax.broadcasted_iota(jnp.int32, (qb, nk), 0)
    ki = lax.broadcasted_iota(jnp.int32, (qb, nk), 1)
    kk = ki - n_ctx
    valid = ((ki < n_ctx)
             | (is_lat & (kk >= 2 * qb) & (kk - 2 * qb <= qi) & has_next)
             | (is_lat & (kk >= qb) & (kk < 2 * qb))
             | (is_lat & (kk >= 0) & (kk < qb) & (kk >= qi) & has_prev))
    q = q_ref[0]
    sink = sink_ref[...]
    group = ATT_HEADS // ATT_KV_HEADS
    outs = []
    for h in range(ATT_HEADS):
        kvh = h // group
        kh = kv_all[:, kvh * ATT_HEAD_DIM:(kvh + 1) * ATT_HEAD_DIM]
        vh = kv_all[:, ATT_KV_DIM + kvh * ATT_HEAD_DIM:ATT_KV_DIM + (kvh + 1) * ATT_HEAD_DIM]
        qh = q[:, h * ATT_HEAD_DIM:(h + 1) * ATT_HEAD_DIM]
        s = jnp.where(valid, _dot_nt(qh, kh), NEG_INF)
        sk = sink[:, h:h + 1]
        m = jnp.maximum(jnp.max(s, axis=-1, keepdims=True), sk)
        p = jnp.exp(s - m)
        denom = jnp.sum(p, axis=-1, keepdims=True) + jnp.exp(sk - m)
        outs.append(_dot(p.astype(BF16), vh) / denom)
    o_ref[0] = jnp.concatenate(outs, axis=-1).astype(o_ref.dtype)


def _attention(q, kv, sink, n_ctx):
    B, T, _ = q.shape
    qb = ATT_QB
    nq = T // qb
    n_ctx_blk = n_ctx // qb
    kvw = kv.shape[2]
    return pl.pallas_call(
        functools.partial(_attn_kernel, n_ctx_blk=n_ctx_blk),
        grid=(B, nq),
        in_specs=[
            pl.BlockSpec((1, qb, ATT_Q_DIM), lambda b, j: (b, j, 0)),
            pl.BlockSpec((1, n_ctx, kvw), lambda b, j: (b, 0, 0)),
            pl.BlockSpec((1, qb, kvw), lambda b, j: (b, jnp.maximum(j - 1, 0), 0)),
            pl.BlockSpec((1, qb, kvw), lambda b, j: (b, j, 0)),
            pl.BlockSpec((1, qb, kvw), lambda b, j: (b, jnp.minimum(j + 1, nq - 1), 0)),
            pl.BlockSpec((1, ATT_HEADS), lambda b, j: (0, 0)),
        ],
        out_specs=pl.BlockSpec((1, qb, ATT_Q_DIM), lambda b, j: (b, j, 0)),
        out_shape=jax.ShapeDtypeStruct((B, T, ATT_Q_DIM), BF16),
        compiler_params=_cparams(("parallel", "arbitrary")),
        name="window_attn",
    )(q, kv, kv, kv, kv, sink.reshape(1, ATT_HEADS))


def _log_sigmoid(x):
    return jnp.minimum(x, 0.0) - jnp.log1p(jnp.exp(-jnp.abs(x)))


def _mlstm_kernel(qkv_f_ref, qkv_b_ref, gc_f_ref, gc_b_ref, gr_f_ref, gr_b_ref,
                  bcol_ref, brow_ref, hf_ref, hb_ref, c_scr, n_scr, m_scr):
    j = pl.program_id(1)
    ch = qkv_f_ref.shape[1]
    dh = MLSTM_HEAD_DIM

    @pl.when(j == 0)
    def _():
        c_scr[...] = jnp.zeros_like(c_scr)
        n_scr[...] = jnp.zeros_like(n_scr)
        m_scr[...] = jnp.zeros_like(m_scr)

    row_i = lax.broadcasted_iota(jnp.int32, (ch, ch), 0)
    col_i = lax.broadcasted_iota(jnp.int32, (ch, ch), 1)
    lower = (col_i <= row_i)
    upper = (col_i >= row_i)
    lower_b = lower.astype(BF16)
    upper_b = upper.astype(BF16)

    dirs = ((qkv_f_ref, gc_f_ref, gr_f_ref, hf_ref), (qkv_b_ref, gc_b_ref, gr_b_ref, hb_ref))
    for d, (qkv_ref, gc_ref, gr_ref, out_ref) in enumerate(dirs):
        qkv = qkv_ref[0]
        gcol = gc_ref[0] + bcol_ref[...]
        grow = gr_ref[0] + brow_ref[...]
        lf_col = _log_sigmoid(gcol)
        lf_row = _log_sigmoid(grow)
        seen = lower if d == 0 else upper
        seen_b = lower_b if d == 0 else upper_b
        seen_t = upper_b if d == 0 else lower_b
        cum_col = sum(_dot(seen_b, part) for part in _split3(lf_col))
        cum_row = sum(_dot(part, seen_t) for part in _split3(lf_row))
        last = ch - 1 if d == 0 else 0
        outs = []
        for h in range(MLSTM_HEADS):
            ci = 2 * d * MLSTM_HEADS + h
            cf = ci + MLSTM_HEADS
            idx = d * MLSTM_HEADS + h
            li_r = grow[ci:ci + 1, :]
            li_c = gcol[:, ci:ci + 1]
            b_r = cum_row[cf:cf + 1, :]
            b_c = cum_col[:, cf:cf + 1]
            g = b_r[:, last:last + 1]
            qh = qkv[:, h * dh:(h + 1) * dh]
            kh = qkv[:, MLSTM_DIM + h * dh:MLSTM_DIM + (h + 1) * dh]
            vh = qkv[:, 2 * MLSTM_DIM + h * dh:2 * MLSTM_DIM + (h + 1) * dh]
            c_prev = c_scr[idx]
            n_prev = n_scr[idx]
            m_prev = m_scr[idx][:, 0:1]

            a = b_c + m_prev
            dmat = jnp.where(seen, b_c - b_r + li_r, -jnp.inf)
            m = jnp.maximum(a, jnp.max(dmat, axis=-1, keepdims=True))
            w_intra = jnp.exp(dmat - m)
            e_inter = jnp.exp(a - m)
            s_qk = _dot_nt(qh, kh) * w_intra
            num = _dot(s_qk.astype(BF16), vh) + e_inter * _dot(qh, c_prev.astype(BF16))
            den = (jnp.sum(s_qk, axis=-1, keepdims=True)
                   + e_inter * jnp.sum(qh.astype(F32) * n_prev, axis=-1, keepdims=True))
            outs.append(num / jnp.maximum(jnp.abs(den), jnp.exp(-m)))

            w_loc = g - b_c + li_c
            m_loc = jnp.max(w_loc, axis=0, keepdims=True)
            e_loc = jnp.exp(w_loc - m_loc)
            ke = kh.astype(F32) * e_loc
            c_loc = _dot_tn(ke.astype(BF16), vh)
            n_loc = jnp.sum(ke, axis=0, keepdims=True)
            m_new = jnp.maximum(g + m_prev, m_loc)
            sa = jnp.exp(g + m_prev - m_new)
            sb = jnp.exp(m_loc - m_new)
            c_scr[idx] = sa * c_prev + sb * c_loc
            n_scr[idx] = sa * n_prev + sb * n_loc
            m_scr[idx] = jnp.broadcast_to(m_new, (1, LANES))
        out_ref[0] = jnp.concatenate(outs, axis=-1)


def _mlstm(mqkv, gcol, grow, gate_b, n_ctx):
    B, T, _ = mqkv.shape
    ch = MLSTM_CH
    nc = T // ch
    ncc = n_ctx // ch

    def rev(j):
        return jnp.where(j < ncc, ncc - 1 - j, nc - 1 - (j - ncc))

    nst = 2 * MLSTM_HEADS
    return pl.pallas_call(
        _mlstm_kernel,
        grid=(B, nc),
        in_specs=[
            pl.BlockSpec((1, ch, 3 * MLSTM_DIM), lambda b, j: (b, j, 0)),
            pl.BlockSpec((1, ch, 3 * MLSTM_DIM), lambda b, j: (b, rev(j), 0)),
            pl.BlockSpec((1, ch, N_GATE), lambda b, j: (b, j, 0)),
            pl.BlockSpec((1, ch, N_GATE), lambda b, j: (b, rev(j), 0)),
            pl.BlockSpec((1, N_GATE, ch), lambda b, j: (b, 0, j)),
            pl.BlockSpec((1, N_GATE, ch), lambda b, j: (b, 0, rev(j))),
            pl.BlockSpec((1, N_GATE), lambda b, j: (0, 0)),
            pl.BlockSpec((N_GATE, 1), lambda b, j: (0, 0)),
        ],
        out_specs=[
            pl.BlockSpec((1, ch, MLSTM_DIM), lambda b, j: (b, j, 0)),
            pl.BlockSpec((1, ch, MLSTM_DIM), lambda b, j: (b, rev(j), 0)),
        ],
        out_shape=[jax.ShapeDtypeStruct((B, T, MLSTM_DIM), F32)] * 2,
        scratch_shapes=[
            pltpu.VMEM((nst, MLSTM_HEAD_DIM, MLSTM_HEAD_DIM), F32),
            pltpu.VMEM((nst, 1, MLSTM_HEAD_DIM), F32),
            pltpu.VMEM((nst, 1, LANES), F32),
        ],
        compiler_params=_cparams(("parallel", "arbitrary")),
        name="mlstm",
    )(mqkv, mqkv, gcol, gcol, grow, grow, gate_b.reshape(1, N_GATE), gate_b.reshape(N_GATE, 1))


def _merge_kernel(x_ref, mod_ref, a0u_ref, up_ref, un_ref, yb_ref, hf_ref, hb_ref, mo_ref,
                  convw_ref, normw_ref, wg_ref, wpa_ref, wpb_ref, wpc_ref, wo_ref,
                  ln_g_ref, ln_b_ref, wr_ref, br_ref,
                  x1_ref, h2_ref, lg_ref, *, off, n_ctx_tiles, alpha):
    i = pl.program_id(1) + off
    nt = pl.num_programs(1) + off
    tm = x_ref.shape[1]
    x = x_ref[0]
    h = (_ln(x) * (1.0 + mod_ref[0, 1:2, :]) + mod_ref[0, 0:1, :]).astype(BF16)

    a0 = a0u_ref[0, :, 0:CONV_DIM].astype(F32)
    u = a0u_ref[0, :, CONV_DIM:].astype(F32)
    prev_ok = jnp.logical_and(i != 0, i != n_ctx_tiles)
    next_ok = jnp.logical_and(i != n_ctx_tiles - 1, i != nt - 1)
    u_prev = jnp.where(prev_ok, up_ref[0, HALO - 1:HALO, CONV_DIM:].astype(F32), 0.0)
    u_next = jnp.where(next_ok, un_ref[0, 0:1, CONV_DIM:].astype(F32), 0.0)
    row = lax.broadcasted_iota(jnp.int32, u.shape, 0)
    u_dn = jnp.where(row == 0, u_prev, pltpu.roll(u, 1, axis=0))
    u_up = jnp.where(row == tm - 1, u_next, pltpu.roll(u, tm - 1, axis=0))
    cw = convw_ref[...]
    ya = a0 * (u_dn * cw[0:1, :] + u * cw[1:2, :] + u_up * cw[2:3, :])

    hm = hf_ref[0] + hb_ref[0]
    normed = []
    for hd in range(MLSTM_HEADS):
        normed.append(_ln(hm[:, hd * MLSTM_HEAD_DIM:(hd + 1) * MLSTM_HEAD_DIM]))
    yc = _sigmoid(mo_ref[0].astype(F32)) * (jnp.concatenate(normed, axis=-1) * normw_ref[...])

    pa = _dot(ya.astype(BF16), wpa_ref[...])
    pb = _dot(yb_ref[0], wpb_ref[...])
    pc = _dot(yc.astype(BF16), wpc_ref[...])
    d = x.shape[1]
    merged = (_sigmoid(_dot(h, wg_ref[:, 0:d])) * pa
              + _sigmoid(_dot(h, wg_ref[:, d:2 * d])) * pb
              + _sigmoid(_dot(h, wg_ref[:, 2 * d:3 * d])) * pc)
    yl = _dot(merged.astype(BF16), wo_ref[...])

    x1 = _ln(alpha * x + mod_ref[0, 2:3, :] * yl) * ln_g_ref[...] + ln_b_ref[...]
    x1_ref[0] = x1
    h2 = _ln(x1) * (1.0 + mod_ref[0, 4:5, :]) + mod_ref[0, 3:4, :]
    h2_ref[0] = h2
    h_hi = h2.astype(BF16)
    h_lo = (h2 - h_hi.astype(F32)).astype(BF16)
    wr = wr_ref[...]
    w_hi = wr.astype(BF16)
    w_lo = (wr - w_hi.astype(F32)).astype(BF16)
    lg_ref[0] = _dot(h_hi, w_hi) + _dot(h_lo, w_hi) + _dot(h_hi, w_lo) + br_ref[...]


def _merge(xa, mod, a0u, yb, hf, hb, mo, conv_w, norm_w, w_gate, w_pa, w_pb, w_pc, w_o,
           ln_g, ln_b, w_route, b_route, n_ctx, off, alpha):
    B, T, D = xa.shape
    tm = TOK_TILE
    nct = n_ctx // tm
    nt = T // tm - off
    tn = nt * tm
    hpt = tm // HALO
    nhalo = T // HALO

    def tok(width):
        return pl.BlockSpec((1, tm, width), lambda b, i: (b, i + off, 0))

    def full(a):
        return pl.BlockSpec(a.shape, lambda b, i: (0,) * a.ndim)

    def otok(width):
        return pl.BlockSpec((1, tm, width), lambda b, i: (b, i, 0))

    consts = [conv_w, norm_w, w_gate, w_pa, w_pb, w_pc, w_o, ln_g, ln_b, w_route, b_route]
    return pl.pallas_call(
        functools.partial(_merge_kernel, off=off, n_ctx_tiles=nct, alpha=alpha),
        grid=(B, nt),
        in_specs=[
            tok(D),
            pl.BlockSpec((1, 6, D), lambda b, i: (jnp.where(i + off < nct, B, b), 0, 0)),
            tok(2 * CONV_DIM),
            pl.BlockSpec((1, HALO, 2 * CONV_DIM),
                         lambda b, i: (b, jnp.maximum((i + off) * hpt - 1, 0), 0)),
            pl.BlockSpec((1, HALO, 2 * CONV_DIM),
                         lambda b, i: (b, jnp.minimum((i + off + 1) * hpt, nhalo - 1), 0)),
            tok(ATT_Q_DIM), tok(MLSTM_DIM), tok(MLSTM_DIM), tok(MLSTM_DIM),
        ] + [full(a) for a in consts],
        out_specs=[otok(D), otok(D), otok(LANES)],
        out_shape=[jax.ShapeDtypeStruct((B, tn, D), F32),
                   jax.ShapeDtypeStruct((B, tn, D), F32),
                   jax.ShapeDtypeStruct((B, tn, LANES), F32)],
        compiler_params=_cparams(("parallel", "arbitrary")),
        name="merge",
    )(xa, mod, a0u, a0u, a0u, yb, hf, hb, mo, *consts)


def _expert_kernel(tok_ref, blke_ref, nblk_ref, h_hbm, wi_ref, wo_ref, y_ref,
                   xbuf, wi_bf, wo_bf, sem):
    i = pl.program_id(0)
    blk = xbuf.shape[1]
    nb = nblk_ref[0]

    def start_gather(b, slot):
        def body(r, carry):
            t = tok_ref[b * blk + r]
            pltpu.make_async_copy(h_hbm.at[pl.ds(t, 1), :],
                                  xbuf.at[slot, pl.ds(r, 1), :], sem.at[slot]).start()
            return carry
        lax.fori_loop(0, blk, body, 0)

    @pl.when(jnp.logical_and(i == 0, nb > 0))
    def _():
        start_gather(0, 0)

    @pl.when(i + 1 < nb)
    def _():
        start_gather(i + 1, (i + 1) % 2)

    @pl.when(i < nb)
    def _():
        slot = i % 2
        pltpu.make_async_copy(h_hbm.at[pl.ds(0, blk), :], xbuf.at[slot], sem.at[slot]).wait()
        e_now = blke_ref[i]
        e_before = blke_ref[jnp.maximum(i - 1, 0)]

        @pl.when(jnp.logical_or(i == 0, e_now != e_before))
        def _():
            wi_bf[...] = wi_ref[0].astype(BF16)
            wo_bf[...] = wo_ref[0].astype(BF16)

        xin = xbuf[slot].astype(BF16)
        mid = _dot(xin, wi_bf[...])
        gt = mid[:, 0:D_EXPERT]
        up = mid[:, D_EXPERT:]
        act = (gt * _sigmoid(gt)) * up
        y_ref[...] = _dot(act.astype(BF16), wo_bf[...])

    @pl.when(i >= nb)
    def _():
        y_ref[...] = jnp.zeros_like(y_ref)


def _experts(h2, buf_tok, blk_e, nblk, w_ei, w_eo):
    n, d = h2.shape
    p = buf_tok.shape[0]
    blk = EXP_BLK
    nb = p // blk
    grid_spec = pltpu.PrefetchScalarGridSpec(
        num_scalar_prefetch=3,
        grid=(nb,),
        in_specs=[
            pl.BlockSpec(memory_space=pl.ANY),
            pl.BlockSpec((1, d, 2 * D_EXPERT), lambda i, tok, be, nbk: (be[i], 0, 0)),
            pl.BlockSpec((1, D_EXPERT, d), lambda i, tok, be, nbk: (be[i], 0, 0)),
        ],
        out_specs=pl.BlockSpec((blk, d), lambda i, tok, be, nbk: (i, 0)),
        scratch_shapes=[
            pltpu.VMEM((2, blk, d), F32),
            pltpu.VMEM((d, 2 * D_EXPERT), BF16),
            pltpu.VMEM((D_EXPERT, d), BF16),
            pltpu.SemaphoreType.DMA((2,)),
        ],
    )
    return pl.pallas_call(
        _expert_kernel,
        grid_spec=grid_spec,
        out_shape=jax.ShapeDtypeStruct((p, d), F32),
        compiler_params=_cparams(("arbitrary",)),
        name="experts",
    )(buf_tok, blk_e, nblk, h2, w_ei, w_eo)


def _combine_kernel(pos_ref, x_ref, mod_ref, wt_ref, ln_g_ref, ln_b_ref, y_hbm, o_ref,
                    ybuf, sem, *, alpha):
    b = pl.program_id(0)
    i = pl.program_id(1)
    nt = pl.num_programs(1)
    tm = x_ref.shape[1]
    step = b * nt + i
    nsteps = pl.num_programs(0) * nt

    def start_gather(s, slot):
        def body(r, carry):
            base = (s * tm + r) * TOP_K
            for k in range(TOP_K):
                pltpu.make_async_copy(y_hbm.at[pl.ds(pos_ref[base + k], 1), :],
                                      ybuf.at[slot, k, pl.ds(r, 1), :], sem.at[slot]).start()
            return carry
        lax.fori_loop(0, tm, body, 0)

    @pl.when(step == 0)
    def _():
        start_gather(0, 0)

    @pl.when(step + 1 < nsteps)
    def _():
        start_gather(step + 1, (step + 1) % 2)

    slot = step % 2
    for k in range(TOP_K):
        pltpu.make_async_copy(y_hbm.at[pl.ds(0, tm), :], ybuf.at[slot, k], sem.at[slot]).wait()
    wt = wt_ref[0]
    f = wt[:, 0:1] * ybuf[slot, 0] + wt[:, 1:2] * ybuf[slot, 1]
    x = x_ref[0]
    o_ref[0] = _ln(alpha * x + mod_ref[0, 5:6, :] * f) * ln_g_ref[...] + ln_b_ref[...]


def _combine(x1, mod, wts, pos, y, ln_g, ln_b, n_ctx_tiles, alpha):
    B, tn, D = x1.shape
    tm = TOK_TILE
    nt = tn // tm
    grid_spec = pltpu.PrefetchScalarGridSpec(
        num_scalar_prefetch=1,
        grid=(B, nt),
        in_specs=[
            pl.BlockSpec((1, tm, D), lambda b, i, pos: (b, i, 0)),
            pl.BlockSpec((1, 6, D), lambda b, i, pos: (jnp.where(i < n_ctx_tiles, B, b), 0, 0)),
            pl.BlockSpec((1, tm, TOP_K), lambda b, i, pos: (b, i, 0)),
            pl.BlockSpec((1, D), lambda b, i, pos: (0, 0)),
            pl.BlockSpec((1, D), lambda b, i, pos: (0, 0)),
            pl.BlockSpec(memory_space=pl.ANY),
        ],
        out_specs=pl.BlockSpec((1, tm, D), lambda b, i, pos: (b, i, 0)),
        scratch_shapes=[
            pltpu.VMEM((2, TOP_K, tm, D), F32),
            pltpu.SemaphoreType.DMA((2,)),
        ],
    )
    return pl.pallas_call(
        functools.partial(_combine_kernel, alpha=alpha),
        grid_spec=grid_spec,
        out_shape=jax.ShapeDtypeStruct((B, tn, D), F32),
        compiler_params=_cparams(("arbitrary", "arbitrary")),
        name="combine",
    )(pos, x1, mod, wts, ln_g, ln_b, y)


def _route(logits):
    n = logits.shape[0]
    a = n * TOP_K
    g_logit = logits[:, :N_GROUPS]
    g_prob = jax.nn.softmax(g_logit, axis=-1)
    g_sel = jnp.argmax(g_logit, axis=-1).astype(jnp.int32)
    e_all = logits[:, N_GROUPS:N_ROUTE].reshape(n, N_GROUPS, EXPERTS_PER_GROUP)
    e_logit = jnp.take_along_axis(e_all, g_sel[:, None, None], axis=1)[:, 0]
    top_l, top_i = lax.top_k(e_logit, TOP_K)
    g_p = jnp.take_along_axis(g_prob, g_sel[:, None], axis=1)
    wts = jax.nn.softmax(top_l, axis=-1) * g_p
    eid = (g_sel[:, None] * EXPERTS_PER_GROUP + top_i).reshape(a)

    onehot = (eid[:, None] == jnp.arange(N_EXPERTS, dtype=jnp.int32)[None, :]).astype(jnp.int32)
    csum = jnp.cumsum(onehot, axis=0)
    counts = csum[-1]
    rank = jnp.sum(csum * onehot, axis=1) - 1
    padded = (counts + EXP_BLK - 1) // EXP_BLK * EXP_BLK
    pad_end = jnp.cumsum(padded)
    pad_start = pad_end - padded
    pos = (pad_start[eid] + rank).astype(jnp.int32)
    p = a + N_EXPERTS * EXP_BLK
    nb = p // EXP_BLK
    buf_tok = jnp.zeros((p,), jnp.int32).at[pos].set(jnp.arange(a, dtype=jnp.int32) // TOP_K)
    blk_e = jnp.minimum(
        jnp.searchsorted(pad_end, jnp.arange(nb, dtype=jnp.int32) * EXP_BLK, side='right'),
        N_EXPERTS - 1).astype(jnp.int32)
    nblk = (pad_end[-1] // EXP_BLK).astype(jnp.int32).reshape(1)
    return wts, pos, buf_tok, blk_e, nblk


def _rope_tables(n_ctx, n_lat):
    nf = ATT_HEAD_DIM // 4
    inv = ROPE_BASE ** (-jnp.arange(nf, dtype=F32) / nf)
    rows = n_lat // GRID_W
    pos_r = jnp.repeat(jnp.arange(rows, dtype=F32), GRID_W)
    pos_c = jnp.tile(jnp.arange(GRID_W, dtype=F32), rows)
    ang_r = pos_r[:, None] * inv
    ang_c = pos_c[:, None] * inv
    cos_h = jnp.concatenate([jnp.cos(ang_r)] * 2 + [jnp.cos(ang_c)] * 2, axis=-1)
    sin_h = jnp.concatenate([-jnp.sin(ang_r), jnp.sin(ang_r),
                             -jnp.sin(ang_c), jnp.sin(ang_c)], axis=-1)
    reps = LANES // ATT_HEAD_DIM
    cos_l = jnp.tile(cos_h, (1, reps))
    sin_l = jnp.tile(sin_h, (1, reps))
    cos_t = jnp.concatenate([jnp.ones((n_ctx, LANES), F32), cos_l], axis=0)
    sin_t = jnp.concatenate([jnp.zeros((n_ctx, LANES), F32), sin_l], axis=0)
    return cos_t, sin_t


def kernel(x, c, ctx, c_ctx, w_ada, b_ada, w_in, conv_w, attn_sink, mlstm_gate_b, mlstm_norm_w,
           w_proj_a, w_proj_b, w_proj_c, w_out, ln1_g, ln1_b, w_route_group, b_route_group,
           w_route_expert, b_route_expert, w_expert_in, w_expert_out, ln2_g, ln2_b):
    B, L, D = x.shape
    n_ctx = ctx.shape[1]
    depth = w_ada.shape[0]
    T = n_ctx + L
    alpha = (2 * depth) ** 0.25
    assert D == D_MODEL and n_ctx % TOK_TILE == 0 and L % TOK_TILE == 0 and L % GRID_W == 0
    nct = n_ctx // TOK_TILE

    nrows = -(-(B + 1) // 8) * 8
    cond = jnp.concatenate([c, c_ctx[None, :], jnp.zeros((nrows - B - 1, D), F32)], axis=0)
    mod_all = _ada(cond, w_ada, b_ada).reshape(depth, nrows, 6, D)

    cos_t, sin_t = _rope_tables(n_ctx, L)
    xa = jnp.concatenate([ctx, x], axis=1)

    for i in range(depth):
        need_ctx = i < depth - 1
        mod = mod_all[i]
        w_mix = jnp.pad(w_in[i][:, :MIX_COLS + N_GATE],
                        ((0, 0), (0, LANES - N_GATE))).astype(BF16)
        w_gate = w_in[i][:, MIX_COLS + N_GATE:].astype(BF16)

        a0u, q, kv, mqkv, mo, gpre = _inproj(xa, mod, w_mix, cos_t, sin_t, n_ctx)
        yb = _attention(q, kv, attn_sink[i], n_ctx)
        gcol = gpre[:, :, :N_GATE]
        grow = jnp.swapaxes(gcol, 1, 2)
        hf, hb = _mlstm(mqkv, gcol, grow, mlstm_gate_b[i], n_ctx)

        off = 0 if need_ctx else nct
        w_route = jnp.pad(jnp.concatenate([w_route_group[i], w_route_expert[i]], axis=1),
                          ((0, 0), (0, LANES - N_ROUTE)))
        b_route = jnp.pad(jnp.concatenate([b_route_group[i], b_route_expert[i]]),
                          (0, LANES - N_ROUTE)).reshape(1, LANES)
        x1, h2, logits = _merge(
            xa, mod, a0u, yb, hf, hb, mo, conv_w[i], mlstm_norm_w[i].reshape(1, MLSTM_DIM),
            w_gate, w_proj_a[i].astype(BF16), w_proj_b[i].astype(BF16), w_proj_c[i].astype(BF16),
            w_out[i].astype(BF16), ln1_g[i].reshape(1, D), ln1_b[i].reshape(1, D),
            w_route, b_route, n_ctx, off, alpha)

        tn = x1.shape[1]
        n_tok = B * tn
        wts, pos, buf_tok, blk_e, nblk = _route(logits.reshape(n_tok, LANES))
        y = _experts(h2.reshape(n_tok, D), buf_tok, blk_e, nblk, w_expert_in[i], w_expert_out[i])
        xa = _combine(x1, mod, wts.reshape(B, tn, TOP_K), pos, y,
                      ln2_g[i].reshape(1, D), ln2_b[i].reshape(1, D),
                      nct if need_ctx else 0, alpha)
    return xa
```

```python
import functools

import jax
import jax.numpy as jnp
import numpy as np
from jax import lax
from jax.experimental import pallas as pl
from jax.experimental.pallas import tpu as pltpu

D_MODEL = 1024
GRID_W = 64
CONV_DIM = 256
ATT_HEADS = 8
ATT_KV_HEADS = 2
ATT_HEAD_DIM = 64
ATT_WINDOW = 128
ROPE_BASE = 10000.0
MLSTM_HEADS = 4
MLSTM_HEAD_DIM = 64
MLSTM_DIM = MLSTM_HEADS * MLSTM_HEAD_DIM
N_GROUPS = 4
EXPERTS_PER_GROUP = 8
N_EXPERTS = N_GROUPS * EXPERTS_PER_GROUP
TOP_K = 2
D_EXPERT = D_MODEL // 2
LN_EPS = 1e-6
NEG_INF = -1e30

ATT_Q_DIM = ATT_HEADS * ATT_HEAD_DIM
ATT_KV_DIM = ATT_KV_HEADS * ATT_HEAD_DIM
N_GATE = 4 * MLSTM_HEADS
MIX_COLS = 3 * CONV_DIM + ATT_Q_DIM + 2 * ATT_KV_DIM + 4 * MLSTM_DIM
N_ROUTE = N_GROUPS + N_EXPERTS

LANES = 128
VMEM_LIMIT = 56 * 1024 * 1024
TOK_TILE = 256
ATT_QB = 128
MLSTM_CH = 128
EXP_BLK = 256
HALO = 16
GATHER_UNROLL = 8

F32 = jnp.float32
BF16 = jnp.bfloat16


def _cparams(sem):
    return pltpu.CompilerParams(dimension_semantics=sem, vmem_limit_bytes=VMEM_LIMIT)


def _ln(x, axis=-1):
    mu = jnp.mean(x, axis=axis, keepdims=True)
    xc = x - mu
    var = jnp.mean(xc * xc, axis=axis, keepdims=True)
    return xc * lax.rsqrt(var + LN_EPS)


def _sigmoid(x):
    return 1.0 / (1.0 + jnp.exp(-x))


def _split3(x):
    hi = x.astype(BF16)
    r1 = x - hi.astype(F32)
    mid = r1.astype(BF16)
    lo = (r1 - mid.astype(F32)).astype(BF16)
    return hi, mid, lo


def _dot(a, b):
    return jnp.dot(a, b, preferred_element_type=F32)


def _dot_nt(a, b):
    return lax.dot_general(a, b, (((1,), (1,)), ((), ())), preferred_element_type=F32)


def _dot3(x, rhs_b):
    return sum(_dot(part, rhs_b) for part in _split3(x))


def _dot3_l(lhs_b, x):
    return sum(_dot(lhs_b, part) for part in _split3(x))


def _ada_kernel(c_ref, w_ref, b_ref, o_ref):
    cv = c_ref[...]
    s = cv * _sigmoid(cv)
    w = w_ref[0]
    acc = jnp.zeros((s.shape[0], w.shape[1]), F32)
    w3 = _split3(w)
    for sp in _split3(s):
        for wp in w3:
            acc = acc + _dot(sp, wp)
    o_ref[0] = acc + b_ref[0]


def _ada(cond, w_ada, b_ada):
    depth, d, n6 = w_ada.shape
    rows = cond.shape[0]
    nt = n6 // d
    return pl.pallas_call(
        _ada_kernel,
        grid=(depth, nt),
        in_specs=[
            pl.BlockSpec((rows, d), lambda l, j: (0, 0)),
            pl.BlockSpec((1, d, d), lambda l, j: (l, 0, j)),
            pl.BlockSpec((1, 1, d), lambda l, j: (l, 0, j)),
        ],
        out_specs=pl.BlockSpec((1, rows, d), lambda l, j: (l, 0, j)),
        out_shape=jax.ShapeDtypeStruct((depth, rows, n6), F32),
        compiler_params=_cparams(("arbitrary", "arbitrary")),
        name="ada_mod",
    )(cond, w_ada, b_ada.reshape(depth, 1, n6))


def _rope(x, cos, sin_signed, lane):
    swapped = jnp.where(lane % 32 < 16,
                        pltpu.roll(x, LANES - 16, axis=1),
                        pltpu.roll(x, 16, axis=1))
    return x * cos + swapped * sin_signed


_OFF_A = 0
_OFF_Q = _OFF_A + 3 * CONV_DIM
_OFF_K = _OFF_Q + ATT_Q_DIM
_OFF_V = _OFF_K + 2 * ATT_KV_DIM
_OFF_MQK = _OFF_V + 2 * ATT_KV_DIM
_OFF_G = _OFF_MQK + 2 * MLSTM_DIM
_W_TOK_COLS = _OFF_G + LANES


def _inproj_kernel(x_ref, mod_ref, w_ref, wt_ref, cos_ref, sin_ref,
                   a0u_ref, q_ref, kv_ref, mqk_ref, mvo_t_ref, g_ref):
    x = x_ref[0]
    shift = mod_ref[0, 0:1, :]
    scale = mod_ref[0, 1:2, :]
    h = (_ln(x) * (1.0 + scale) + shift).astype(BF16)
    cos = cos_ref[...]
    sin = sin_ref[...]
    lane = lax.broadcasted_iota(jnp.int32, cos.shape, 1)

    def cols(lo, n):
        return _dot(h, w_ref[:, lo:lo + n])

    za = cols(_OFF_A, 3 * CONV_DIM)
    a0u_ref[0, :, 0:CONV_DIM] = za[:, 0:CONV_DIM].astype(BF16)
    a0u_ref[0, :, CONV_DIM:2 * CONV_DIM] = (
        za[:, CONV_DIM:2 * CONV_DIM] * za[:, 2 * CONV_DIM:3 * CONV_DIM]).astype(BF16)
    zq = cols(_OFF_Q, ATT_Q_DIM)
    qscale = ATT_HEAD_DIM ** -0.5
    for j in range(ATT_Q_DIM // LANES):
        piece = _rope(zq[:, j * LANES:(j + 1) * LANES], cos, sin, lane)
        q_ref[0, :, j * LANES:(j + 1) * LANES] = (piece * qscale).astype(BF16)
    zk = cols(_OFF_K, 2 * ATT_KV_DIM)
    for j in range(2):
        kv_ref[0, :, j * LANES:(j + 1) * LANES] = _rope(
            zk[:, j * LANES:(j + 1) * LANES], cos, sin, lane).astype(BF16)
    kv_ref[0, :, 2 * LANES:] = cols(_OFF_V, 2 * ATT_KV_DIM).astype(BF16)
    zm = cols(_OFF_MQK, 2 * MLSTM_DIM)
    mqk_ref[0, :, 0:MLSTM_DIM] = zm[:, 0:MLSTM_DIM].astype(BF16)
    mqk_ref[0, :, MLSTM_DIM:] = (zm[:, MLSTM_DIM:] * (MLSTM_HEAD_DIM ** -0.5)).astype(BF16)
    g_ref[0] = cols(_OFF_G, LANES)
    mvo_t_ref[0] = _dot_nt(wt_ref[...], h).astype(BF16)


def _inproj(xa, mod, w_tok, w_chan, cos_t, sin_t, n_ctx):
    B, T, D = xa.shape
    tm = TOK_TILE
    nct = n_ctx // tm
    nchan = w_chan.shape[0]

    def tok(width):
        return pl.BlockSpec((1, tm, width), lambda b, i: (b, i, 0))

    outs = [(2 * CONV_DIM, BF16), (ATT_Q_DIM, BF16), (4 * ATT_KV_DIM, BF16), (2 * MLSTM_DIM, BF16)]
    return pl.pallas_call(
        _inproj_kernel,
        grid=(B, T // tm),
        in_specs=[
            tok(D),
            pl.BlockSpec((1, 6, D), lambda b, i: (jnp.where(i < nct, B, b), 0, 0)),
            pl.BlockSpec(w_tok.shape, lambda b, i: (0, 0)),
            pl.BlockSpec(w_chan.shape, lambda b, i: (0, 0)),
            pl.BlockSpec((tm, LANES), lambda b, i: (i, 0)),
            pl.BlockSpec((tm, LANES), lambda b, i: (i, 0)),
        ],
        out_specs=[tok(w) for w, _ in outs] + [
            pl.BlockSpec((1, nchan, tm), lambda b, i: (b, 0, i)), tok(LANES)],
        out_shape=[jax.ShapeDtypeStruct((B, T, w), dt) for w, dt in outs] + [
            jax.ShapeDtypeStruct((B, nchan, T), BF16), jax.ShapeDtypeStruct((B, T, LANES), F32)],
        compiler_params=_cparams(("parallel", "arbitrary")),
        name="in_proj",
    )(xa, mod, w_tok, w_chan, cos_t, sin_t)


def _attn_kernel(sink_ref, q_ref, kvc_ref, kvp_ref, kvm_ref, kvn_ref, o_ref, *, n_ctx_blk):
    j = pl.program_id(1)
    nq = pl.num_programs(1)
    is_lat = j >= n_ctx_blk
    has_prev = j >= n_ctx_blk + 1
    has_next = j < nq - 1
    n_ctx = kvc_ref.shape[1]
    qb = q_ref.shape[1]
    nk = n_ctx + 3 * qb
    half = LANES // 2

    kv_all = jnp.concatenate([kvc_ref[0], kvp_ref[0], kvm_ref[0], kvn_ref[0]], axis=0)
    lane_row = lax.broadcasted_iota(jnp.int32, (1, LANES), 1)
    keep = [(lane_row < half).astype(F32).astype(BF16), (lane_row >= half).astype(F32).astype(BF16)]
    pad_v = jnp.zeros((qb, LANES), BF16)

    def k_ext(tile, par):
        return kv_all[:, tile * LANES:(tile + 1) * LANES] * keep[par]

    def v_ext(tile, par):
        v = jnp.concatenate([kv_all[:, (2 + tile) * LANES:(3 + tile) * LANES], pad_v], axis=0)
        return jnp.concatenate([v * keep[par], jnp.broadcast_to(keep[par], v.shape)], axis=1)

    qi = lax.broadcasted_iota(jnp.int32, (qb, qb), 0)
    ki = lax.broadcasted_iota(jnp.int32, (qb, qb), 1)
    ok_prev = jnp.logical_and(ki >= qi, has_prev)
    ok_next = jnp.logical_and(ki <= qi, jnp.logical_and(has_next, is_lat))
    lane_q = lax.broadcasted_iota(jnp.int32, (qb, LANES), 1)
    neg = jnp.full((qb, LANES), NEG_INF, F32)

    q = q_ref[0]
    group = ATT_HEADS // ATT_KV_HEADS
    outs = []
    for pair in range(ATT_HEADS // 2):
        kvh = (2 * pair) // group
        q_pair = q[:, pair * LANES:(pair + 1) * LANES]
        acc = None
        for par in range(2):
            h = 2 * pair + par
            tile = 0 if kvh == par else 1
            s = _dot_nt(q_pair, k_ext(tile, par))
            s_ext = jnp.concatenate([
                s[:, 0:n_ctx],
                jnp.where(ok_prev, s[:, n_ctx:n_ctx + qb], neg),
                jnp.where(is_lat, s[:, n_ctx + qb:n_ctx + 2 * qb], neg),
                jnp.where(ok_next, s[:, n_ctx + 2 * qb:nk], neg),
                jnp.where(lane_q == 0, sink_ref[h], neg)], axis=1)
            m = jnp.max(s_ext, axis=-1, keepdims=True)
            p = jnp.exp(s_ext - m).astype(BF16)
            part = _dot(p, v_ext(tile, par))
            acc = part if acc is None else acc + part
        outs.append(acc[:, 0:LANES] / acc[:, LANES:])
    o_ref[0] = jnp.concatenate(outs, axis=-1).astype(o_ref.dtype)


def _attention(q, kv, sink, n_ctx):
    B, T, _ = q.shape
    qb = ATT_QB
    nq = T // qb
    n_ctx_blk = n_ctx // qb
    kvw = kv.shape[2]
    grid_spec = pltpu.PrefetchScalarGridSpec(
        num_scalar_prefetch=1,
        grid=(B, nq),
        in_specs=[
            pl.BlockSpec((1, qb, ATT_Q_DIM), lambda b, j, sk: (b, j, 0)),
            pl.BlockSpec((1, n_ctx, kvw), lambda b, j, sk: (b, 0, 0)),
            pl.BlockSpec((1, qb, kvw), lambda b, j, sk: (b, jnp.maximum(j - 1, 0), 0)),
            pl.BlockSpec((1, qb, kvw), lambda b, j, sk: (b, j, 0)),
            pl.BlockSpec((1, qb, kvw), lambda b, j, sk: (b, jnp.minimum(j + 1, nq - 1), 0)),
        ],
        out_specs=pl.BlockSpec((1, qb, ATT_Q_DIM), lambda b, j, sk: (b, j, 0)),
    )
    return pl.pallas_call(
        functools.partial(_attn_kernel, n_ctx_blk=n_ctx_blk),
        grid_spec=grid_spec,
        out_shape=jax.ShapeDtypeStruct((B, T, ATT_Q_DIM), BF16),
        compiler_params=_cparams(("parallel", "arbitrary")),
        name="window_attn",
    )(sink, q, kv, kv, kv, kv)


def _log_sigmoid(x):
    return jnp.minimum(x, 0.0) - jnp.log1p(jnp.exp(-jnp.abs(x)))


def _rows_to_lanes(a, base):
    return jnp.concatenate([a[base + h:base + h + 1, :] for h in range(MLSTM_HEADS)], axis=1)


def _mlstm_dir(d, qk, v_t, gcol, grow, ct_bd, n_bd, m_prev):
    ch = qk.shape[0]
    nh = MLSTM_HEADS
    dh = MLSTM_HEAD_DIM
    wide = nh * ch
    fwd = d == 0

    r_i = lax.broadcasted_iota(jnp.int32, (ch, ch), 0)
    c_i = lax.broadcasted_iota(jnp.int32, (ch, ch), 1)
    seen_t = (r_i <= c_i) if fwd else (r_i >= c_i)
    seen_tt = (c_i <= r_i) if fwd else (c_i >= r_i)
    r_w = lax.broadcasted_iota(jnp.int32, (ch, wide), 0)
    s_w = lax.broadcasted_iota(jnp.int32, (ch, wide), 1) % ch
    seen_w = (r_w <= s_w) if fwd else (r_w >= s_w)

    base_i = 2 * d * nh
    base_f = base_i + nh

    lf_col = _log_sigmoid(gcol)
    cum_col = _dot3_l(seen_tt.astype(BF16), lf_col)
    lane16 = lax.broadcasted_iota(jnp.int32, gcol.shape, 1)
    z = jnp.where(jnp.logical_and(lane16 >= base_i, lane16 < base_f), gcol, -cum_col)
    ch16 = lax.broadcasted_iota(jnp.int32, (N_GATE, wide), 0)
    hd16 = lax.broadcasted_iota(jnp.int32, (N_GATE, wide), 1) // ch
    sel = jnp.logical_or(ch16 == base_i + hd16, ch16 == base_f + hd16).astype(BF16)
    x_t = _dot3(z, sel)

    lf_row = _log_sigmoid(grow)
    rhs2 = jnp.concatenate([seen_t.astype(BF16), jnp.ones((ch, ch), BF16)], axis=1)
    rows = _dot3(lf_row, rhs2)
    b_all = _rows_to_lanes(rows[:, 0:ch], base_f)
    g_all = _rows_to_lanes(rows[:, ch:], base_f)
    li_all = _rows_to_lanes(grow, base_i)

    dmat = jnp.where(seen_w, x_t + b_all, -jnp.inf)
    a = b_all + m_prev
    m = jnp.maximum(a, jnp.max(dmat, axis=0, keepdims=True))
    w_intra = jnp.exp(dmat - m)
    e_inter = jnp.exp(a - m)

    q = qk[:, 0:MLSTM_DIM]
    k = qk[:, MLSTM_DIM:]
    lb = lax.broadcasted_iota(jnp.int32, (1, MLSTM_DIM), 1) // dh
    q_bd = jnp.concatenate([q * (lb == h).astype(F32).astype(BF16) for h in range(nh)], axis=0)
    s_t = _dot_nt(k, q_bd) * w_intra
    nq = _rows_to_lanes(_dot_nt(n_bd.astype(BF16), q), 0)
    den = jnp.sum(s_t, axis=0, keepdims=True) + e_inter * nq
    inv = 1.0 / jnp.maximum(jnp.abs(den), jnp.exp(-m))
    inter_t = _dot_nt(ct_bd.astype(BF16), q)
    s_b = s_t.astype(BF16)
    outs = []
    for h in range(nh):
        seg = slice(h * ch, (h + 1) * ch)
        blk = slice(h * dh, (h + 1) * dh)
        num = _dot(v_t[blk, :], s_b[:, seg]) + e_inter[:, seg] * inter_t[blk, :]
        outs.append(num * inv[:, seg])
    h_t = jnp.concatenate(outs, axis=0)

    m_loc = g_all + jnp.max(x_t, axis=0, keepdims=True)
    m_new = jnp.maximum(g_all + m_prev, m_loc)
    sa = jnp.exp(g_all + m_prev - m_new)
    sb = jnp.exp(m_loc - m_new)
    e_loc = jnp.exp(g_all - b_all + li_all - m_loc)
    v_e = jnp.concatenate(
        [v_t[h * dh:(h + 1) * dh, :].astype(F32) * e_loc[:, h * ch:(h + 1) * ch] for h in range(nh)],
        axis=0).astype(BF16)
    ct_loc = _dot(v_e, k)
    reps = MLSTM_DIM // ch

    def per_head_rows(row, nrows):
        return jnp.concatenate(
            [jnp.broadcast_to(jnp.concatenate([row[:, h * ch:(h + 1) * ch]] * reps, axis=1),
                              (nrows, MLSTM_DIM)) for h in range(nh)], axis=0)

    eb = lax.broadcasted_iota(jnp.int32, (MLSTM_DIM, MLSTM_DIM), 0) // dh
    db = lax.broadcasted_iota(jnp.int32, (MLSTM_DIM, MLSTM_DIM), 1) // dh
    ct_new = jnp.where(eb == db, per_head_rows(sa, dh) * ct_bd + per_head_rows(sb, dh) * ct_loc, 0.0)
    nrow = n_bd.shape[0]
    e_rows = jnp.concatenate([e_loc[:, h * ch:(h + 1) * ch] for h in range(nh)]
                             + [jnp.zeros((nrow - nh, ch), F32)], axis=0).astype(BF16)
    n_loc = _dot(e_rows, k)
    pad = jnp.zeros((nrow - nh, MLSTM_DIM), F32)
    sa8 = jnp.concatenate([per_head_rows(sa, 1), pad], axis=0)
    sb8 = jnp.concatenate([per_head_rows(sb, 1), pad], axis=0)
    hb8 = lax.broadcasted_iota(jnp.int32, (nrow, MLSTM_DIM), 0)
    db8 = lax.broadcasted_iota(jnp.int32, (nrow, MLSTM_DIM), 1) // dh
    n_new = jnp.where(hb8 == db8, sa8 * n_bd + sb8 * n_loc, 0.0)
    return h_t, ct_new, n_new, m_new


def _mlstm_kernel(qk_f_ref, qk_b_ref, vt_f_ref, vt_b_ref, gc_f_ref, gc_b_ref, gr_f_ref, gr_b_ref,
                  bcol_ref, brow_ref, hf_ref, hb_ref, c_scr, n_scr, m_scr):
    j = pl.program_id(1)

    @pl.when(j == 0)
    def _():
        c_scr[...] = jnp.zeros_like(c_scr)
        n_scr[...] = jnp.zeros_like(n_scr)
        m_scr[...] = jnp.zeros_like(m_scr)

    states = [(c_scr[d], n_scr[d], m_scr[d]) for d in range(2)]
    dirs = ((qk_f_ref, vt_f_ref, gc_f_ref, gr_f_ref), (qk_b_ref, vt_b_ref, gc_b_ref, gr_b_ref))
    results = []
    for d, (qk_ref, vt_ref, gc_ref, gr_ref) in enumerate(dirs):
        results.append(_mlstm_dir(
            d, qk_ref[0], vt_ref[0],
            gc_ref[0] + bcol_ref[...], gr_ref[0] + brow_ref[...], *states[d]))
    hf_ref[0] = results[0][0]
    hb_ref[0] = results[1][0]
    for d in range(2):
        c_scr[d] = results[d][1]
        n_scr[d] = results[d][2]
        m_scr[d] = results[d][3]


def _mlstm(mqk, mvo_t, gcol, grow, gate_b, n_ctx):
    B, T, _ = mqk.shape
    ch = MLSTM_CH
    nc = T // ch
    ncc = n_ctx // ch

    def rev(j):
        return jnp.where(j < ncc, ncc - 1 - j, nc - 1 - (j - ncc))

    def tok(width, order):
        return pl.BlockSpec((1, ch, width), lambda b, j: (b, order(j), 0))

    def chan(rows, order):
        return pl.BlockSpec((1, rows, ch), lambda b, j: (b, 0, order(j)))

    ident = lambda j: j
    return pl.pallas_call(
        _mlstm_kernel,
        grid=(B, nc),
        in_specs=[
            tok(2 * MLSTM_DIM, ident), tok(2 * MLSTM_DIM, rev),
            chan(MLSTM_DIM, ident), chan(MLSTM_DIM, rev),
            tok(N_GATE, ident), tok(N_GATE, rev),
            chan(N_GATE, ident), chan(N_GATE, rev),
            pl.BlockSpec((1, N_GATE), lambda b, j: (0, 0)),
            pl.BlockSpec((N_GATE, 1), lambda b, j: (0, 0)),
        ],
        out_specs=[chan(MLSTM_DIM, ident), chan(MLSTM_DIM, rev)],
        out_shape=[jax.ShapeDtypeStruct((B, MLSTM_DIM, T), F32)] * 2,
        scratch_shapes=[
            pltpu.VMEM((2, MLSTM_DIM, MLSTM_DIM), F32),
            pltpu.VMEM((2, 8, MLSTM_DIM), F32),
            pltpu.VMEM((2, 1, MLSTM_HEADS * ch), F32),
        ],
        compiler_params=_cparams(("parallel", "arbitrary")),
        name="mlstm",
    )(mqk, mqk, mvo_t, mvo_t, gcol, gcol, grow, grow,
      gate_b.reshape(1, N_GATE), gate_b.reshape(N_GATE, 1))


def _merge_kernel(x_ref, mod_ref, a0u_ref, up_ref, un_ref, yb_ref, hf_ref, hb_ref, mvo_t_ref,
                  convw_ref, normw_ref, wg_ref, wpa_ref, wpb_ref, wpc_ref, wo_ref,
                  ln_g_ref, ln_b_ref, wr_ref, br_ref,
                  x1_ref, h2_ref, lg_ref, *, off, n_ctx_tiles, alpha):
    i = pl.program_id(1) + off
    nt = pl.num_programs(1) + off
    tm = x_ref.shape[1]
    x = x_ref[0]
    h = (_ln(x) * (1.0 + mod_ref[0, 1:2, :]) + mod_ref[0, 0:1, :]).astype(BF16)

    a0 = a0u_ref[0, :, 0:CONV_DIM].astype(F32)
    u = a0u_ref[0, :, CONV_DIM:].astype(F32)
    prev_ok = jnp.logical_and(i != 0, i != n_ctx_tiles)
    next_ok = jnp.logical_and(i != n_ctx_tiles - 1, i != nt - 1)
    u_prev = jnp.where(prev_ok, up_ref[0, HALO - 1:HALO, CONV_DIM:].astype(F32), 0.0)
    u_next = jnp.where(next_ok, un_ref[0, 0:1, CONV_DIM:].astype(F32), 0.0)
    row = lax.broadcasted_iota(jnp.int32, u.shape, 0)
    u_dn = jnp.where(row == 0, u_prev, pltpu.roll(u, 1, axis=0))
    u_up = jnp.where(row == tm - 1, u_next, pltpu.roll(u, tm - 1, axis=0))
    cw = convw_ref[...]
    ya = a0 * (u_dn * cw[0:1, :] + u * cw[1:2, :] + u_up * cw[2:3, :])

    hm = hf_ref[0] + hb_ref[0]
    normed = jnp.concatenate(
        [_ln(hm[hd * MLSTM_HEAD_DIM:(hd + 1) * MLSTM_HEAD_DIM, :], axis=0)
         for hd in range(MLSTM_HEADS)], axis=0)
    yc_t = _sigmoid(mvo_t_ref[0].astype(F32)) * (normed * normw_ref[...])
    yc = yc_t.T

    pa = _dot(ya.astype(BF16), wpa_ref[...])
    pb = _dot(yb_ref[0], wpb_ref[...])
    pc = _dot(yc.astype(BF16), wpc_ref[...])
    d = x.shape[1]
    merged = (_sigmoid(_dot(h, wg_ref[:, 0:d])) * pa
              + _sigmoid(_dot(h, wg_ref[:, d:2 * d])) * pb
              + _sigmoid(_dot(h, wg_ref[:, 2 * d:3 * d])) * pc)
    yl = _dot(merged.astype(BF16), wo_ref[...])

    x1 = _ln(alpha * x + mod_ref[0, 2:3, :] * yl) * ln_g_ref[...] + ln_b_ref[...]
    x1_ref[0] = x1
    h2 = _ln(x1) * (1.0 + mod_ref[0, 4:5, :]) + mod_ref[0, 3:4, :]
    h2_ref[0] = h2
    h_hi = h2.astype(BF16)
    h_lo = (h2 - h_hi.astype(F32)).astype(BF16)
    wr = wr_ref[...]
    w_hi = wr.astype(BF16)
    w_lo = (wr - w_hi.astype(F32)).astype(BF16)
    lg_ref[0] = _dot(h_hi, w_hi) + _dot(h_lo, w_hi) + _dot(h_hi, w_lo) + br_ref[...]


def _merge(xa, mod, a0u, yb, hf_t, hb_t, mvo_t, conv_w, norm_w, w_gate, w_pa, w_pb, w_pc, w_o,
           ln_g, ln_b, w_route, b_route, n_ctx, off, alpha):
    B, T, D = xa.shape
    tm = TOK_TILE
    nct = n_ctx // tm
    nt = T // tm - off
    tn = nt * tm
    hpt = tm // HALO
    nhalo = T // HALO

    def tok(width):
        return pl.BlockSpec((1, tm, width), lambda b, i: (b, i + off, 0))

    def chan(rows):
        return pl.BlockSpec((1, rows, tm), lambda b, i: (b, 0, i + off))

    def full(a):
        return pl.BlockSpec(a.shape, lambda b, i: (0,) * a.ndim)

    def otok(width):
        return pl.BlockSpec((1, tm, width), lambda b, i: (b, i, 0))

    consts = [conv_w, norm_w, w_gate, w_pa, w_pb, w_pc, w_o, ln_g, ln_b, w_route, b_route]
    return pl.pallas_call(
        functools.partial(_merge_kernel, off=off, n_ctx_tiles=nct, alpha=alpha),
        grid=(B, nt),
        in_specs=[
            tok(D),
            pl.BlockSpec((1, 6, D), lambda b, i: (jnp.where(i + off < nct, B, b), 0, 0)),
            tok(2 * CONV_DIM),
            pl.BlockSpec((1, HALO, 2 * CONV_DIM),
                         lambda b, i: (b, jnp.maximum((i + off) * hpt - 1, 0), 0)),
            pl.BlockSpec((1, HALO, 2 * CONV_DIM),
                         lambda b, i: (b, jnp.minimum((i + off + 1) * hpt, nhalo - 1), 0)),
            tok(ATT_Q_DIM), chan(MLSTM_DIM), chan(MLSTM_DIM),
            pl.BlockSpec((1, MLSTM_DIM, tm), lambda b, i: (b, 1, i + off)),

        ] + [full(a) for a in consts],
        out_specs=[otok(D), otok(D), otok(LANES)],
        out_shape=[jax.ShapeDtypeStruct((B, tn, D), F32),
                   jax.ShapeDtypeStruct((B, tn, D), F32),
                   jax.ShapeDtypeStruct((B, tn, LANES), F32)],
        compiler_params=_cparams(("parallel", "arbitrary")),
        name="merge",
    )(xa, mod, a0u, a0u, a0u, yb, hf_t, hb_t, mvo_t, *consts)


def _expert_kernel(tok_ref, blke_ref, nblk_ref, h_hbm, wi_ref, wo_ref, y_ref,
                   xbuf, wi_bf, wo_bf, sem):
    i = pl.program_id(0)
    blk = xbuf.shape[1]
    nb = nblk_ref[0]

    def start_gather(b, slot):
        def body(r, carry):
            t = tok_ref[b * blk + r]
            pltpu.make_async_copy(h_hbm.at[pl.ds(t, 1), :],
                                  xbuf.at[slot, pl.ds(r, 1), :], sem.at[slot]).start()
            return carry
        lax.fori_loop(0, blk, body, 0, unroll=GATHER_UNROLL)

    @pl.when(jnp.logical_and(i == 0, nb > 0))
    def _():
        start_gather(0, 0)

    @pl.when(i + 1 < nb)
    def _():
        start_gather(i + 1, (i + 1) % 2)

    @pl.when(i < nb)
    def _():
        slot = i % 2
        pltpu.make_async_copy(h_hbm.at[pl.ds(0, blk), :], xbuf.at[slot], sem.at[slot]).wait()
        e_now = blke_ref[i]
        e_before = blke_ref[jnp.maximum(i - 1, 0)]

        @pl.when(jnp.logical_or(i == 0, e_now != e_before))
        def _():
            wi_bf[...] = wi_ref[0].astype(BF16)
            wo_bf[...] = wo_ref[0].astype(BF16)

        xin = xbuf[slot].astype(BF16)
        mid = _dot(xin, wi_bf[...])
        gt = mid[:, 0:D_EXPERT]
        up = mid[:, D_EXPERT:]
        act = (gt * _sigmoid(gt)) * up
        y_ref[...] = _dot(act.astype(BF16), wo_bf[...])

    @pl.when(i >= nb)
    def _():
        y_ref[...] = jnp.zeros_like(y_ref)


def _experts(h2, buf_tok, blk_e, nblk, w_ei, w_eo):
    n, d = h2.shape
    p = buf_tok.shape[0]
    blk = EXP_BLK
    nb = p // blk
    grid_spec = pltpu.PrefetchScalarGridSpec(
        num_scalar_prefetch=3,
        grid=(nb,),
        in_specs=[
            pl.BlockSpec(memory_space=pl.ANY),
            pl.BlockSpec((1, d, 2 * D_EXPERT), lambda i, tok, be, nbk: (be[i], 0, 0)),
            pl.BlockSpec((1, D_EXPERT, d), lambda i, tok, be, nbk: (be[i], 0, 0)),
        ],
        out_specs=pl.BlockSpec((blk, d), lambda i, tok, be, nbk: (i, 0)),
        scratch_shapes=[
            pltpu.VMEM((2, blk, d), F32),
            pltpu.VMEM((d, 2 * D_EXPERT), BF16),
            pltpu.VMEM((D_EXPERT, d), BF16),
            pltpu.SemaphoreType.DMA((2,)),
        ],
    )
    return pl.pallas_call(
        _expert_kernel,
        grid_spec=grid_spec,
        out_shape=jax.ShapeDtypeStruct((p, d), F32),
        compiler_params=_cparams(("arbitrary",)),
        name="experts",
    )(buf_tok, blk_e, nblk, h2, w_ei, w_eo)


def _combine_kernel(pos_ref, x_ref, mod_ref, wt_ref, ln_g_ref, ln_b_ref, y_hbm, o_ref,
                    ybuf, sem, *, alpha):
    b = pl.program_id(0)
    i = pl.program_id(1)
    nt = pl.num_programs(1)
    tm = x_ref.shape[1]
    step = b * nt + i
    nsteps = pl.num_programs(0) * nt

    def start_gather(s, slot):
        def body(r, carry):
            base = (s * tm + r) * TOP_K
            for k in range(TOP_K):
                pltpu.make_async_copy(y_hbm.at[pl.ds(pos_ref[base + k], 1), :],
                                      ybuf.at[slot, k, pl.ds(r, 1), :], sem.at[slot]).start()
            return carry
        lax.fori_loop(0, tm, body, 0, unroll=GATHER_UNROLL)

    @pl.when(step == 0)
    def _():
        start_gather(0, 0)

    @pl.when(step + 1 < nsteps)
    def _():
        start_gather(step + 1, (step + 1) % 2)

    slot = step % 2
    for k in range(TOP_K):
        pltpu.make_async_copy(y_hbm.at[pl.ds(0, tm), :], ybuf.at[slot, k], sem.at[slot]).wait()
    wt = wt_ref[0]
    f = wt[:, 0:1] * ybuf[slot, 0] + wt[:, 1:2] * ybuf[slot, 1]
    x = x_ref[0]
    o_ref[0] = _ln(alpha * x + mod_ref[0, 5:6, :] * f) * ln_g_ref[...] + ln_b_ref[...]


def _combine(x1, mod, wts, pos, y, ln_g, ln_b, n_ctx_tiles, alpha):
    B, tn, D = x1.shape
    tm = TOK_TILE
    nt = tn // tm
    grid_spec = pltpu.PrefetchScalarGridSpec(
        num_scalar_prefetch=1,
        grid=(B, nt),
        in_specs=[
            pl.BlockSpec((1, tm, D), lambda b, i, pos: (b, i, 0)),
            pl.BlockSpec((1, 6, D), lambda b, i, pos: (jnp.where(i < n_ctx_tiles, B, b), 0, 0)),
            pl.BlockSpec((1, tm, TOP_K), lambda b, i, pos: (b, i, 0)),
            pl.BlockSpec((1, D), lambda b, i, pos: (0, 0)),
            pl.BlockSpec((1, D), lambda b, i, pos: (0, 0)),
            pl.BlockSpec(memory_space=pl.ANY),
        ],
        out_specs=pl.BlockSpec((1, tm, D), lambda b, i, pos: (b, i, 0)),
        scratch_shapes=[
            pltpu.VMEM((2, TOP_K, tm, D), F32),
            pltpu.SemaphoreType.DMA((2,)),
        ],
    )
    return pl.pallas_call(
        functools.partial(_combine_kernel, alpha=alpha),
        grid_spec=grid_spec,
        out_shape=jax.ShapeDtypeStruct((B, tn, D), F32),
        compiler_params=_cparams(("arbitrary", "arbitrary")),
        name="combine",
    )(pos, x1, mod, wts, ln_g, ln_b, y)


def _route(logits):
    n = logits.shape[0]
    a = n * TOP_K
    g_logit = logits[:, :N_GROUPS]
    g_prob = jax.nn.softmax(g_logit, axis=-1)
    g_sel = jnp.argmax(g_logit, axis=-1).astype(jnp.int32)
    e_all = logits[:, N_GROUPS:N_ROUTE].reshape(n, N_GROUPS, EXPERTS_PER_GROUP)
    e_logit = jnp.take_along_axis(e_all, g_sel[:, None, None], axis=1)[:, 0]
    top_l, top_i = lax.top_k(e_logit, TOP_K)
    g_p = jnp.take_along_axis(g_prob, g_sel[:, None], axis=1)
    wts = jax.nn.softmax(top_l, axis=-1) * g_p
    eid = (g_sel[:, None] * EXPERTS_PER_GROUP + top_i).reshape(a)

    onehot = (eid[:, None] == jnp.arange(N_EXPERTS, dtype=jnp.int32)[None, :]).astype(jnp.int32)
    csum = jnp.cumsum(onehot, axis=0)
    counts = csum[-1]
    rank = jnp.sum(csum * onehot, axis=1) - 1
    padded = (counts + EXP_BLK - 1) // EXP_BLK * EXP_BLK
    pad_end = jnp.cumsum(padded)
    pad_start = pad_end - padded
    pos = (pad_start[eid] + rank).astype(jnp.int32)
    p = a + N_EXPERTS * EXP_BLK
    nb = p // EXP_BLK
    buf_tok = jnp.zeros((p,), jnp.int32).at[pos].set(jnp.arange(a, dtype=jnp.int32) // TOP_K)
    blk_e = jnp.minimum(
        jnp.searchsorted(pad_end, jnp.arange(nb, dtype=jnp.int32) * EXP_BLK, side='right'),
        N_EXPERTS - 1).astype(jnp.int32)
    nblk = (pad_end[-1] // EXP_BLK).astype(jnp.int32).reshape(1)
    return wts, pos, buf_tok, blk_e, nblk


def _rope_tables(n_ctx, n_lat):
    nf = ATT_HEAD_DIM // 4
    inv = ROPE_BASE ** (-jnp.arange(nf, dtype=F32) / nf)
    rows = n_lat // GRID_W
    pos_r = jnp.repeat(jnp.arange(rows, dtype=F32), GRID_W)
    pos_c = jnp.tile(jnp.arange(GRID_W, dtype=F32), rows)
    ang_r = pos_r[:, None] * inv
    ang_c = pos_c[:, None] * inv
    cos_h = jnp.concatenate([jnp.cos(ang_r)] * 2 + [jnp.cos(ang_c)] * 2, axis=-1)
    sin_h = jnp.concatenate([-jnp.sin(ang_r), jnp.sin(ang_r),
                             -jnp.sin(ang_c), jnp.sin(ang_c)], axis=-1)
    reps = LANES // ATT_HEAD_DIM
    cos_l = jnp.tile(cos_h, (1, reps))
    sin_l = jnp.tile(sin_h, (1, reps))
    cos_t = jnp.concatenate([jnp.ones((n_ctx, LANES), F32), cos_l], axis=0)
    sin_t = jnp.concatenate([jnp.zeros((n_ctx, LANES), F32), sin_l], axis=0)
    return cos_t, sin_t


def _projection_weights(w_in_l):
    offs = np.cumsum((0, 3 * CONV_DIM, ATT_Q_DIM, ATT_KV_DIM, ATT_KV_DIM,
                      MLSTM_DIM, MLSTM_DIM, MLSTM_DIM, MLSTM_DIM, N_GATE)).tolist()
    a, q, k, v, mq, mk, mv, mo, g = [w_in_l[:, offs[n]:offs[n + 1]] for n in range(9)]
    hd = ATT_HEAD_DIM

    def swap(w):
        return jnp.concatenate([w[:, hd:], w[:, :hd]], axis=1)

    g_pad = jnp.pad(g, ((0, 0), (0, LANES - N_GATE)))
    w_tok = jnp.concatenate([a, q, k, swap(k), v, swap(v), mq, mk, g_pad], axis=1).astype(BF16)
    w_chan = jnp.concatenate([mv, mo], axis=1).T.astype(BF16)
    w_gate = w_in_l[:, offs[9]:].astype(BF16)
    return w_tok, w_chan, w_gate


def kernel(x, c, ctx, c_ctx, w_ada, b_ada, w_in, conv_w, attn_sink, mlstm_gate_b, mlstm_norm_w,
           w_proj_a, w_proj_b, w_proj_c, w_out, ln1_g, ln1_b, w_route_group, b_route_group,
           w_route_expert, b_route_expert, w_expert_in, w_expert_out, ln2_g, ln2_b):
    B, L, D = x.shape
    n_ctx = ctx.shape[1]
    depth = w_ada.shape[0]
    T = n_ctx + L
    alpha = (2 * depth) ** 0.25
    assert D == D_MODEL and n_ctx % TOK_TILE == 0 and L % TOK_TILE == 0 and L % GRID_W == 0
    assert MLSTM_CH == LANES and ATT_QB == LANES
    nct = n_ctx // TOK_TILE

    nrows = -(-(B + 1) // 8) * 8
    cond = jnp.concatenate([c, c_ctx[None, :], jnp.zeros((nrows - B - 1, D), F32)], axis=0)
    mod_all = _ada(cond, w_ada, b_ada).reshape(depth, nrows, 6, D)

    cos_t, sin_t = _rope_tables(n_ctx, L)
    xa = jnp.concatenate([ctx, x], axis=1)

    for i in range(depth):
        need_ctx = i < depth - 1
        mod = mod_all[i]
        w_tok, w_chan, w_gate = _projection_weights(w_in[i])

        a0u, q, kv, mqk, mvo_t, gpre = _inproj(xa, mod, w_tok, w_chan, cos_t, sin_t, n_ctx)
        yb = _attention(q, kv, attn_sink[i], n_ctx)
        gcol = gpre[:, :, :N_GATE]
        grow = jnp.swapaxes(gcol, 1, 2)
        hf_t, hb_t = _mlstm(mqk, mvo_t, gcol, grow, mlstm_gate_b[i], n_ctx)

        off = 0 if need_ctx else nct
        w_route = jnp.pad(jnp.concatenate([w_route_group[i], w_route_expert[i]], axis=1),
                          ((0, 0), (0, LANES - N_ROUTE)))
        b_route = jnp.pad(jnp.concatenate([b_route_group[i], b_route_expert[i]]),
                          (0, LANES - N_ROUTE)).reshape(1, LANES)
        x1, h2, logits = _merge(
            xa, mod, a0u, yb, hf_t, hb_t, mvo_t, conv_w[i], mlstm_norm_w[i].reshape(MLSTM_DIM, 1),
            w_gate, w_proj_a[i].astype(BF16), w_proj_b[i].astype(BF16), w_proj_c[i].astype(BF16),
            w_out[i].astype(BF16), ln1_g[i].reshape(1, D), ln1_b[i].reshape(1, D),
            w_route, b_route, n_ctx, off, alpha)

        tn = x1.shape[1]
        n_tok = B * tn
        wts, pos, buf_tok, blk_e, nblk = _route(logits.reshape(n_tok, LANES))
        y = _experts(h2.reshape(n_tok, D), buf_tok, blk_e, nblk, w_expert_in[i], w_expert_out[i])
        xa = _combine(x1, mod, wts.reshape(B, tn, TOP_K), pos, y,
                      ln2_g[i].reshape(1, D), ln2_b[i].reshape(1, D),
                      nct if need_ctx else 0, alpha)
    return xa
```

```python
import functools

import jax
import jax.numpy as jnp
import numpy as np
from jax import lax
from jax.experimental import pallas as pl
from jax.experimental.pallas import tpu as pltpu

D_MODEL = 1024
GRID_W = 64
CONV_DIM = 256
ATT_HEADS = 8
ATT_KV_HEADS = 2
ATT_HEAD_DIM = 64
ATT_WINDOW = 128
ROPE_BASE = 10000.0
MLSTM_HEADS = 4
MLSTM_HEAD_DIM = 64
MLSTM_DIM = MLSTM_HEADS * MLSTM_HEAD_DIM
N_GROUPS = 4
EXPERTS_PER_GROUP = 8
N_EXPERTS = N_GROUPS * EXPERTS_PER_GROUP
TOP_K = 2
D_EXPERT = D_MODEL // 2
LN_EPS = 1e-6
NEG_INF = -1e30

ATT_Q_DIM = ATT_HEADS * ATT_HEAD_DIM
ATT_KV_DIM = ATT_KV_HEADS * ATT_HEAD_DIM
N_GATE = 4 * MLSTM_HEADS
MIX_COLS = 3 * CONV_DIM + ATT_Q_DIM + 2 * ATT_KV_DIM + 4 * MLSTM_DIM
N_ROUTE = N_GROUPS + N_EXPERTS

LANES = 128
VMEM_LIMIT = 56 * 1024 * 1024
TOK_TILE = 256
ATT_QB = 128
MLSTM_CH = 128
EXP_BLK = 256
HALO = 16
ROW_TILE = 8
GATHER_UNROLL = 8

F32 = jnp.float32
BF16 = jnp.bfloat16


def _cparams(sem):
    return pltpu.CompilerParams(dimension_semantics=sem, vmem_limit_bytes=VMEM_LIMIT)


def _ln(x, axis=-1):
    mu = jnp.mean(x, axis=axis, keepdims=True)
    xc = x - mu
    var = jnp.mean(xc * xc, axis=axis, keepdims=True)
    return xc * lax.rsqrt(var + LN_EPS)


def _sigmoid(x):
    return 1.0 / (1.0 + jnp.exp(-x))


def _split3(x):
    hi = x.astype(BF16)
    r1 = x - hi.astype(F32)
    mid = r1.astype(BF16)
    lo = (r1 - mid.astype(F32)).astype(BF16)
    return hi, mid, lo


def _dot(a, b):
    return jnp.dot(a, b, preferred_element_type=F32)


def _dot_nt(a, b):
    return lax.dot_general(a, b, (((1,), (1,)), ((), ())), preferred_element_type=F32)


def _dot3(x, rhs_b):
    return sum(_dot(part, rhs_b) for part in _split3(x))


def _dot3_l(lhs_b, x):
    return sum(_dot(lhs_b, part) for part in _split3(x))


def _ada_kernel(c_ref, w_ref, b_ref, o_ref):
    cv = c_ref[...]
    s = cv * _sigmoid(cv)
    w = w_ref[0]
    acc = jnp.zeros((s.shape[0], w.shape[1]), F32)
    w3 = _split3(w)
    for sp in _split3(s):
        for wp in w3:
            acc = acc + _dot(sp, wp)
    o_ref[0] = acc + b_ref[0]


def _ada(cond, w_ada, b_ada):
    depth, d, n6 = w_ada.shape
    rows = cond.shape[0]
    nt = n6 // d
    return pl.pallas_call(
        _ada_kernel,
        grid=(depth, nt),
        in_specs=[
            pl.BlockSpec((rows, d), lambda l, j: (0, 0)),
            pl.BlockSpec((1, d, d), lambda l, j: (l, 0, j)),
            pl.BlockSpec((1, 1, d), lambda l, j: (l, 0, j)),
        ],
        out_specs=pl.BlockSpec((1, rows, d), lambda l, j: (l, 0, j)),
        out_shape=jax.ShapeDtypeStruct((depth, rows, n6), F32),
        compiler_params=_cparams(("arbitrary", "arbitrary")),
        name="ada_mod",
    )(cond, w_ada, b_ada.reshape(depth, 1, n6))


def _rope(x, cos, sin_signed, lane):
    swapped = jnp.where(lane % 32 < 16,
                        pltpu.roll(x, LANES - 16, axis=1),
                        pltpu.roll(x, 16, axis=1))
    return x * cos + swapped * sin_signed


_OFF_A = 0
_OFF_Q = _OFF_A + 3 * CONV_DIM
_OFF_K = _OFF_Q + ATT_Q_DIM
_OFF_V = _OFF_K + 2 * ATT_KV_DIM
_OFF_MQK = _OFF_V + 2 * ATT_KV_DIM
_OFF_G = _OFF_MQK + 2 * MLSTM_DIM
_W_TOK_COLS = _OFF_G + LANES


def _inproj_kernel(x_ref, mod_ref, w_ref, wt_ref, cos_ref, sin_ref,
                   a0u_ref, q_ref, kv_ref, mqk_ref, mvo_t_ref, gcol_ref, grow_ref):
    x = x_ref[0]
    shift = mod_ref[0, 0:1, :]
    scale = mod_ref[0, 1:2, :]
    h = (_ln(x) * (1.0 + scale) + shift).astype(BF16)
    cos = cos_ref[...]
    sin = sin_ref[...]
    lane = lax.broadcasted_iota(jnp.int32, cos.shape, 1)

    def cols(lo, n):
        return _dot(h, w_ref[:, lo:lo + n])

    za = cols(_OFF_A, 3 * CONV_DIM)
    a0u_ref[0, :, 0:CONV_DIM] = za[:, 0:CONV_DIM].astype(BF16)
    a0u_ref[0, :, CONV_DIM:2 * CONV_DIM] = (
        za[:, CONV_DIM:2 * CONV_DIM] * za[:, 2 * CONV_DIM:3 * CONV_DIM]).astype(BF16)
    zq = cols(_OFF_Q, ATT_Q_DIM)
    qscale = ATT_HEAD_DIM ** -0.5
    for j in range(ATT_Q_DIM // LANES):
        piece = _rope(zq[:, j * LANES:(j + 1) * LANES], cos, sin, lane)
        q_ref[0, :, j * LANES:(j + 1) * LANES] = (piece * qscale).astype(BF16)
    zk = cols(_OFF_K, 2 * ATT_KV_DIM)
    for j in range(2):
        kv_ref[0, :, j * LANES:(j + 1) * LANES] = _rope(
            zk[:, j * LANES:(j + 1) * LANES], cos, sin, lane).astype(BF16)
    kv_ref[0, :, 2 * LANES:] = cols(_OFF_V, 2 * ATT_KV_DIM).astype(BF16)
    zm = cols(_OFF_MQK, 2 * MLSTM_DIM)
    mqk_ref[0, :, 0:MLSTM_DIM] = zm[:, 0:MLSTM_DIM].astype(BF16)
    mqk_ref[0, :, MLSTM_DIM:] = (zm[:, MLSTM_DIM:] * (MLSTM_HEAD_DIM ** -0.5)).astype(BF16)
    gcol_ref[0] = cols(_OFF_G, LANES)[:, 0:N_GATE]
    nchan = mvo_t_ref.shape[1]
    zt = _dot_nt(wt_ref[...], h)
    mvo_t_ref[0] = zt[0:nchan, :].astype(BF16)
    grow_ref[0] = zt[nchan:, :]


def _inproj(xa, mod, w_tok, w_chan, cos_t, sin_t, n_ctx):
    B, T, D = xa.shape
    tm = TOK_TILE
    nct = n_ctx // tm
    nchan = w_chan.shape[0] - N_GATE

    def tok(width):
        return pl.BlockSpec((1, tm, width), lambda b, i: (b, i, 0))

    def chan(rows):
        return pl.BlockSpec((1, rows, tm), lambda b, i: (b, 0, i))

    outs = [(2 * CONV_DIM, BF16), (ATT_Q_DIM, BF16), (4 * ATT_KV_DIM, BF16), (2 * MLSTM_DIM, BF16)]
    return pl.pallas_call(
        _inproj_kernel,
        grid=(B, T // tm),
        in_specs=[
            tok(D),
            pl.BlockSpec((1, 6, D), lambda b, i: (jnp.where(i < nct, B, b), 0, 0)),
            pl.BlockSpec(w_tok.shape, lambda b, i: (0, 0)),
            pl.BlockSpec(w_chan.shape, lambda b, i: (0, 0)),
            pl.BlockSpec((tm, LANES), lambda b, i: (i, 0)),
            pl.BlockSpec((tm, LANES), lambda b, i: (i, 0)),
        ],
        out_specs=[tok(w) for w, _ in outs] + [
            chan(nchan), tok(N_GATE), chan(N_GATE)],
        out_shape=[jax.ShapeDtypeStruct((B, T, w), dt) for w, dt in outs] + [
            jax.ShapeDtypeStruct((B, nchan, T), BF16),
            jax.ShapeDtypeStruct((B, T, N_GATE), F32), jax.ShapeDtypeStruct((B, N_GATE, T), F32)],
        compiler_params=_cparams(("parallel", "arbitrary")),
        name="in_proj",
    )(xa, mod, w_tok, w_chan, cos_t, sin_t)


def _attn_kernel(sink_ref, q_ref, kvc_ref, kvp_ref, kvm_ref, kvn_ref, o_ref, *, n_ctx_blk):
    j = pl.program_id(1)
    nq = pl.num_programs(1)
    is_lat = j >= n_ctx_blk
    has_prev = j >= n_ctx_blk + 1
    has_next = j < nq - 1
    n_ctx = kvc_ref.shape[1]
    qb = q_ref.shape[1]
    nk = n_ctx + 3 * qb
    half = LANES // 2

    kv_all = jnp.concatenate([kvc_ref[0], kvp_ref[0], kvm_ref[0], kvn_ref[0]], axis=0)
    lane_row = lax.broadcasted_iota(jnp.int32, (1, LANES), 1)
    keep = [(lane_row < half).astype(F32).astype(BF16), (lane_row >= half).astype(F32).astype(BF16)]
    pad_v = jnp.zeros((qb, LANES), BF16)

    def k_ext(tile, par):
        return kv_all[:, tile * LANES:(tile + 1) * LANES] * keep[par]

    def v_ext(tile, par):
        v = jnp.concatenate([kv_all[:, (2 + tile) * LANES:(3 + tile) * LANES], pad_v], axis=0)
        return jnp.concatenate([v * keep[par], jnp.broadcast_to(keep[par], v.shape)], axis=1)

    qi = lax.broadcasted_iota(jnp.int32, (qb, qb), 0)
    ki = lax.broadcasted_iota(jnp.int32, (qb, qb), 1)
    ok_prev = jnp.logical_and(ki >= qi, has_prev)
    ok_next = jnp.logical_and(ki <= qi, jnp.logical_and(has_next, is_lat))
    lane_q = lax.broadcasted_iota(jnp.int32, (qb, LANES), 1)
    neg = jnp.full((qb, LANES), NEG_INF, F32)

    q = q_ref[0]
    group = ATT_HEADS // ATT_KV_HEADS
    outs = []
    for pair in range(ATT_HEADS // 2):
        kvh = (2 * pair) // group
        q_pair = q[:, pair * LANES:(pair + 1) * LANES]
        acc = None
        for par in range(2):
            h = 2 * pair + par
            tile = 0 if kvh == par else 1
            s = _dot_nt(q_pair, k_ext(tile, par))
            s_ext = jnp.concatenate([
                s[:, 0:n_ctx],
                jnp.where(ok_prev, s[:, n_ctx:n_ctx + qb], neg),
                jnp.where(is_lat, s[:, n_ctx + qb:n_ctx + 2 * qb], neg),
                jnp.where(ok_next, s[:, n_ctx + 2 * qb:nk], neg),
                jnp.where(lane_q == 0, sink_ref[h], neg)], axis=1)
            m = jnp.max(s_ext, axis=-1, keepdims=True)
            p = jnp.exp(s_ext - m).astype(BF16)
            part = _dot(p, v_ext(tile, par))
            acc = part if acc is None else acc + part
        outs.append(acc[:, 0:LANES] / acc[:, LANES:])
    o_ref[0] = jnp.concatenate(outs, axis=-1).astype(o_ref.dtype)


def _attention(q, kv, sink, n_ctx):
    B, T, _ = q.shape
    qb = ATT_QB
    nq = T // qb
    n_ctx_blk = n_ctx // qb
    kvw = kv.shape[2]
    grid_spec = pltpu.PrefetchScalarGridSpec(
        num_scalar_prefetch=1,
        grid=(B, nq),
        in_specs=[
            pl.BlockSpec((1, qb, ATT_Q_DIM), lambda b, j, sk: (b, j, 0)),
            pl.BlockSpec((1, n_ctx, kvw), lambda b, j, sk: (b, 0, 0)),
            pl.BlockSpec((1, qb, kvw), lambda b, j, sk: (b, jnp.maximum(j - 1, 0), 0)),
            pl.BlockSpec((1, qb, kvw), lambda b, j, sk: (b, j, 0)),
            pl.BlockSpec((1, qb, kvw), lambda b, j, sk: (b, jnp.minimum(j + 1, nq - 1), 0)),
        ],
        out_specs=pl.BlockSpec((1, qb, ATT_Q_DIM), lambda b, j, sk: (b, j, 0)),
    )
    return pl.pallas_call(
        functools.partial(_attn_kernel, n_ctx_blk=n_ctx_blk),
        grid_spec=grid_spec,
        out_shape=jax.ShapeDtypeStruct((B, T, ATT_Q_DIM), BF16),
        compiler_params=_cparams(("parallel", "arbitrary")),
        name="window_attn",
    )(sink, q, kv, kv, kv, kv)


def _log_sigmoid(x):
    return jnp.minimum(x, 0.0) - jnp.log1p(jnp.exp(-jnp.abs(x)))


def _rows_to_lanes(a, base):
    return jnp.concatenate([a[base + h:base + h + 1, :] for h in range(MLSTM_HEADS)], axis=1)


def _mlstm_dir(d, qk, v_t, gcol, grow, ct_bd, n_bd, m_prev):
    ch = qk.shape[0]
    nh = MLSTM_HEADS
    dh = MLSTM_HEAD_DIM
    wide = nh * ch
    fwd = d == 0

    r_i = lax.broadcasted_iota(jnp.int32, (ch, ch), 0)
    c_i = lax.broadcasted_iota(jnp.int32, (ch, ch), 1)
    seen_t = (r_i <= c_i) if fwd else (r_i >= c_i)
    seen_tt = (c_i <= r_i) if fwd else (c_i >= r_i)
    r_w = lax.broadcasted_iota(jnp.int32, (ch, wide), 0)
    s_w = lax.broadcasted_iota(jnp.int32, (ch, wide), 1) % ch
    seen_w = (r_w <= s_w) if fwd else (r_w >= s_w)

    base_i = 2 * d * nh
    base_f = base_i + nh

    lf_col = _log_sigmoid(gcol)
    cum_col = _dot3_l(seen_tt.astype(BF16), lf_col)
    lane16 = lax.broadcasted_iota(jnp.int32, gcol.shape, 1)
    z = jnp.where(jnp.logical_and(lane16 >= base_i, lane16 < base_f), gcol, -cum_col)
    ch16 = lax.broadcasted_iota(jnp.int32, (N_GATE, wide), 0)
    hd16 = lax.broadcasted_iota(jnp.int32, (N_GATE, wide), 1) // ch
    sel = jnp.logical_or(ch16 == base_i + hd16, ch16 == base_f + hd16).astype(BF16)
    x_t = _dot3(z, sel)

    lf_row = _log_sigmoid(grow)
    rhs2 = jnp.concatenate([seen_t.astype(BF16), jnp.ones((ch, ch), BF16)], axis=1)
    rows = _dot3(lf_row, rhs2)
    b_all = _rows_to_lanes(rows[:, 0:ch], base_f)
    g_all = _rows_to_lanes(rows[:, ch:], base_f)
    li_all = _rows_to_lanes(grow, base_i)

    dmat = jnp.where(seen_w, x_t + b_all, -jnp.inf)
    a = b_all + m_prev
    m = jnp.maximum(a, jnp.max(dmat, axis=0, keepdims=True))
    w_intra = jnp.exp(dmat - m)
    e_inter = jnp.exp(a - m)

    q = qk[:, 0:MLSTM_DIM]
    k = qk[:, MLSTM_DIM:]
    lb = lax.broadcasted_iota(jnp.int32, (1, MLSTM_DIM), 1) // dh
    q_bd = jnp.concatenate([q * (lb == h).astype(F32).astype(BF16) for h in range(nh)], axis=0)
    s_t = _dot_nt(k, q_bd) * w_intra
    nq = _rows_to_lanes(_dot_nt(n_bd.astype(BF16), q), 0)
    den = jnp.sum(s_t, axis=0, keepdims=True) + e_inter * nq
    inv = 1.0 / jnp.maximum(jnp.abs(den), jnp.exp(-m))
    inter_t = _dot_nt(ct_bd.astype(BF16), q)
    s_b = s_t.astype(BF16)
    outs = []
    for h in range(nh):
        seg = slice(h * ch, (h + 1) * ch)
        blk = slice(h * dh, (h + 1) * dh)
        num = _dot(v_t[blk, :], s_b[:, seg]) + e_inter[:, seg] * inter_t[blk, :]
        outs.append(num * inv[:, seg])
    h_t = jnp.concatenate(outs, axis=0)

    m_loc = g_all + jnp.max(x_t, axis=0, keepdims=True)
    m_new = jnp.maximum(g_all + m_prev, m_loc)
    sa = jnp.exp(g_all + m_prev - m_new)
    sb = jnp.exp(m_loc - m_new)
    e_loc = jnp.exp(g_all - b_all + li_all - m_loc)
    v_e = jnp.concatenate(
        [v_t[h * dh:(h + 1) * dh, :].astype(F32) * e_loc[:, h * ch:(h + 1) * ch] for h in range(nh)],
        axis=0).astype(BF16)
    ct_loc = _dot(v_e, k)
    reps = MLSTM_DIM // ch

    def per_head_rows(row, nrows):
        return jnp.concatenate(
            [jnp.broadcast_to(jnp.concatenate([row[:, h * ch:(h + 1) * ch]] * reps, axis=1),
                              (nrows, MLSTM_DIM)) for h in range(nh)], axis=0)

    eb = lax.broadcasted_iota(jnp.int32, (MLSTM_DIM, MLSTM_DIM), 0) // dh
    db = lax.broadcasted_iota(jnp.int32, (MLSTM_DIM, MLSTM_DIM), 1) // dh
    ct_new = jnp.where(eb == db, per_head_rows(sa, dh) * ct_bd + per_head_rows(sb, dh) * ct_loc, 0.0)
    nrow = n_bd.shape[0]
    e_rows = jnp.concatenate([e_loc[:, h * ch:(h + 1) * ch] for h in range(nh)]
                             + [jnp.zeros((nrow - nh, ch), F32)], axis=0).astype(BF16)
    n_loc = _dot(e_rows, k)
    pad = jnp.zeros((nrow - nh, MLSTM_DIM), F32)
    sa8 = jnp.concatenate([per_head_rows(sa, 1), pad], axis=0)
    sb8 = jnp.concatenate([per_head_rows(sb, 1), pad], axis=0)
    hb8 = lax.broadcasted_iota(jnp.int32, (nrow, MLSTM_DIM), 0)
    db8 = lax.broadcasted_iota(jnp.int32, (nrow, MLSTM_DIM), 1) // dh
    n_new = jnp.where(hb8 == db8, sa8 * n_bd + sb8 * n_loc, 0.0)
    return h_t, ct_new, n_new, m_new


def _mlstm_kernel(qk_f_ref, qk_b_ref, vt_f_ref, vt_b_ref, gc_f_ref, gc_b_ref, gr_f_ref, gr_b_ref,
                  bcol_ref, brow_ref, hf_ref, hb_ref, c_scr, n_scr, m_scr):
    j = pl.program_id(1)

    @pl.when(j == 0)
    def _():
        c_scr[...] = jnp.zeros_like(c_scr)
        n_scr[...] = jnp.zeros_like(n_scr)
        m_scr[...] = jnp.zeros_like(m_scr)

    states = [(c_scr[d], n_scr[d], m_scr[d]) for d in range(2)]
    dirs = ((qk_f_ref, vt_f_ref, gc_f_ref, gr_f_ref), (qk_b_ref, vt_b_ref, gc_b_ref, gr_b_ref))
    results = []
    for d, (qk_ref, vt_ref, gc_ref, gr_ref) in enumerate(dirs):
        results.append(_mlstm_dir(
            d, qk_ref[0], vt_ref[0],
            gc_ref[0] + bcol_ref[...], gr_ref[0] + brow_ref[...], *states[d]))
    hf_ref[0] = results[0][0]
    hb_ref[0] = results[1][0]
    for d in range(2):
        c_scr[d] = results[d][1]
        n_scr[d] = results[d][2]
        m_scr[d] = results[d][3]


def _mlstm(mqk, mvo_t, gcol, grow, gate_b, n_ctx):
    B, T, _ = mqk.shape
    ch = MLSTM_CH
    nc = T // ch
    ncc = n_ctx // ch

    def rev(j):
        return jnp.where(j < ncc, ncc - 1 - j, nc - 1 - (j - ncc))

    def tok(width, order):
        return pl.BlockSpec((1, ch, width), lambda b, j: (b, order(j), 0))

    def chan(rows, order):
        return pl.BlockSpec((1, rows, ch), lambda b, j: (b, 0, order(j)))

    ident = lambda j: j
    return pl.pallas_call(
        _mlstm_kernel,
        grid=(B, nc),
        in_specs=[
            tok(2 * MLSTM_DIM, ident), tok(2 * MLSTM_DIM, rev),
            chan(MLSTM_DIM, ident), chan(MLSTM_DIM, rev),
            tok(N_GATE, ident), tok(N_GATE, rev),
            chan(N_GATE, ident), chan(N_GATE, rev),
            pl.BlockSpec((1, N_GATE), lambda b, j: (0, 0)),
            pl.BlockSpec((N_GATE, 1), lambda b, j: (0, 0)),
        ],
        out_specs=[chan(MLSTM_DIM, ident), chan(MLSTM_DIM, rev)],
        out_shape=[jax.ShapeDtypeStruct((B, MLSTM_DIM, T), F32)] * 2,
        scratch_shapes=[
            pltpu.VMEM((2, MLSTM_DIM, MLSTM_DIM), F32),
            pltpu.VMEM((2, 8, MLSTM_DIM), F32),
            pltpu.VMEM((2, 1, MLSTM_HEADS * ch), F32),
        ],
        compiler_params=_cparams(("parallel", "arbitrary")),
        name="mlstm",
    )(mqk, mqk, mvo_t, mvo_t, gcol, gcol, grow, grow,
      gate_b.reshape(1, N_GATE), gate_b.reshape(N_GATE, 1))


def _merge_kernel(x_ref, mod_ref, a0u_ref, up_ref, un_ref, yb_ref, hf_ref, hb_ref, mvo_t_ref,
                  convw_ref, normw_ref, wg_ref, wpa_ref, wpb_ref, wpc_ref, wo_ref,
                  ln_g_ref, ln_b_ref, wr_ref, br_ref,
                  x1_ref, h2_ref, wt_ref, route_ref, hist_ref, *, off, n_ctx_tiles, alpha):
    i = pl.program_id(1) + off
    nt = pl.num_programs(1) + off
    tm = x_ref.shape[1]
    x = x_ref[0]
    h = (_ln(x) * (1.0 + mod_ref[0, 1:2, :]) + mod_ref[0, 0:1, :]).astype(BF16)

    a0 = a0u_ref[0, :, 0:CONV_DIM].astype(F32)
    u = a0u_ref[0, :, CONV_DIM:].astype(F32)
    prev_ok = jnp.logical_and(i != 0, i != n_ctx_tiles)
    next_ok = jnp.logical_and(i != n_ctx_tiles - 1, i != nt - 1)
    u_prev = jnp.where(prev_ok, up_ref[0, HALO - 1:HALO, CONV_DIM:].astype(F32), 0.0)
    u_next = jnp.where(next_ok, un_ref[0, 0:1, CONV_DIM:].astype(F32), 0.0)
    row = lax.broadcasted_iota(jnp.int32, u.shape, 0)
    u_dn = jnp.where(row == 0, u_prev, pltpu.roll(u, 1, axis=0))
    u_up = jnp.where(row == tm - 1, u_next, pltpu.roll(u, tm - 1, axis=0))
    cw = convw_ref[...]
    ya = a0 * (u_dn * cw[0:1, :] + u * cw[1:2, :] + u_up * cw[2:3, :])

    hm = hf_ref[0] + hb_ref[0]
    normed = jnp.concatenate(
        [_ln(hm[hd * MLSTM_HEAD_DIM:(hd + 1) * MLSTM_HEAD_DIM, :], axis=0)
         for hd in range(MLSTM_HEADS)], axis=0)
    yc_t = _sigmoid(mvo_t_ref[0].astype(F32)) * (normed * normw_ref[...])
    yc = yc_t.T

    pa = _dot(ya.astype(BF16), wpa_ref[...])
    pb = _dot(yb_ref[0], wpb_ref[...])
    pc = _dot(yc.astype(BF16), wpc_ref[...])
    d = x.shape[1]
    merged = (_sigmoid(_dot(h, wg_ref[:, 0:d])) * pa
              + _sigmoid(_dot(h, wg_ref[:, d:2 * d])) * pb
              + _sigmoid(_dot(h, wg_ref[:, 2 * d:3 * d])) * pc)
    yl = _dot(merged.astype(BF16), wo_ref[...])

    x1 = _ln(alpha * x + mod_ref[0, 2:3, :] * yl) * ln_g_ref[...] + ln_b_ref[...]
    x1_ref[0] = x1
    h2 = _ln(x1) * (1.0 + mod_ref[0, 4:5, :]) + mod_ref[0, 3:4, :]
    for s in range(d // LANES):
        h2_ref[pl.ds(s, tm, stride=ROW_TILE), :] = h2[:, s * LANES:(s + 1) * LANES]

    h_hi = h2.astype(BF16)
    h_lo = (h2 - h_hi.astype(F32)).astype(BF16)
    wr = wr_ref[...]
    w_hi = wr.astype(BF16)
    w_lo = (wr - w_hi.astype(F32)).astype(BF16)
    lg = _dot_nt(w_hi, h_hi) + _dot_nt(w_hi, h_lo) + _dot_nt(w_lo, h_hi) + br_ref[...]
    sub = lax.broadcasted_iota(jnp.int32, lg.shape, 0)
    big = jnp.int32(2 * LANES)
    is_g = sub < N_GROUPS
    gl = jnp.where(is_g, lg, -jnp.inf)
    g_max = jnp.max(gl, axis=0, keepdims=True)
    g_sel = jnp.min(jnp.where(gl == g_max, sub, big), axis=0, keepdims=True)
    g_p = 1.0 / jnp.sum(jnp.where(is_g, jnp.exp(gl - g_max), 0.0), axis=0, keepdims=True)
    lo = N_GROUPS + EXPERTS_PER_GROUP * g_sel
    el = jnp.where(jnp.logical_and(sub >= lo, sub < lo + EXPERTS_PER_GROUP), lg, -jnp.inf)
    e1 = jnp.max(el, axis=0, keepdims=True)
    i1 = jnp.min(jnp.where(el == e1, sub, big), axis=0, keepdims=True)
    el2 = jnp.where(sub == i1, -jnp.inf, el)
    e2 = jnp.max(el2, axis=0, keepdims=True)
    i2 = jnp.min(jnp.where(el2 == e2, sub, big), axis=0, keepdims=True)
    t = jnp.exp(e2 - e1)
    w1 = g_p / (1.0 + t)
    w2 = w1 * t
    rows = jnp.concatenate([(i1 - N_GROUPS).astype(F32), (i2 - N_GROUPS).astype(F32), w1, w2,
                            jnp.zeros((4, tm), F32)], axis=0)
    route_ref[0] = rows
    wt_ref[0] = rows.T
    chosen = jnp.where(jnp.logical_or(sub == i1, sub == i2), 1.0, 0.0).astype(BF16)
    hist_ref[0] = _dot_nt(jnp.ones((ROW_TILE, tm), BF16), chosen)


def _merge(xa, mod, a0u, yb, hf_t, hb_t, mvo_t, conv_w, norm_w, w_gate, w_pa, w_pb, w_pc, w_o,
           ln_g, ln_b, w_route, b_route, n_ctx, off, alpha):
    B, T, D = xa.shape
    tm = TOK_TILE
    nct = n_ctx // tm
    nt = T // tm - off
    tn = nt * tm
    hpt = tm // HALO
    nhalo = T // HALO

    def tok(width):
        return pl.BlockSpec((1, tm, width), lambda b, i: (b, i + off, 0))

    def chan(rows):
        return pl.BlockSpec((1, rows, tm), lambda b, i: (b, 0, i + off))

    def full(a):
        return pl.BlockSpec(a.shape, lambda b, i: (0,) * a.ndim)

    def otok(width):
        return pl.BlockSpec((1, tm, width), lambda b, i: (b, i, 0))

    consts = [conv_w, norm_w, w_gate, w_pa, w_pb, w_pc, w_o, ln_g, ln_b, w_route, b_route]
    return pl.pallas_call(
        functools.partial(_merge_kernel, off=off, n_ctx_tiles=nct, alpha=alpha),
        grid=(B, nt),
        in_specs=[
            tok(D),
            pl.BlockSpec((1, 6, D), lambda b, i: (jnp.where(i + off < nct, B, b), 0, 0)),
            tok(2 * CONV_DIM),
            pl.BlockSpec((1, HALO, 2 * CONV_DIM),
                         lambda b, i: (b, jnp.maximum((i + off) * hpt - 1, 0), 0)),
            pl.BlockSpec((1, HALO, 2 * CONV_DIM),
                         lambda b, i: (b, jnp.minimum((i + off + 1) * hpt, nhalo - 1), 0)),
            tok(ATT_Q_DIM), chan(MLSTM_DIM), chan(MLSTM_DIM),
            pl.BlockSpec((1, MLSTM_DIM, tm), lambda b, i: (b, 1, i + off)),

        ] + [full(a) for a in consts],
        out_specs=[otok(D),
                   pl.BlockSpec((tm * ROW_TILE, LANES), lambda b, i: (b * nt + i, 0)),
                   otok(ROW_TILE),
                   pl.BlockSpec((1, ROW_TILE, tm), lambda b, i: (b, 0, i)),
                   pl.BlockSpec((1, ROW_TILE, LANES), lambda b, i: (b * nt + i, 0, 0))],
        out_shape=[jax.ShapeDtypeStruct((B, tn, D), F32),
                   jax.ShapeDtypeStruct((B * tn * ROW_TILE, LANES), F32),
                   jax.ShapeDtypeStruct((B, tn, ROW_TILE), F32),
                   jax.ShapeDtypeStruct((B, ROW_TILE, tn), F32),
                   jax.ShapeDtypeStruct((B * nt, ROW_TILE, LANES), F32)],
        compiler_params=_cparams(("parallel", "arbitrary")),
        name="merge",
    )(xa, mod, a0u, a0u, a0u, yb, hf_t, hb_t, mvo_t, *consts)


def _positions_kernel(route_ref, pstart_ref, upper_ref, pos_ref, carry):
    step = pl.program_id(0) * pl.num_programs(1) + pl.program_id(1)

    @pl.when(step == 0)
    def _():
        carry[...] = jnp.zeros_like(carry)

    r = route_ref[0]
    eid = jnp.concatenate([r[0:1, :], r[1:2, :]], axis=1)
    sub = lax.broadcasted_iota(jnp.int32, (N_EXPERTS, eid.shape[1]), 0).astype(F32)
    onehot = jnp.where(sub == eid, 1.0, 0.0)
    earlier = _dot(onehot.astype(BF16), upper_ref[...])
    seen = carry[...]
    pos = jnp.sum(onehot * (earlier + (seen + pstart_ref[...])), axis=0, keepdims=True)
    pos_ref[0] = pos.astype(jnp.int32)
    carry[...] = seen + jnp.sum(onehot, axis=1, keepdims=True)


def _positions(route, pstart, tm):
    B, _, tn = route.shape
    nt = tn // tm
    a_i = lax.broadcasted_iota(jnp.int32, (TOP_K * tm, TOP_K * tm), 0)
    b_i = lax.broadcasted_iota(jnp.int32, (TOP_K * tm, TOP_K * tm), 1)
    upper = (a_i < b_i).astype(BF16)
    return pl.pallas_call(
        _positions_kernel,
        grid=(B, nt),
        in_specs=[
            pl.BlockSpec((1, ROW_TILE, tm), lambda b, i: (b, 0, i)),
            pl.BlockSpec(pstart.shape, lambda b, i: (0, 0)),
            pl.BlockSpec(upper.shape, lambda b, i: (0, 0)),
        ],
        out_specs=pl.BlockSpec((1, 1, TOP_K * tm), lambda b, i: (b * nt + i, 0, 0)),
        out_shape=jax.ShapeDtypeStruct((B * nt, 1, TOP_K * tm), jnp.int32),
        scratch_shapes=[pltpu.VMEM((N_EXPERTS, 1), F32)],
        compiler_params=_cparams(("arbitrary", "arbitrary")),
        name="positions",
    )(route, pstart, upper)


def _dispatch_kernel(pos_ref, pend_ref, nblk_ref, h_ref, xs_hbm, zbuf, sem, zsem):
    s = pl.program_id(0)
    tm = h_ref.shape[0] // ROW_TILE
    zrows = zbuf.shape[0]
    last_blk = xs_hbm.shape[0] // zrows - 1

    @pl.when(s == 0)
    def _():
        zbuf[...] = jnp.zeros_like(zbuf)

        def zero_copy(start):
            return pltpu.make_async_copy(
                zbuf, xs_hbm.at[pl.ds(pl.multiple_of(start, ROW_TILE), zrows), :], zsem)

        jobs = []
        for e in range(N_EXPERTS):
            before = pend_ref[e - 1] if e > 0 else 0
            jobs.append((pend_ref[e] * ROW_TILE - zrows, pend_ref[e] > before))
        for e in range(N_EXPERTS):
            jobs.append(((nblk_ref[0] + e) * zrows, nblk_ref[0] + e <= last_blk))
        for st, needed in jobs:
            @pl.when(needed)
            def _(st=st):
                zero_copy(st).start()
        for st, needed in jobs:
            @pl.when(needed)
            def _(st=st):
                zero_copy(st).wait()

    def row_copy(r, k):
        src = pl.multiple_of(r * ROW_TILE, ROW_TILE)
        dst = pl.multiple_of(pos_ref[(s * TOP_K + k) * tm + r] * ROW_TILE, ROW_TILE)
        return pltpu.make_async_copy(h_ref.at[pl.ds(src, ROW_TILE), :],
                                     xs_hbm.at[pl.ds(dst, ROW_TILE), :], sem)

    def body(r, carry):
        for k in range(TOP_K):
            row_copy(r, k).start()
        return carry
    lax.fori_loop(0, tm, body, 0, unroll=GATHER_UNROLL)
    for k in range(TOP_K):
        pltpu.make_async_copy(h_ref, xs_hbm.at[pl.ds(0, tm * ROW_TILE), :], sem).wait()


def _dispatch(h2v, pos, pend, nblk, p_rows, tm):
    rows = h2v.shape[0]
    nsteps = rows // (tm * ROW_TILE)
    grid_spec = pltpu.PrefetchScalarGridSpec(
        num_scalar_prefetch=3,
        grid=(nsteps,),
        in_specs=[pl.BlockSpec((tm * ROW_TILE, LANES), lambda s, pos, pend, nbk: (s, 0))],
        out_specs=pl.BlockSpec(memory_space=pl.ANY),
        scratch_shapes=[
            pltpu.VMEM((EXP_BLK * ROW_TILE, LANES), F32),
            pltpu.SemaphoreType.DMA,
            pltpu.SemaphoreType.DMA,
        ],
    )
    return pl.pallas_call(
        _dispatch_kernel,
        grid_spec=grid_spec,
        out_shape=jax.ShapeDtypeStruct((p_rows * ROW_TILE, LANES), F32),
        compiler_params=_cparams(("arbitrary",)),
        name="dispatch",
    )(pos, pend, nblk, h2v)


def _expert_kernel(blke_ref, nblk_ref, xs_ref, wi_ref, wo_ref, y_ref, wi_bf, wo_bf):
    i = pl.program_id(0)
    nb = nblk_ref[0]
    blk = xs_ref.shape[0] // ROW_TILE
    nsl = wi_ref.shape[1] // LANES

    @pl.when(i < nb)
    def _():
        e_now = blke_ref[i]
        e_before = blke_ref[jnp.maximum(i - 1, 0)]

        @pl.when(jnp.logical_or(i == 0, e_now != e_before))
        def _():
            wi_bf[...] = wi_ref[0].astype(BF16)
            wo_bf[...] = wo_ref[0].astype(BF16)

        xin = jnp.concatenate(
            [xs_ref[pl.ds(s, blk, stride=ROW_TILE), :] for s in range(nsl)], axis=1).astype(BF16)
        mid = _dot(xin, wi_bf[...])
        gt = mid[:, 0:D_EXPERT]
        up = mid[:, D_EXPERT:]
        act = (gt * _sigmoid(gt)) * up
        y = _dot(act.astype(BF16), wo_bf[...])
        for s in range(nsl):
            y_ref[pl.ds(s, blk, stride=ROW_TILE), :] = y[:, s * LANES:(s + 1) * LANES]

    @pl.when(i >= nb)
    def _():
        y_ref[...] = jnp.zeros_like(y_ref)


def _experts(xs, blk_e, nblk, w_ei, w_eo):
    d = w_ei.shape[1]
    blk = EXP_BLK
    nb = xs.shape[0] // (blk * ROW_TILE)
    grid_spec = pltpu.PrefetchScalarGridSpec(
        num_scalar_prefetch=2,
        grid=(nb,),
        in_specs=[
            pl.BlockSpec((blk * ROW_TILE, LANES),
                         lambda i, be, nbk: (jnp.minimum(i, jnp.maximum(nbk[0] - 1, 0)), 0)),
            pl.BlockSpec((1, d, 2 * D_EXPERT), lambda i, be, nbk: (be[i], 0, 0)),
            pl.BlockSpec((1, D_EXPERT, d), lambda i, be, nbk: (be[i], 0, 0)),
        ],
        out_specs=pl.BlockSpec((blk * ROW_TILE, LANES), lambda i, be, nbk: (i, 0)),
        scratch_shapes=[
            pltpu.VMEM((d, 2 * D_EXPERT), BF16),
            pltpu.VMEM((D_EXPERT, d), BF16),
        ],
    )
    return pl.pallas_call(
        _expert_kernel,
        grid_spec=grid_spec,
        out_shape=jax.ShapeDtypeStruct(xs.shape, F32),
        compiler_params=_cparams(("arbitrary",)),
        name="experts",
    )(blk_e, nblk, xs, w_ei, w_eo)


def _combine_kernel(pos_ref, x_ref, mod_ref, wt_ref, ln_g_ref, ln_b_ref, y_hbm, o_ref,
                    ybuf, sem, *, alpha):
    b = pl.program_id(0)
    i = pl.program_id(1)
    nt = pl.num_programs(1)
    tm = x_ref.shape[1]
    step = b * nt + i
    nsteps = pl.num_programs(0) * nt

    def start_gather(s, slot):
        def body(r, carry):
            dst = pl.multiple_of(r * ROW_TILE, ROW_TILE)
            for k in range(TOP_K):
                src = pl.multiple_of(pos_ref[(s * TOP_K + k) * tm + r] * ROW_TILE, ROW_TILE)
                pltpu.make_async_copy(y_hbm.at[pl.ds(src, ROW_TILE), :],
                                      ybuf.at[slot, k, pl.ds(dst, ROW_TILE), :], sem.at[slot]).start()
            return carry
        lax.fori_loop(0, tm, body, 0, unroll=GATHER_UNROLL)

    @pl.when(step == 0)
    def _():
        start_gather(0, 0)

    @pl.when(step + 1 < nsteps)
    def _():
        start_gather(step + 1, (step + 1) % 2)

    slot = step % 2
    for k in range(TOP_K):
        pltpu.make_async_copy(y_hbm.at[pl.ds(0, tm * ROW_TILE), :], ybuf.at[slot, k],
                              sem.at[slot]).wait()
    wt = wt_ref[0]
    w0 = wt[:, 2:3]
    w1 = wt[:, 3:4]
    f = jnp.concatenate(
        [w0 * ybuf[slot, 0, pl.ds(j, tm, stride=ROW_TILE), :]
         + w1 * ybuf[slot, 1, pl.ds(j, tm, stride=ROW_TILE), :]
         for j in range(x_ref.shape[2] // LANES)], axis=1)
    x = x_ref[0]
    o_ref[0] = _ln(alpha * x + mod_ref[0, 5:6, :] * f) * ln_g_ref[...] + ln_b_ref[...]


def _combine(x1, mod, wts, pos, y, ln_g, ln_b, n_ctx_tiles, alpha):
    B, tn, D = x1.shape
    tm = TOK_TILE
    nt = tn // tm
    grid_spec = pltpu.PrefetchScalarGridSpec(
        num_scalar_prefetch=1,
        grid=(B, nt),
        in_specs=[
            pl.BlockSpec((1, tm, D), lambda b, i, pos: (b, i, 0)),
            pl.BlockSpec((1, 6, D), lambda b, i, pos: (jnp.where(i < n_ctx_tiles, B, b), 0, 0)),
            pl.BlockSpec((1, tm, ROW_TILE), lambda b, i, pos: (b, i, 0)),
            pl.BlockSpec((1, D), lambda b, i, pos: (0, 0)),
            pl.BlockSpec((1, D), lambda b, i, pos: (0, 0)),
            pl.BlockSpec(memory_space=pl.ANY),
        ],
        out_specs=pl.BlockSpec((1, tm, D), lambda b, i, pos: (b, i, 0)),
        scratch_shapes=[
            pltpu.VMEM((2, TOP_K, tm * ROW_TILE, LANES), F32),
            pltpu.SemaphoreType.DMA((2,)),
        ],
    )
    return pl.pallas_call(
        functools.partial(_combine_kernel, alpha=alpha),
        grid_spec=grid_spec,
        out_shape=jax.ShapeDtypeStruct((B, tn, D), F32),
        compiler_params=_cparams(("arbitrary", "arbitrary")),
        name="combine",
    )(pos, x1, mod, wts, ln_g, ln_b, y)


def _segments(hist, n_assign):
    counts = jnp.sum(hist[:, 0, N_GROUPS:N_ROUTE], axis=0).astype(jnp.int32)
    padded = (counts + EXP_BLK - 1) // EXP_BLK * EXP_BLK
    pad_end = jnp.cumsum(padded)
    pad_start = pad_end - padded
    p_rows = n_assign + N_EXPERTS * EXP_BLK
    nb = p_rows // EXP_BLK
    blk_first = jnp.arange(nb, dtype=jnp.int32) * EXP_BLK
    blk_e = jnp.minimum(jnp.sum((pad_end[None, :] <= blk_first[:, None]).astype(jnp.int32), axis=1),
                        N_EXPERTS - 1).astype(jnp.int32)
    nblk = (pad_end[-1] // EXP_BLK).astype(jnp.int32).reshape(1)
    return pad_start.astype(F32).reshape(N_EXPERTS, 1), pad_end.astype(jnp.int32), blk_e, nblk, p_rows


def _rope_tables(n_ctx, n_lat):
    nf = ATT_HEAD_DIM // 4
    inv = ROPE_BASE ** (-jnp.arange(nf, dtype=F32) / nf)
    rows = n_lat // GRID_W
    pos_r = jnp.repeat(jnp.arange(rows, dtype=F32), GRID_W)
    pos_c = jnp.tile(jnp.arange(GRID_W, dtype=F32), rows)
    ang_r = pos_r[:, None] * inv
    ang_c = pos_c[:, None] * inv
    cos_h = jnp.concatenate([jnp.cos(ang_r)] * 2 + [jnp.cos(ang_c)] * 2, axis=-1)
    sin_h = jnp.concatenate([-jnp.sin(ang_r), jnp.sin(ang_r),
                             -jnp.sin(ang_c), jnp.sin(ang_c)], axis=-1)
    reps = LANES // ATT_HEAD_DIM
    cos_l = jnp.tile(cos_h, (1, reps))
    sin_l = jnp.tile(sin_h, (1, reps))
    cos_t = jnp.concatenate([jnp.ones((n_ctx, LANES), F32), cos_l], axis=0)
    sin_t = jnp.concatenate([jnp.zeros((n_ctx, LANES), F32), sin_l], axis=0)
    return cos_t, sin_t


def _projection_weights(w_in_l):
    offs = np.cumsum((0, 3 * CONV_DIM, ATT_Q_DIM, ATT_KV_DIM, ATT_KV_DIM,
                      MLSTM_DIM, MLSTM_DIM, MLSTM_DIM, MLSTM_DIM, N_GATE)).tolist()
    a, q, k, v, mq, mk, mv, mo, g = [w_in_l[:, offs[n]:offs[n + 1]] for n in range(9)]
    hd = ATT_HEAD_DIM

    def swap(w):
        return jnp.concatenate([w[:, hd:], w[:, :hd]], axis=1)

    g_pad = jnp.pad(g, ((0, 0), (0, LANES - N_GATE)))
    w_tok = jnp.concatenate([a, q, k, swap(k), v, swap(v), mq, mk, g_pad], axis=1).astype(BF16)
    w_chan = jnp.concatenate([mv, mo, g], axis=1).T.astype(BF16)
    w_gate = w_in_l[:, offs[9]:].astype(BF16)
    return w_tok, w_chan, w_gate


def kernel(x, c, ctx, c_ctx, w_ada, b_ada, w_in, conv_w, attn_sink, mlstm_gate_b, mlstm_norm_w,
           w_proj_a, w_proj_b, w_proj_c, w_out, ln1_g, ln1_b, w_route_group, b_route_group,
           w_route_expert, b_route_expert, w_expert_in, w_expert_out, ln2_g, ln2_b):
    B, L, D = x.shape
    n_ctx = ctx.shape[1]
    depth = w_ada.shape[0]
    T = n_ctx + L
    alpha = (2 * depth) ** 0.25
    assert D == D_MODEL and n_ctx % TOK_TILE == 0 and L % TOK_TILE == 0 and L % GRID_W == 0
    assert MLSTM_CH == LANES and ATT_QB == LANES
    nct = n_ctx // TOK_TILE

    nrows = -(-(B + 1) // 8) * 8
    cond = jnp.concatenate([c, c_ctx[None, :], jnp.zeros((nrows - B - 1, D), F32)], axis=0)
    mod_all = _ada(cond, w_ada, b_ada).reshape(depth, nrows, 6, D)

    cos_t, sin_t = _rope_tables(n_ctx, L)
    xa = jnp.concatenate([ctx, x], axis=1)

    for i in range(depth):
        need_ctx = i < depth - 1
        mod = mod_all[i]
        w_tok, w_chan, w_gate = _projection_weights(w_in[i])

        a0u, q, kv, mqk, mvo_t, gcol, grow = _inproj(xa, mod, w_tok, w_chan, cos_t, sin_t, n_ctx)
        yb = _attention(q, kv, attn_sink[i], n_ctx)
        hf_t, hb_t = _mlstm(mqk, mvo_t, gcol, grow, mlstm_gate_b[i], n_ctx)

        off = 0 if need_ctx else nct
        w_route = jnp.pad(jnp.concatenate([w_route_group[i], w_route_expert[i]], axis=1).T,
                          ((0, LANES - N_ROUTE), (0, 0)))
        b_route = jnp.pad(jnp.concatenate([b_route_group[i], b_route_expert[i]]),
                          (0, LANES - N_ROUTE)).reshape(LANES, 1)
        x1, h2v, wts, route, hist = _merge(
            xa, mod, a0u, yb, hf_t, hb_t, mvo_t, conv_w[i], mlstm_norm_w[i].reshape(MLSTM_DIM, 1),
            w_gate, w_proj_a[i].astype(BF16), w_proj_b[i].astype(BF16), w_proj_c[i].astype(BF16),
            w_out[i].astype(BF16), ln1_g[i].reshape(1, D), ln1_b[i].reshape(1, D),
            w_route, b_route, n_ctx, off, alpha)

        tn = x1.shape[1]
        pstart, pend, blk_e, nblk, p_rows = _segments(hist, B * tn * TOP_K)
        pos = _positions(route, pstart, TOK_TILE).reshape(-1)
        xs = _dispatch(h2v, pos, pend, nblk, p_rows, TOK_TILE)
        y = _experts(xs, blk_e, nblk, w_expert_in[i], w_expert_out[i])
        xa = _combine(x1, mod, wts, pos, y,
                      ln2_g[i].reshape(1, D), ln2_b[i].reshape(1, D),
                      nct if need_ctx else 0, alpha)
    return xa
```

```python
import functools

import jax
import jax.numpy as jnp
import numpy as np
from jax import lax
from jax.experimental import pallas as pl
from jax.experimental.pallas import tpu as pltpu

D_MODEL = 1024
GRID_W = 64
CONV_DIM = 256
ATT_HEADS = 8
ATT_KV_HEADS = 2
ATT_HEAD_DIM = 64
ATT_WINDOW = 128
ROPE_BASE = 10000.0
MLSTM_HEADS = 4
MLSTM_HEAD_DIM = 64
MLSTM_DIM = MLSTM_HEADS * MLSTM_HEAD_DIM
N_GROUPS = 4
EXPERTS_PER_GROUP = 8
N_EXPERTS = N_GROUPS * EXPERTS_PER_GROUP
TOP_K = 2
D_EXPERT = D_MODEL // 2
LN_EPS = 1e-6
NEG_INF = -1e30

ATT_Q_DIM = ATT_HEADS * ATT_HEAD_DIM
ATT_KV_DIM = ATT_KV_HEADS * ATT_HEAD_DIM
N_GATE = 4 * MLSTM_HEADS
MIX_COLS = 3 * CONV_DIM + ATT_Q_DIM + 2 * ATT_KV_DIM + 4 * MLSTM_DIM
N_ROUTE = N_GROUPS + N_EXPERTS

LANES = 128
VMEM_LIMIT = 56 * 1024 * 1024
TOK_TILE = 256
SUB_TILE = 128
ATT_QB = 128
ATT_STEP = 256
MLSTM_CH = 128
MLSTM_STEP = 256
EXP_BLK = 256
HALO = 16
ROW_TILE = 8
GATHER_UNROLL = 8

F32 = jnp.float32
BF16 = jnp.bfloat16


def _cparams(sem):
    return pltpu.CompilerParams(dimension_semantics=sem, vmem_limit_bytes=VMEM_LIMIT)


def _ln(x, axis=-1):
    mu = jnp.mean(x, axis=axis, keepdims=True)
    xc = x - mu
    var = jnp.mean(xc * xc, axis=axis, keepdims=True)
    return xc * lax.rsqrt(var + LN_EPS)


def _sigmoid(x):
    return 0.5 * jnp.tanh(0.5 * x) + 0.5


def _split3(x):
    hi = x.astype(BF16)
    r1 = x - hi.astype(F32)
    mid = r1.astype(BF16)
    lo = (r1 - mid.astype(F32)).astype(BF16)
    return hi, mid, lo


def _dot(a, b):
    return jnp.dot(a, b, preferred_element_type=F32)


def _dot_nt(a, b):
    return lax.dot_general(a, b, (((1,), (1,)), ((), ())), preferred_element_type=F32)


def _dot3(x, rhs_b):
    return sum(_dot(part, rhs_b) for part in _split3(x))


def _dot3_l(lhs_b, x):
    return sum(_dot(lhs_b, part) for part in _split3(x))


def _ada_kernel(c_ref, w_ref, b_ref, o_ref):
    cv = c_ref[...]
    s = cv * _sigmoid(cv)
    w = w_ref[0]
    acc = jnp.zeros((s.shape[0], w.shape[1]), F32)
    w3 = _split3(w)
    for sp in _split3(s):
        for wp in w3:
            acc = acc + _dot(sp, wp)
    o_ref[0] = acc + b_ref[0]


def _ada(cond, w_ada, b_ada):
    depth, d, n6 = w_ada.shape
    rows = cond.shape[0]
    nt = n6 // d
    return pl.pallas_call(
        _ada_kernel,
        grid=(depth, nt),
        in_specs=[
            pl.BlockSpec((rows, d), lambda l, j: (0, 0)),
            pl.BlockSpec((1, d, d), lambda l, j: (l, 0, j)),
            pl.BlockSpec((1, 1, d), lambda l, j: (l, 0, j)),
        ],
        out_specs=pl.BlockSpec((1, rows, d), lambda l, j: (l, 0, j)),
        out_shape=jax.ShapeDtypeStruct((depth, rows, n6), F32),
        compiler_params=_cparams(("arbitrary", "arbitrary")),
        name="ada_mod",
    )(cond, w_ada, b_ada.reshape(depth, 1, n6))


def _rope(x, cos, sin_signed, lane):
    swapped = jnp.where(lane % 32 < 16,
                        pltpu.roll(x, LANES - 16, axis=1),
                        pltpu.roll(x, 16, axis=1))
    return x * cos + swapped * sin_signed


_OFF_A = 0
_OFF_Q = _OFF_A + 3 * CONV_DIM
_OFF_K = _OFF_Q + ATT_Q_DIM
_OFF_V = _OFF_K + 2 * ATT_KV_DIM
_OFF_MQK = _OFF_V + 2 * ATT_KV_DIM
_OFF_G = _OFF_MQK + 2 * MLSTM_DIM
_W_TOK_COLS = _OFF_G + LANES


def _inproj_kernel(x_ref, mod_ref, w_ref, wt_ref, cos_ref, sin_ref,
                   a0u_ref, q_ref, kv_ref, mqk_ref, mvo_t_ref, gcol_ref, grow_ref):
    x = x_ref[0]
    shift = mod_ref[0, 0:1, :]
    scale = mod_ref[0, 1:2, :]
    h = (_ln(x) * (1.0 + scale) + shift).astype(BF16)
    cos = cos_ref[...]
    sin = sin_ref[...]
    lane = lax.broadcasted_iota(jnp.int32, cos.shape, 1)

    def cols(lo, n):
        return _dot(h, w_ref[:, lo:lo + n])

    za = cols(_OFF_A, 3 * CONV_DIM)
    a0u_ref[0, :, 0:CONV_DIM] = za[:, 0:CONV_DIM].astype(BF16)
    a0u_ref[0, :, CONV_DIM:2 * CONV_DIM] = (
        za[:, CONV_DIM:2 * CONV_DIM] * za[:, 2 * CONV_DIM:3 * CONV_DIM]).astype(BF16)
    zq = cols(_OFF_Q, ATT_Q_DIM)
    qscale = ATT_HEAD_DIM ** -0.5
    for j in range(ATT_Q_DIM // LANES):
        piece = _rope(zq[:, j * LANES:(j + 1) * LANES], cos, sin, lane)
        q_ref[0, :, j * LANES:(j + 1) * LANES] = (piece * qscale).astype(BF16)
    zk = cols(_OFF_K, 2 * ATT_KV_DIM)
    for j in range(2):
        kv_ref[0, :, j * LANES:(j + 1) * LANES] = _rope(
            zk[:, j * LANES:(j + 1) * LANES], cos, sin, lane).astype(BF16)
    kv_ref[0, :, 2 * LANES:] = cols(_OFF_V, 2 * ATT_KV_DIM).astype(BF16)
    zm = cols(_OFF_MQK, 2 * MLSTM_DIM)
    mqk_ref[0, :, 0:MLSTM_DIM] = zm[:, 0:MLSTM_DIM].astype(BF16)
    mqk_ref[0, :, MLSTM_DIM:] = (zm[:, MLSTM_DIM:] * (MLSTM_HEAD_DIM ** -0.5)).astype(BF16)
    gcol_ref[0] = cols(_OFF_G, LANES)[:, 0:N_GATE]
    nchan = mvo_t_ref.shape[1]
    zt = _dot_nt(wt_ref[...], h)
    mvo_t_ref[0] = zt[0:nchan, :].astype(BF16)
    grow_ref[0] = zt[nchan:, :]


def _inproj(xa, mod, w_tok, w_chan, cos_t, sin_t, n_ctx):
    B, T, D = xa.shape
    tm = TOK_TILE
    nct = n_ctx // tm
    nchan = w_chan.shape[0] - N_GATE

    def tok(width):
        return pl.BlockSpec((1, tm, width), lambda b, i: (b, i, 0))

    def chan(rows):
        return pl.BlockSpec((1, rows, tm), lambda b, i: (b, 0, i))

    outs = [(2 * CONV_DIM, BF16), (ATT_Q_DIM, BF16), (4 * ATT_KV_DIM, BF16), (2 * MLSTM_DIM, BF16)]
    return pl.pallas_call(
        _inproj_kernel,
        grid=(B, T // tm),
        in_specs=[
            tok(D),
            pl.BlockSpec((1, 6, D), lambda b, i: (jnp.where(i < nct, B, b), 0, 0)),
            pl.BlockSpec(w_tok.shape, lambda b, i: (0, 0)),
            pl.BlockSpec(w_chan.shape, lambda b, i: (0, 0)),
            pl.BlockSpec((tm, LANES), lambda b, i: (i, 0)),
            pl.BlockSpec((tm, LANES), lambda b, i: (i, 0)),
        ],
        out_specs=[tok(w) for w, _ in outs] + [
            chan(nchan), tok(N_GATE), chan(N_GATE)],
        out_shape=[jax.ShapeDtypeStruct((B, T, w), dt) for w, dt in outs] + [
            jax.ShapeDtypeStruct((B, nchan, T), BF16),
            jax.ShapeDtypeStruct((B, T, N_GATE), F32), jax.ShapeDtypeStruct((B, N_GATE, T), F32)],
        compiler_params=_cparams(("parallel", "arbitrary")),
        name="in_proj",
    )(xa, mod, w_tok, w_chan, cos_t, sin_t)


def _attn_kernel(sink_ref, q_ref, kvc_ref, kvp_ref, kvm_ref, kvn_ref, o_ref, *, n_ctx_blk, n_blk):
    n_ctx = kvc_ref.shape[1]
    qb = kvp_ref.shape[1]
    nsub = q_ref.shape[1] // qb
    nk = n_ctx + 3 * qb
    half = LANES // 2

    lane_row = lax.broadcasted_iota(jnp.int32, (1, LANES), 1)
    keep = [(lane_row < half).astype(F32).astype(BF16), (lane_row >= half).astype(F32).astype(BF16)]
    pad_v = jnp.zeros((qb, LANES), BF16)
    qi = lax.broadcasted_iota(jnp.int32, (qb, qb), 0)
    ki = lax.broadcasted_iota(jnp.int32, (qb, qb), 1)
    lane_q = lax.broadcasted_iota(jnp.int32, (qb, LANES), 1)
    neg = jnp.full((qb, LANES), NEG_INF, F32)
    group = ATT_HEADS // ATT_KV_HEADS
    around = [kvp_ref[0]] + [kvm_ref[0, sb * qb:(sb + 1) * qb, :] for sb in range(nsub)] + [kvn_ref[0]]

    for sb in range(nsub):
        g = pl.program_id(1) * nsub + sb
        is_lat = g >= n_ctx_blk
        has_prev = g >= n_ctx_blk + 1
        has_next = g < n_blk - 1
        kv_all = jnp.concatenate([kvc_ref[0]] + around[sb:sb + 3], axis=0)

        def k_ext(tile, par):
            return kv_all[:, tile * LANES:(tile + 1) * LANES] * keep[par]

        def v_ext(tile, par):
            v = jnp.concatenate([kv_all[:, (2 + tile) * LANES:(3 + tile) * LANES], pad_v], axis=0)
            return jnp.concatenate([v * keep[par], jnp.broadcast_to(keep[par], v.shape)], axis=1)

        ok_prev = jnp.logical_and(ki >= qi, has_prev)
        ok_next = jnp.logical_and(ki <= qi, jnp.logical_and(has_next, is_lat))
        q = q_ref[0, sb * qb:(sb + 1) * qb, :]
        outs = []
        for pair in range(ATT_HEADS // 2):
            kvh = (2 * pair) // group
            q_pair = q[:, pair * LANES:(pair + 1) * LANES]
            acc = None
            for par in range(2):
                h = 2 * pair + par
                tile = 0 if kvh == par else 1
                s = _dot_nt(q_pair, k_ext(tile, par))
                s_ext = jnp.concatenate([
                    s[:, 0:n_ctx],
                    jnp.where(ok_prev, s[:, n_ctx:n_ctx + qb], neg),
                    jnp.where(is_lat, s[:, n_ctx + qb:n_ctx + 2 * qb], neg),
                    jnp.where(ok_next, s[:, n_ctx + 2 * qb:nk], neg),
                    jnp.where(lane_q == 0, sink_ref[h], neg)], axis=1)
                m = jnp.max(s_ext, axis=-1, keepdims=True)
                p = jnp.exp(s_ext - m).astype(BF16)
                part = _dot(p, v_ext(tile, par))
                acc = part if acc is None else acc + part
            outs.append(acc[:, 0:LANES] / acc[:, LANES:])
        o_ref[0, sb * qb:(sb + 1) * qb, :] = jnp.concatenate(outs, axis=-1).astype(o_ref.dtype)


def _attention(q, kv, sink, n_ctx):
    B, T, _ = q.shape
    qb = ATT_QB
    nsub = ATT_STEP // qb
    n_blk = T // qb
    nq = T // ATT_STEP
    n_ctx_blk = n_ctx // qb
    kvw = kv.shape[2]
    grid_spec = pltpu.PrefetchScalarGridSpec(
        num_scalar_prefetch=1,
        grid=(B, nq),
        in_specs=[
            pl.BlockSpec((1, ATT_STEP, ATT_Q_DIM), lambda b, j, sk: (b, j, 0)),
            pl.BlockSpec((1, n_ctx, kvw), lambda b, j, sk: (b, 0, 0)),
            pl.BlockSpec((1, qb, kvw), lambda b, j, sk: (b, jnp.maximum(j * nsub - 1, 0), 0)),
            pl.BlockSpec((1, ATT_STEP, kvw), lambda b, j, sk: (b, j, 0)),
            pl.BlockSpec((1, qb, kvw), lambda b, j, sk: (b, jnp.minimum((j + 1) * nsub, n_blk - 1), 0)),
        ],
        out_specs=pl.BlockSpec((1, ATT_STEP, ATT_Q_DIM), lambda b, j, sk: (b, j, 0)),
    )
    return pl.pallas_call(
        functools.partial(_attn_kernel, n_ctx_blk=n_ctx_blk, n_blk=n_blk),
        grid_spec=grid_spec,
        out_shape=jax.ShapeDtypeStruct((B, T, ATT_Q_DIM), BF16),
        compiler_params=_cparams(("parallel", "arbitrary")),
        name="window_attn",
    )(sink, q, kv, kv, kv, kv)


def _log_sigmoid(x):
    return jnp.minimum(x, 0.0) - jnp.log1p(jnp.exp(-jnp.abs(x)))


def _rows_to_lanes(a, base):
    return jnp.concatenate([a[base + h:base + h + 1, :] for h in range(MLSTM_HEADS)], axis=1)


def _mlstm_dir(d, qk, v_t, gcol, grow, ct_bd, n_bd, m_prev):
    ch = qk.shape[0]
    nh = MLSTM_HEADS
    dh = MLSTM_HEAD_DIM
    wide = nh * ch
    fwd = d == 0

    r_i = lax.broadcasted_iota(jnp.int32, (ch, ch), 0)
    c_i = lax.broadcasted_iota(jnp.int32, (ch, ch), 1)
    seen_t = (r_i <= c_i) if fwd else (r_i >= c_i)
    seen_tt = (c_i <= r_i) if fwd else (c_i >= r_i)
    r_w = lax.broadcasted_iota(jnp.int32, (ch, wide), 0)
    s_w = lax.broadcasted_iota(jnp.int32, (ch, wide), 1) % ch
    seen_w = (r_w <= s_w) if fwd else (r_w >= s_w)

    base_i = 2 * d * nh
    base_f = base_i + nh

    lf_col = _log_sigmoid(gcol)
    cum_col = _dot3_l(seen_tt.astype(BF16), lf_col)
    lane16 = lax.broadcasted_iota(jnp.int32, gcol.shape, 1)
    z = jnp.where(jnp.logical_and(lane16 >= base_i, lane16 < base_f), gcol, -cum_col)
    ch16 = lax.broadcasted_iota(jnp.int32, (N_GATE, wide), 0)
    hd16 = lax.broadcasted_iota(jnp.int32, (N_GATE, wide), 1) // ch
    sel = jnp.logical_or(ch16 == base_i + hd16, ch16 == base_f + hd16).astype(BF16)
    x_t = _dot3(z, sel)

    lf_row = _log_sigmoid(grow)
    rhs2 = jnp.concatenate([seen_t.astype(BF16), jnp.ones((ch, ch), BF16)], axis=1)
    rows = _dot3(lf_row, rhs2)
    b_all = _rows_to_lanes(rows[:, 0:ch], base_f)
    g_all = _rows_to_lanes(rows[:, ch:], base_f)
    li_all = _rows_to_lanes(grow, base_i)

    dmat = jnp.where(seen_w, x_t + b_all, -jnp.inf)
    a = b_all + m_prev
    m = jnp.maximum(a, jnp.max(dmat, axis=0, keepdims=True))
    w_intra = jnp.exp(dmat - m)
    e_inter = jnp.exp(a - m)

    q = qk[:, 0:MLSTM_DIM]
    k = qk[:, MLSTM_DIM:]
    lb = lax.broadcasted_iota(jnp.int32, (1, MLSTM_DIM), 1) // dh
    q_bd = jnp.concatenate([q * (lb == h).astype(F32).astype(BF16) for h in range(nh)], axis=0)
    s_t = _dot_nt(k, q_bd) * w_intra
    nq = _rows_to_lanes(_dot_nt(n_bd.astype(BF16), q), 0)
    den = jnp.sum(s_t, axis=0, keepdims=True) + e_inter * nq
    inv = 1.0 / jnp.maximum(jnp.abs(den), jnp.exp(-m))
    inter_t = _dot_nt(ct_bd.astype(BF16), q)
    s_b = s_t.astype(BF16)
    outs = []
    for h in range(nh):
        seg = slice(h * ch, (h + 1) * ch)
        blk = slice(h * dh, (h + 1) * dh)
        num = _dot(v_t[blk, :], s_b[:, seg]) + e_inter[:, seg] * inter_t[blk, :]
        outs.append(num * inv[:, seg])
    h_t = jnp.concatenate(outs, axis=0)

    m_loc = g_all + jnp.max(x_t, axis=0, keepdims=True)
    m_new = jnp.maximum(g_all + m_prev, m_loc)
    sa = jnp.exp(g_all + m_prev - m_new)
    sb = jnp.exp(m_loc - m_new)
    e_loc = jnp.exp(g_all - b_all + li_all - m_loc)
    v_e = jnp.concatenate(
        [v_t[h * dh:(h + 1) * dh, :].astype(F32) * e_loc[:, h * ch:(h + 1) * ch] for h in range(nh)],
        axis=0).astype(BF16)
    ct_loc = _dot(v_e, k)
    reps = MLSTM_DIM // ch

    def per_head_rows(row, nrows):
        return jnp.concatenate(
            [jnp.broadcast_to(jnp.concatenate([row[:, h * ch:(h + 1) * ch]] * reps, axis=1),
                              (nrows, MLSTM_DIM)) for h in range(nh)], axis=0)

    eb = lax.broadcasted_iota(jnp.int32, (MLSTM_DIM, MLSTM_DIM), 0) // dh
    db = lax.broadcasted_iota(jnp.int32, (MLSTM_DIM, MLSTM_DIM), 1) // dh
    ct_new = jnp.where(eb == db, per_head_rows(sa, dh) * ct_bd + per_head_rows(sb, dh) * ct_loc, 0.0)
    nrow = n_bd.shape[0]
    e_rows = jnp.concatenate([e_loc[:, h * ch:(h + 1) * ch] for h in range(nh)]
                             + [jnp.zeros((nrow - nh, ch), F32)], axis=0).astype(BF16)
    n_loc = _dot(e_rows, k)
    pad = jnp.zeros((nrow - nh, MLSTM_DIM), F32)
    sa8 = jnp.concatenate([per_head_rows(sa, 1), pad], axis=0)
    sb8 = jnp.concatenate([per_head_rows(sb, 1), pad], axis=0)
    hb8 = lax.broadcasted_iota(jnp.int32, (nrow, MLSTM_DIM), 0)
    db8 = lax.broadcasted_iota(jnp.int32, (nrow, MLSTM_DIM), 1) // dh
    n_new = jnp.where(hb8 == db8, sa8 * n_bd + sb8 * n_loc, 0.0)
    return h_t, ct_new, n_new, m_new


def _mlstm_kernel(qk_f_ref, qk_b_ref, vt_f_ref, vt_b_ref, gc_f_ref, gc_b_ref, gr_f_ref, gr_b_ref,
                  bcol_ref, brow_ref, hf_ref, hb_ref, c_scr, n_scr, m_scr):
    j = pl.program_id(1)

    @pl.when(j == 0)
    def _():
        c_scr[...] = jnp.zeros_like(c_scr)
        n_scr[...] = jnp.zeros_like(n_scr)
        m_scr[...] = jnp.zeros_like(m_scr)

    states = [(c_scr[d], n_scr[d], m_scr[d]) for d in range(2)]
    dirs = ((qk_f_ref, vt_f_ref, gc_f_ref, gr_f_ref, hf_ref),
            (qk_b_ref, vt_b_ref, gc_b_ref, gr_b_ref, hb_ref))
    ch = MLSTM_CH
    nsub = qk_f_ref.shape[1] // ch
    for d, (qk_ref, vt_ref, gc_ref, gr_ref, out_ref) in enumerate(dirs):
        st = states[d]
        for sc in (range(nsub) if d == 0 else reversed(range(nsub))):
            ts = slice(sc * ch, (sc + 1) * ch)
            h_t, *st = _mlstm_dir(
                d, qk_ref[0, ts, :], vt_ref[0, :, ts],
                gc_ref[0, ts, :] + bcol_ref[...], gr_ref[0, :, ts] + brow_ref[...], *st)
            out_ref[0, :, ts] = h_t
        states[d] = st
    for d in range(2):
        c_scr[d] = states[d][0]
        n_scr[d] = states[d][1]
        m_scr[d] = states[d][2]


def _mlstm(mqk, mvo_t, gcol, grow, gate_b, n_ctx):
    B, T, _ = mqk.shape
    ch = MLSTM_STEP
    nc = T // ch
    ncc = n_ctx // ch

    def rev(j):
        return jnp.where(j < ncc, ncc - 1 - j, nc - 1 - (j - ncc))

    def tok(width, order):
        return pl.BlockSpec((1, ch, width), lambda b, j: (b, order(j), 0))

    def chan(rows, order):
        return pl.BlockSpec((1, rows, ch), lambda b, j: (b, 0, order(j)))

    ident = lambda j: j
    return pl.pallas_call(
        _mlstm_kernel,
        grid=(B, nc),
        in_specs=[
            tok(2 * MLSTM_DIM, ident), tok(2 * MLSTM_DIM, rev),
            chan(MLSTM_DIM, ident), chan(MLSTM_DIM, rev),
            tok(N_GATE, ident), tok(N_GATE, rev),
            chan(N_GATE, ident), chan(N_GATE, rev),
            pl.BlockSpec((1, N_GATE), lambda b, j: (0, 0)),
            pl.BlockSpec((N_GATE, 1), lambda b, j: (0, 0)),
        ],
        out_specs=[chan(MLSTM_DIM, ident), chan(MLSTM_DIM, rev)],
        out_shape=[jax.ShapeDtypeStruct((B, MLSTM_DIM, T), F32)] * 2,
        scratch_shapes=[
            pltpu.VMEM((2, MLSTM_DIM, MLSTM_DIM), F32),
            pltpu.VMEM((2, 8, MLSTM_DIM), F32),
            pltpu.VMEM((2, 1, MLSTM_HEADS * MLSTM_CH), F32),
        ],
        compiler_params=_cparams(("parallel", "arbitrary")),
        name="mlstm",
    )(mqk, mqk, mvo_t, mvo_t, gcol, gcol, grow, grow,
      gate_b.reshape(1, N_GATE), gate_b.reshape(N_GATE, 1))


def _merge_kernel(x_ref, mod_ref, a0u_ref, up_ref, un_ref, yb_ref, hf_ref, hb_ref, mvo_t_ref,
                  convw_ref, normw_ref, wg_ref, wpa_ref, wpb_ref, wpc_ref, wo_ref,
                  ln_g_ref, ln_b_ref, wr_ref, br_ref,
                  x1_ref, h2_ref, wt_ref, route_ref, hist_ref, *, off, n_ctx_tiles, alpha):
    i = pl.program_id(1) + off
    nt = pl.num_programs(1) + off
    tm = x_ref.shape[1]
    d = x_ref.shape[2]

    a0 = a0u_ref[0, :, 0:CONV_DIM].astype(F32)
    u = a0u_ref[0, :, CONV_DIM:].astype(F32)
    prev_ok = jnp.logical_and(i != 0, i != n_ctx_tiles)
    next_ok = jnp.logical_and(i != n_ctx_tiles - 1, i != nt - 1)
    u_prev = jnp.where(prev_ok, up_ref[0, HALO - 1:HALO, CONV_DIM:].astype(F32), 0.0)
    u_next = jnp.where(next_ok, un_ref[0, 0:1, CONV_DIM:].astype(F32), 0.0)
    row = lax.broadcasted_iota(jnp.int32, u.shape, 0)
    u_dn = jnp.where(row == 0, u_prev, pltpu.roll(u, 1, axis=0))
    u_up = jnp.where(row == tm - 1, u_next, pltpu.roll(u, tm - 1, axis=0))
    cw = convw_ref[...]
    ya_all = (a0 * (u_dn * cw[0:1, :] + u * cw[1:2, :] + u_up * cw[2:3, :])).astype(BF16)

    wr = wr_ref[...]
    w_hi = wr.astype(BF16)
    w_lo = (wr - w_hi.astype(F32)).astype(BF16)
    sub = lax.broadcasted_iota(jnp.int32, (wr.shape[0], SUB_TILE), 0)
    big = jnp.int32(2 * LANES)
    is_g = sub < N_GROUPS
    hist = jnp.zeros(hist_ref.shape[1:], F32)

    for part in range(tm // SUB_TILE):
        rs = slice(part * SUB_TILE, (part + 1) * SUB_TILE)
        x = x_ref[0, rs, :]
        h = (_ln(x) * (1.0 + mod_ref[0, 1:2, :]) + mod_ref[0, 0:1, :]).astype(BF16)

        hm = hf_ref[0, :, rs] + hb_ref[0, :, rs]
        normed = jnp.concatenate(
            [_ln(hm[hd * MLSTM_HEAD_DIM:(hd + 1) * MLSTM_HEAD_DIM, :], axis=0)
             for hd in range(MLSTM_HEADS)], axis=0)
        yc_t = _sigmoid(mvo_t_ref[0, :, rs].astype(F32)) * (normed * normw_ref[...])
        yc = yc_t.T

        pa = _dot(ya_all[rs, :], wpa_ref[...])
        pb = _dot(yb_ref[0, rs, :], wpb_ref[...])
        pc = _dot(yc.astype(BF16), wpc_ref[...])
        merged = (_sigmoid(_dot(h, wg_ref[:, 0:d])) * pa
                  + _sigmoid(_dot(h, wg_ref[:, d:2 * d])) * pb
                  + _sigmoid(_dot(h, wg_ref[:, 2 * d:3 * d])) * pc)
        yl = _dot(merged.astype(BF16), wo_ref[...])

        x1 = _ln(alpha * x + mod_ref[0, 2:3, :] * yl) * ln_g_ref[...] + ln_b_ref[...]
        x1_ref[0, rs, :] = x1
        h2 = _ln(x1) * (1.0 + mod_ref[0, 4:5, :]) + mod_ref[0, 3:4, :]
        for s in range(d // LANES):
            h2_ref[pl.ds(part * SUB_TILE * ROW_TILE + s, SUB_TILE, stride=ROW_TILE), :] = (
                h2[:, s * LANES:(s + 1) * LANES])

        h_hi = h2.astype(BF16)
        h_lo = (h2 - h_hi.astype(F32)).astype(BF16)
        lg = _dot_nt(w_hi, h_hi) + _dot_nt(w_hi, h_lo) + _dot_nt(w_lo, h_hi) + br_ref[...]
        gl = jnp.where(is_g, lg, -jnp.inf)
        g_max = jnp.max(gl, axis=0, keepdims=True)
        g_sel = jnp.min(jnp.where(gl == g_max, sub, big), axis=0, keepdims=True)
        g_p = 1.0 / jnp.sum(jnp.where(is_g, jnp.exp(gl - g_max), 0.0), axis=0, keepdims=True)
        lo = N_GROUPS + EXPERTS_PER_GROUP * g_sel
        el = jnp.where(jnp.logical_and(sub >= lo, sub < lo + EXPERTS_PER_GROUP), lg, -jnp.inf)
        e1 = jnp.max(el, axis=0, keepdims=True)
        i1 = jnp.min(jnp.where(el == e1, sub, big), axis=0, keepdims=True)
        el2 = jnp.where(sub == i1, -jnp.inf, el)
        e2 = jnp.max(el2, axis=0, keepdims=True)
        i2 = jnp.min(jnp.where(el2 == e2, sub, big), axis=0, keepdims=True)
        t = jnp.exp(e2 - e1)
        w1 = g_p / (1.0 + t)
        w2 = w1 * t
        rows = jnp.concatenate([(i1 - N_GROUPS).astype(F32), (i2 - N_GROUPS).astype(F32), w1, w2,
                                jnp.zeros((4, SUB_TILE), F32)], axis=0)
        route_ref[0, :, rs] = rows
        wt_ref[0, rs, :] = rows.T
        chosen = jnp.where(jnp.logical_or(sub == i1, sub == i2), 1.0, 0.0).astype(BF16)
        hist = hist + _dot_nt(jnp.ones((ROW_TILE, SUB_TILE), BF16), chosen)
    hist_ref[0] = hist


def _merge(xa, mod, a0u, yb, hf_t, hb_t, mvo_t, conv_w, norm_w, w_gate, w_pa, w_pb, w_pc, w_o,
           ln_g, ln_b, w_route, b_route, n_ctx, off, alpha):
    B, T, D = xa.shape
    tm = TOK_TILE
    nct = n_ctx // tm
    nt = T // tm - off
    tn = nt * tm
    hpt = tm // HALO
    nhalo = T // HALO

    def tok(width):
        return pl.BlockSpec((1, tm, width), lambda b, i: (b, i + off, 0))

    def chan(rows):
        return pl.BlockSpec((1, rows, tm), lambda b, i: (b, 0, i + off))

    def full(a):
        return pl.BlockSpec(a.shape, lambda b, i: (0,) * a.ndim)

    def otok(width):
        return pl.BlockSpec((1, tm, width), lambda b, i: (b, i, 0))

    consts = [conv_w, norm_w, w_gate, w_pa, w_pb, w_pc, w_o, ln_g, ln_b, w_route, b_route]
    return pl.pallas_call(
        functools.partial(_merge_kernel, off=off, n_ctx_tiles=nct, alpha=alpha),
        grid=(B, nt),
        in_specs=[
            tok(D),
            pl.BlockSpec((1, 6, D), lambda b, i: (jnp.where(i + off < nct, B, b), 0, 0)),
            tok(2 * CONV_DIM),
            pl.BlockSpec((1, HALO, 2 * CONV_DIM),
                         lambda b, i: (b, jnp.maximum((i + off) * hpt - 1, 0), 0)),
            pl.BlockSpec((1, HALO, 2 * CONV_DIM),
                         lambda b, i: (b, jnp.minimum((i + off + 1) * hpt, nhalo - 1), 0)),
            tok(ATT_Q_DIM), chan(MLSTM_DIM), chan(MLSTM_DIM),
            pl.BlockSpec((1, MLSTM_DIM, tm), lambda b, i: (b, 1, i + off)),

        ] + [full(a) for a in consts],
        out_specs=[otok(D),
                   pl.BlockSpec((tm * ROW_TILE, LANES), lambda b, i: (b * nt + i, 0)),
                   otok(ROW_TILE),
                   pl.BlockSpec((1, ROW_TILE, tm), lambda b, i: (b, 0, i)),
                   pl.BlockSpec((1, ROW_TILE, LANES), lambda b, i: (b * nt + i, 0, 0))],
        out_shape=[jax.ShapeDtypeStruct((B, tn, D), F32),
                   jax.ShapeDtypeStruct((B * tn * ROW_TILE, LANES), F32),
                   jax.ShapeDtypeStruct((B, tn, ROW_TILE), F32),
                   jax.ShapeDtypeStruct((B, ROW_TILE, tn), F32),
                   jax.ShapeDtypeStruct((B * nt, ROW_TILE, LANES), F32)],
        compiler_params=_cparams(("parallel", "arbitrary")),
        name="merge",
    )(xa, mod, a0u, a0u, a0u, yb, hf_t, hb_t, mvo_t, *consts)


def _positions_kernel(route_ref, pstart_ref, upper_ref, pos_ref, carry):
    step = pl.program_id(0) * pl.num_programs(1) + pl.program_id(1)

    @pl.when(step == 0)
    def _():
        carry[...] = jnp.zeros_like(carry)

    r = route_ref[0]
    eid = jnp.concatenate([r[0:1, :], r[1:2, :]], axis=1)
    sub = lax.broadcasted_iota(jnp.int32, (N_EXPERTS, eid.shape[1]), 0).astype(F32)
    onehot = jnp.where(sub == eid, 1.0, 0.0)
    earlier = _dot(onehot.astype(BF16), upper_ref[...])
    seen = carry[...]
    pos = jnp.sum(onehot * (earlier + (seen + pstart_ref[...])), axis=0, keepdims=True)
    pos_ref[0] = pos.astype(jnp.int32)
    carry[...] = seen + jnp.sum(onehot, axis=1, keepdims=True)


def _positions(route, pstart, tm):
    B, _, tn = route.shape
    nt = tn // tm
    a_i = lax.broadcasted_iota(jnp.int32, (TOP_K * tm, TOP_K * tm), 0)
    b_i = lax.broadcasted_iota(jnp.int32, (TOP_K * tm, TOP_K * tm), 1)
    upper = (a_i < b_i).astype(BF16)
    return pl.pallas_call(
        _positions_kernel,
        grid=(B, nt),
        in_specs=[
            pl.BlockSpec((1, ROW_TILE, tm), lambda b, i: (b, 0, i)),
            pl.BlockSpec(pstart.shape, lambda b, i: (0, 0)),
            pl.BlockSpec(upper.shape, lambda b, i: (0, 0)),
        ],
        out_specs=pl.BlockSpec((1, 1, TOP_K * tm), lambda b, i: (b * nt + i, 0, 0)),
        out_shape=jax.ShapeDtypeStruct((B * nt, 1, TOP_K * tm), jnp.int32),
        scratch_shapes=[pltpu.VMEM((N_EXPERTS, 1), F32)],
        compiler_params=_cparams(("arbitrary", "arbitrary")),
        name="positions",
    )(route, pstart, upper)


def _dispatch_kernel(pos_ref, pend_ref, nblk_ref, h_ref, xs_hbm, zbuf, sem, zsem):
    s = pl.program_id(0)
    tm = h_ref.shape[0] // ROW_TILE
    zrows = zbuf.shape[0]
    last_blk = xs_hbm.shape[0] // zrows - 1

    @pl.when(s == 0)
    def _():
        zbuf[...] = jnp.zeros_like(zbuf)

        def zero_copy(start):
            return pltpu.make_async_copy(
                zbuf, xs_hbm.at[pl.ds(pl.multiple_of(start, ROW_TILE), zrows), :], zsem)

        jobs = []
        for e in range(N_EXPERTS):
            before = pend_ref[e - 1] if e > 0 else 0
            jobs.append((pend_ref[e] * ROW_TILE - zrows, pend_ref[e] > before))
        for e in range(N_EXPERTS):
            jobs.append(((nblk_ref[0] + e) * zrows, nblk_ref[0] + e <= last_blk))
        for st, needed in jobs:
            @pl.when(needed)
            def _(st=st):
                zero_copy(st).start()
        for st, needed in jobs:
            @pl.when(needed)
            def _(st=st):
                zero_copy(st).wait()

    def row_copy(r, k):
        src = pl.multiple_of(r * ROW_TILE, ROW_TILE)
        dst = pl.multiple_of(pos_ref[(s * TOP_K + k) * tm + r] * ROW_TILE, ROW_TILE)
        return pltpu.make_async_copy(h_ref.at[pl.ds(src, ROW_TILE), :],
                                     xs_hbm.at[pl.ds(dst, ROW_TILE), :], sem)

    def body(r, carry):
        for k in range(TOP_K):
            row_copy(r, k).start()
        return carry
    lax.fori_loop(0, tm, body, 0, unroll=GATHER_UNROLL)
    for k in range(TOP_K):
        pltpu.make_async_copy(h_ref, xs_hbm.at[pl.ds(0, tm * ROW_TILE), :], sem).wait()


def _dispatch(h2v, pos, pend, nblk, p_rows, tm):
    rows = h2v.shape[0]
    nsteps = rows // (tm * ROW_TILE)
    grid_spec = pltpu.PrefetchScalarGridSpec(
        num_scalar_prefetch=3,
        grid=(nsteps,),
        in_specs=[pl.BlockSpec((tm * ROW_TILE, LANES), lambda s, pos, pend, nbk: (s, 0))],
        out_specs=pl.BlockSpec(memory_space=pl.ANY),
        scratch_shapes=[
            pltpu.VMEM((EXP_BLK * ROW_TILE, LANES), F32),
            pltpu.SemaphoreType.DMA,
            pltpu.SemaphoreType.DMA,
        ],
    )
    return pl.pallas_call(
        _dispatch_kernel,
        grid_spec=grid_spec,
        out_shape=jax.ShapeDtypeStruct((p_rows * ROW_TILE, LANES), F32),
        compiler_params=_cparams(("arbitrary",)),
        name="dispatch",
    )(pos, pend, nblk, h2v)


def _expert_kernel(blke_ref, nblk_ref, xs_ref, wi_ref, wo_ref, y_ref, wi_bf, wo_bf):
    i = pl.program_id(0)
    nb = nblk_ref[0]
    blk = xs_ref.shape[0] // ROW_TILE
    nsl = wi_ref.shape[2] // LANES

    @pl.when(i < nb)
    def _():
        e_now = blke_ref[i]
        e_before = blke_ref[jnp.maximum(i - 1, 0)]

        @pl.when(jnp.logical_or(i == 0, e_now != e_before))
        def _():
            wi_bf[...] = wi_ref[0, 0].astype(BF16)
            wo_bf[...] = wo_ref[0, 0].astype(BF16)

        xin = jnp.concatenate(
            [xs_ref[pl.ds(s, blk, stride=ROW_TILE), :] for s in range(nsl)], axis=1).astype(BF16)
        mid = _dot(xin, wi_bf[...])
        gt = mid[:, 0:D_EXPERT]
        up = mid[:, D_EXPERT:]
        act = (gt * _sigmoid(gt)) * up
        y = _dot(act.astype(BF16), wo_bf[...])
        for s in range(nsl):
            y_ref[pl.ds(s, blk, stride=ROW_TILE), :] = y[:, s * LANES:(s + 1) * LANES]

    @pl.when(i >= nb)
    def _():
        y_ref[...] = jnp.zeros_like(y_ref)


def _experts(xs, blk_e, nblk, w_ei, w_eo, layer):
    d = w_ei.shape[2]
    blk = EXP_BLK
    nb = xs.shape[0] // (blk * ROW_TILE)
    grid_spec = pltpu.PrefetchScalarGridSpec(
        num_scalar_prefetch=2,
        grid=(nb,),
        in_specs=[
            pl.BlockSpec((blk * ROW_TILE, LANES),
                         lambda i, be, nbk: (jnp.minimum(i, jnp.maximum(nbk[0] - 1, 0)), 0)),
            pl.BlockSpec((1, 1, d, 2 * D_EXPERT), lambda i, be, nbk: (layer, be[i], 0, 0)),
            pl.BlockSpec((1, 1, D_EXPERT, d), lambda i, be, nbk: (layer, be[i], 0, 0)),
        ],
        out_specs=pl.BlockSpec((blk * ROW_TILE, LANES), lambda i, be, nbk: (i, 0)),
        scratch_shapes=[
            pltpu.VMEM((d, 2 * D_EXPERT), BF16),
            pltpu.VMEM((D_EXPERT, d), BF16),
        ],
    )
    return pl.pallas_call(
        _expert_kernel,
        grid_spec=grid_spec,
        out_shape=jax.ShapeDtypeStruct(xs.shape, F32),
        compiler_params=_cparams(("arbitrary",)),
        name="experts",
    )(blk_e, nblk, xs, w_ei, w_eo)


def _combine_kernel(pos_ref, x_ref, mod_ref, wt_ref, ln_g_ref, ln_b_ref, y_hbm, o_ref,
                    ybuf, sem, *, alpha):
    b = pl.program_id(0)
    i = pl.program_id(1)
    nt = pl.num_programs(1)
    tm = x_ref.shape[1]
    step = b * nt + i
    nsteps = pl.num_programs(0) * nt

    def start_gather(s, slot):
        def body(r, carry):
            dst = pl.multiple_of(r * ROW_TILE, ROW_TILE)
            for k in range(TOP_K):
                src = pl.multiple_of(pos_ref[(s * TOP_K + k) * tm + r] * ROW_TILE, ROW_TILE)
                pltpu.make_async_copy(y_hbm.at[pl.ds(src, ROW_TILE), :],
                                      ybuf.at[slot, k, pl.ds(dst, ROW_TILE), :], sem.at[slot]).start()
            return carry
        lax.fori_loop(0, tm, body, 0, unroll=GATHER_UNROLL)

    @pl.when(step == 0)
    def _():
        start_gather(0, 0)

    @pl.when(step + 1 < nsteps)
    def _():
        start_gather(step + 1, (step + 1) % 2)

    slot = step % 2
    for k in range(TOP_K):
        pltpu.make_async_copy(y_hbm.at[pl.ds(0, tm * ROW_TILE), :], ybuf.at[slot, k],
                              sem.at[slot]).wait()
    wt = wt_ref[0]
    w0 = wt[:, 2:3]
    w1 = wt[:, 3:4]
    f = jnp.concatenate(
        [w0 * ybuf[slot, 0, pl.ds(j, tm, stride=ROW_TILE), :]
         + w1 * ybuf[slot, 1, pl.ds(j, tm, stride=ROW_TILE), :]
         for j in range(x_ref.shape[2] // LANES)], axis=1)
    x = x_ref[0]
    o_ref[0] = _ln(alpha * x + mod_ref[0, 5:6, :] * f) * ln_g_ref[...] + ln_b_ref[...]


def _combine(x1, mod, wts, pos, y, ln_g, ln_b, n_ctx_tiles, alpha):
    B, tn, D = x1.shape
    tm = TOK_TILE
    nt = tn // tm
    grid_spec = pltpu.PrefetchScalarGridSpec(
        num_scalar_prefetch=1,
        grid=(B, nt),
        in_specs=[
            pl.BlockSpec((1, tm, D), lambda b, i, pos: (b, i, 0)),
            pl.BlockSpec((1, 6, D), lambda b, i, pos: (jnp.where(i < n_ctx_tiles, B, b), 0, 0)),
            pl.BlockSpec((1, tm, ROW_TILE), lambda b, i, pos: (b, i, 0)),
            pl.BlockSpec((1, D), lambda b, i, pos: (0, 0)),
            pl.BlockSpec((1, D), lambda b, i, pos: (0, 0)),
            pl.BlockSpec(memory_space=pl.ANY),
        ],
        out_specs=pl.BlockSpec((1, tm, D), lambda b, i, pos: (b, i, 0)),
        scratch_shapes=[
            pltpu.VMEM((2, TOP_K, tm * ROW_TILE, LANES), F32),
            pltpu.SemaphoreType.DMA((2,)),
        ],
    )
    return pl.pallas_call(
        functools.partial(_combine_kernel, alpha=alpha),
        grid_spec=grid_spec,
        out_shape=jax.ShapeDtypeStruct((B, tn, D), F32),
        compiler_params=_cparams(("arbitrary", "arbitrary")),
        name="combine",
    )(pos, x1, mod, wts, ln_g, ln_b, y)


def _segments(hist, n_assign):
    counts = jnp.sum(hist[:, 0, N_GROUPS:N_ROUTE], axis=0).astype(jnp.int32)
    padded = (counts + EXP_BLK - 1) // EXP_BLK * EXP_BLK
    pad_end = jnp.cumsum(padded)
    pad_start = pad_end - padded
    p_rows = n_assign + N_EXPERTS * EXP_BLK
    nb = p_rows // EXP_BLK
    blk_first = jnp.arange(nb, dtype=jnp.int32) * EXP_BLK
    blk_e = jnp.minimum(jnp.sum((pad_end[None, :] <= blk_first[:, None]).astype(jnp.int32), axis=1),
                        N_EXPERTS - 1).astype(jnp.int32)
    nblk = (pad_end[-1] // EXP_BLK).astype(jnp.int32).reshape(1)
    return pad_start.astype(F32).reshape(N_EXPERTS, 1), pad_end.astype(jnp.int32), blk_e, nblk, p_rows


def _rope_tables(n_ctx, n_lat):
    nf = ATT_HEAD_DIM // 4
    inv = ROPE_BASE ** (-jnp.arange(nf, dtype=F32) / nf)
    rows = n_lat // GRID_W
    pos_r = jnp.repeat(jnp.arange(rows, dtype=F32), GRID_W)
    pos_c = jnp.tile(jnp.arange(GRID_W, dtype=F32), rows)
    ang_r = pos_r[:, None] * inv
    ang_c = pos_c[:, None] * inv
    cos_h = jnp.concatenate([jnp.cos(ang_r)] * 2 + [jnp.cos(ang_c)] * 2, axis=-1)
    sin_h = jnp.concatenate([-jnp.sin(ang_r), jnp.sin(ang_r),
                             -jnp.sin(ang_c), jnp.sin(ang_c)], axis=-1)
    reps = LANES // ATT_HEAD_DIM
    cos_l = jnp.tile(cos_h, (1, reps))
    sin_l = jnp.tile(sin_h, (1, reps))
    cos_t = jnp.concatenate([jnp.ones((n_ctx, LANES), F32), cos_l], axis=0)
    sin_t = jnp.concatenate([jnp.zeros((n_ctx, LANES), F32), sin_l], axis=0)
    return cos_t, sin_t


def _projection_weights(w_in_l):
    offs = np.cumsum((0, 3 * CONV_DIM, ATT_Q_DIM, ATT_KV_DIM, ATT_KV_DIM,
                      MLSTM_DIM, MLSTM_DIM, MLSTM_DIM, MLSTM_DIM, N_GATE)).tolist()
    a, q, k, v, mq, mk, mv, mo, g = [w_in_l[:, offs[n]:offs[n + 1]] for n in range(9)]
    hd = ATT_HEAD_DIM

    def swap(w):
        return jnp.concatenate([w[:, hd:], w[:, :hd]], axis=1)

    g_pad = jnp.pad(g, ((0, 0), (0, LANES - N_GATE)))
    w_tok = jnp.concatenate([a, q, k, swap(k), v, swap(v), mq, mk, g_pad], axis=1).astype(BF16)
    w_chan = jnp.concatenate([mv, mo, g], axis=1).T.astype(BF16)
    w_gate = w_in_l[:, offs[9]:].astype(BF16)
    return w_tok, w_chan, w_gate


def kernel(x, c, ctx, c_ctx, w_ada, b_ada, w_in, conv_w, attn_sink, mlstm_gate_b, mlstm_norm_w,
           w_proj_a, w_proj_b, w_proj_c, w_out, ln1_g, ln1_b, w_route_group, b_route_group,
           w_route_expert, b_route_expert, w_expert_in, w_expert_out, ln2_g, ln2_b):
    B, L, D = x.shape
    n_ctx = ctx.shape[1]
    depth = w_ada.shape[0]
    T = n_ctx + L
    alpha = (2 * depth) ** 0.25
    assert D == D_MODEL and n_ctx % TOK_TILE == 0 and L % TOK_TILE == 0 and L % GRID_W == 0
    assert MLSTM_CH == LANES and ATT_QB == LANES
    nct = n_ctx // TOK_TILE

    nrows = -(-(B + 1) // 8) * 8
    cond = jnp.concatenate([c, c_ctx[None, :], jnp.zeros((nrows - B - 1, D), F32)], axis=0)
    mod_all = _ada(cond, w_ada, b_ada).reshape(depth, nrows, 6, D)

    cos_t, sin_t = _rope_tables(n_ctx, L)
    xa = jnp.concatenate([ctx, x], axis=1)

    for i in range(depth):
        need_ctx = i < depth - 1
        mod = mod_all[i]
        w_tok, w_chan, w_gate = _projection_weights(w_in[i])

        a0u, q, kv, mqk, mvo_t, gcol, grow = _inproj(xa, mod, w_tok, w_chan, cos_t, sin_t, n_ctx)
        yb = _attention(q, kv, attn_sink[i], n_ctx)
        hf_t, hb_t = _mlstm(mqk, mvo_t, gcol, grow, mlstm_gate_b[i], n_ctx)

        off = 0 if need_ctx else nct
        w_route = jnp.pad(jnp.concatenate([w_route_group[i], w_route_expert[i]], axis=1).T,
                          ((0, LANES - N_ROUTE), (0, 0)))
        b_route = jnp.pad(jnp.concatenate([b_route_group[i], b_route_expert[i]]),
                          (0, LANES - N_ROUTE)).reshape(LANES, 1)
        x1, h2v, wts, route, hist = _merge(
            xa, mod, a0u, yb, hf_t, hb_t, mvo_t, conv_w[i], mlstm_norm_w[i].reshape(MLSTM_DIM, 1),
            w_gate, w_proj_a[i].astype(BF16), w_proj_b[i].astype(BF16), w_proj_c[i].astype(BF16),
            w_out[i].astype(BF16), ln1_g[i].reshape(1, D), ln1_b[i].reshape(1, D),
            w_route, b_route, n_ctx, off, alpha)

        tn = x1.shape[1]
        pstart, pend, blk_e, nblk, p_rows = _segments(hist, B * tn * TOP_K)
        pos = _positions(route, pstart, TOK_TILE).reshape(-1)
        xs = _dispatch(h2v, pos, pend, nblk, p_rows, TOK_TILE)
        y = _experts(xs, blk_e, nblk, w_expert_in, w_expert_out, i)
        xa = _combine(x1, mod, wts, pos, y,
                      ln2_g[i].reshape(1, D), ln2_b[i].reshape(1, D),
                      nct if need_ctx else 0, alpha)
    return xa
```

```python
import functools

import jax
import jax.numpy as jnp
import numpy as np
from jax import lax
from jax.experimental import pallas as pl
from jax.experimental.pallas import tpu as pltpu

D_MODEL = 1024
GRID_W = 64
CONV_DIM = 256
ATT_HEADS = 8
ATT_KV_HEADS = 2
ATT_HEAD_DIM = 64
ATT_WINDOW = 128
ROPE_BASE = 10000.0
MLSTM_HEADS = 4
MLSTM_HEAD_DIM = 64
MLSTM_DIM = MLSTM_HEADS * MLSTM_HEAD_DIM
N_GROUPS = 4
EXPERTS_PER_GROUP = 8
N_EXPERTS = N_GROUPS * EXPERTS_PER_GROUP
TOP_K = 2
D_EXPERT = D_MODEL // 2
LN_EPS = 1e-6
NEG_INF = -1e30

ATT_Q_DIM = ATT_HEADS * ATT_HEAD_DIM
ATT_KV_DIM = ATT_KV_HEADS * ATT_HEAD_DIM
N_GATE = 4 * MLSTM_HEADS
MIX_COLS = 3 * CONV_DIM + ATT_Q_DIM + 2 * ATT_KV_DIM + 4 * MLSTM_DIM
N_ROUTE = N_GROUPS + N_EXPERTS

LANES = 128
VMEM_LIMIT = 56 * 1024 * 1024
TOK_TILE = 256
SUB_TILE = 256
ATT_QB = 128
ATT_STEP = 256
MLSTM_CH = 128
MLSTM_STEP = 256
EXP_BLK = 256
HALO = 16
ROW_TILE = 8
GATHER_UNROLL = 8

F32 = jnp.float32
BF16 = jnp.bfloat16


def _cparams(sem):
    return pltpu.CompilerParams(dimension_semantics=sem, vmem_limit_bytes=VMEM_LIMIT)


def _ln(x, axis=-1):
    mu = jnp.mean(x, axis=axis, keepdims=True)
    xc = x - mu
    var = jnp.mean(xc * xc, axis=axis, keepdims=True)
    return xc * lax.rsqrt(var + LN_EPS)


def _sigmoid(x):
    return 0.5 * jnp.tanh(0.5 * x) + 0.5


def _split3(x):
    hi = x.astype(BF16)
    r1 = x - hi.astype(F32)
    mid = r1.astype(BF16)
    lo = (r1 - mid.astype(F32)).astype(BF16)
    return hi, mid, lo


def _dot(a, b):
    return jnp.dot(a, b, preferred_element_type=F32)


def _dot_nt(a, b):
    return lax.dot_general(a, b, (((1,), (1,)), ((), ())), preferred_element_type=F32)


def _dot3(x, rhs_b):
    return sum(_dot(part, rhs_b) for part in _split3(x))


def _dot3_l(lhs_b, x):
    return sum(_dot(lhs_b, part) for part in _split3(x))


def _ada_kernel(c_ref, w_ref, b_ref, o_ref):
    cv = c_ref[...]
    s = cv * _sigmoid(cv)
    w = w_ref[0]
    acc = jnp.zeros((s.shape[0], w.shape[1]), F32)
    w3 = _split3(w)
    for sp in _split3(s):
        for wp in w3:
            acc = acc + _dot(sp, wp)
    o_ref[0] = acc + b_ref[0]


def _ada(cond, w_ada, b_ada):
    depth, d, n6 = w_ada.shape
    rows = cond.shape[0]
    nt = n6 // d
    return pl.pallas_call(
        _ada_kernel,
        grid=(depth, nt),
        in_specs=[
            pl.BlockSpec((rows, d), lambda l, j: (0, 0)),
            pl.BlockSpec((1, d, d), lambda l, j: (l, 0, j)),
            pl.BlockSpec((1, 1, d), lambda l, j: (l, 0, j)),
        ],
        out_specs=pl.BlockSpec((1, rows, d), lambda l, j: (l, 0, j)),
        out_shape=jax.ShapeDtypeStruct((depth, rows, n6), F32),
        compiler_params=_cparams(("arbitrary", "arbitrary")),
        name="ada_mod",
    )(cond, w_ada, b_ada.reshape(depth, 1, n6))


def _rope(x, cos, sin_signed, lane):
    swapped = jnp.where(lane % 32 < 16,
                        pltpu.roll(x, LANES - 16, axis=1),
                        pltpu.roll(x, 16, axis=1))
    return x * cos + swapped * sin_signed


_OFF_A = 0
_OFF_Q = _OFF_A + 3 * CONV_DIM
_OFF_K = _OFF_Q + ATT_Q_DIM
_OFF_V = _OFF_K + 2 * ATT_KV_DIM
_OFF_MQK = _OFF_V + 2 * ATT_KV_DIM
_W_TOK_COLS = _OFF_MQK + 2 * MLSTM_DIM


def _stream_specs(streams, tm, nct, off=0):
    d = streams[0].shape[2]
    if len(streams) == 1:
        return [pl.BlockSpec((1, tm, d), lambda b, i, *_: (b, i + off, 0))]
    return [pl.BlockSpec((1, tm, d), lambda b, i, *_: (b, jnp.minimum(i + off, nct - 1), 0)),
            pl.BlockSpec((1, tm, d), lambda b, i, *_: (b, jnp.maximum(i + off - nct, 0), 0))]


def _stream_tile(refs, i, nct):
    if len(refs) == 1:
        return refs[0][0]
    return jnp.where(i < nct, refs[0][0], refs[1][0])


def _inproj_kernel(*refs, n_src, nct):
    (mod_ref, w_ref, wt_ref, cos_ref, sin_ref,
     a0u_ref, q_ref, kv_ref, mqk_ref, mvo_t_ref, gcol_ref, grow_ref) = refs[n_src:]
    x = _stream_tile(refs[:n_src], pl.program_id(1), nct)
    shift = mod_ref[0, 0:1, :]
    scale = mod_ref[0, 1:2, :]
    h = (_ln(x) * (1.0 + scale) + shift).astype(BF16)
    cos = cos_ref[...]
    sin = sin_ref[...]
    lane = lax.broadcasted_iota(jnp.int32, cos.shape, 1)

    def cols(lo, n):
        return _dot(h, w_ref[:, lo:lo + n])

    za = cols(_OFF_A, 3 * CONV_DIM)
    a0u_ref[0, :, 0:CONV_DIM] = za[:, 0:CONV_DIM].astype(BF16)
    a0u_ref[0, :, CONV_DIM:2 * CONV_DIM] = (
        za[:, CONV_DIM:2 * CONV_DIM] * za[:, 2 * CONV_DIM:3 * CONV_DIM]).astype(BF16)
    zq = cols(_OFF_Q, ATT_Q_DIM)
    qscale = ATT_HEAD_DIM ** -0.5
    for j in range(ATT_Q_DIM // LANES):
        piece = _rope(zq[:, j * LANES:(j + 1) * LANES], cos, sin, lane)
        q_ref[0, :, j * LANES:(j + 1) * LANES] = (piece * qscale).astype(BF16)
    zk = cols(_OFF_K, 2 * ATT_KV_DIM)
    for j in range(2):
        kv_ref[0, :, j * LANES:(j + 1) * LANES] = _rope(
            zk[:, j * LANES:(j + 1) * LANES], cos, sin, lane).astype(BF16)
    kv_ref[0, :, 2 * LANES:] = cols(_OFF_V, 2 * ATT_KV_DIM).astype(BF16)
    zm = cols(_OFF_MQK, 2 * MLSTM_DIM)
    mqk_ref[0, :, 0:MLSTM_DIM] = zm[:, 0:MLSTM_DIM].astype(BF16)
    mqk_ref[0, :, MLSTM_DIM:] = (zm[:, MLSTM_DIM:] * (MLSTM_HEAD_DIM ** -0.5)).astype(BF16)
    nchan = mvo_t_ref.shape[1]
    zt = _dot_nt(wt_ref[...], h)
    mvo_t_ref[0] = zt[0:nchan, :].astype(BF16)
    grow = zt[nchan:, :]
    grow_ref[0] = grow
    gcol_ref[0] = grow.T


def _inproj(streams, mod, w_tok, w_chan, cos_t, sin_t, n_ctx):
    B, _, D = streams[0].shape
    T = sum(s.shape[1] for s in streams) if len(streams) > 1 else streams[0].shape[1]
    tm = TOK_TILE
    nct = n_ctx // tm
    nchan = w_chan.shape[0] - N_GATE

    def tok(width):
        return pl.BlockSpec((1, tm, width), lambda b, i: (b, i, 0))

    def chan(rows):
        return pl.BlockSpec((1, rows, tm), lambda b, i: (b, 0, i))

    outs = [(2 * CONV_DIM, BF16), (ATT_Q_DIM, BF16), (4 * ATT_KV_DIM, BF16), (2 * MLSTM_DIM, BF16)]
    return pl.pallas_call(
        functools.partial(_inproj_kernel, n_src=len(streams), nct=nct),
        grid=(B, T // tm),
        in_specs=_stream_specs(streams, tm, nct) + [
            pl.BlockSpec((1, 6, D), lambda b, i: (jnp.where(i < nct, B, b), 0, 0)),
            pl.BlockSpec(w_tok.shape, lambda b, i: (0, 0)),
            pl.BlockSpec(w_chan.shape, lambda b, i: (0, 0)),
            pl.BlockSpec((tm, LANES), lambda b, i: (i, 0)),
            pl.BlockSpec((tm, LANES), lambda b, i: (i, 0)),
        ],
        out_specs=[tok(w) for w, _ in outs] + [
            chan(nchan), tok(N_GATE), chan(N_GATE)],
        out_shape=[jax.ShapeDtypeStruct((B, T, w), dt) for w, dt in outs] + [
            jax.ShapeDtypeStruct((B, nchan, T), BF16),
            jax.ShapeDtypeStruct((B, T, N_GATE), F32), jax.ShapeDtypeStruct((B, N_GATE, T), F32)],
        compiler_params=_cparams(("parallel", "arbitrary")),
        name="in_proj",
    )(*streams, mod, w_tok, w_chan, cos_t, sin_t)


def _attn_kernel(sink_ref, q_ref, kvc_ref, kvp_ref, kvm_ref, kvn_ref, o_ref, *, n_ctx_blk, n_blk):
    n_ctx = kvc_ref.shape[1]
    qb = kvp_ref.shape[1]
    nsub = q_ref.shape[1] // qb
    nk = n_ctx + 3 * qb
    half = LANES // 2

    lane_row = lax.broadcasted_iota(jnp.int32, (1, LANES), 1)
    keep = [(lane_row < half).astype(F32).astype(BF16), (lane_row >= half).astype(F32).astype(BF16)]
    pad_v = jnp.zeros((qb, LANES), BF16)
    qi = lax.broadcasted_iota(jnp.int32, (qb, qb), 0)
    ki = lax.broadcasted_iota(jnp.int32, (qb, qb), 1)
    lane_q = lax.broadcasted_iota(jnp.int32, (qb, LANES), 1)
    neg = jnp.full((qb, LANES), NEG_INF, F32)
    group = ATT_HEADS // ATT_KV_HEADS
    around = [kvp_ref[0]] + [kvm_ref[0, sb * qb:(sb + 1) * qb, :] for sb in range(nsub)] + [kvn_ref[0]]

    for sb in range(nsub):
        g = pl.program_id(1) * nsub + sb
        is_lat = g >= n_ctx_blk
        has_prev = g >= n_ctx_blk + 1
        has_next = g < n_blk - 1
        kv_all = jnp.concatenate([kvc_ref[0]] + around[sb:sb + 3], axis=0)

        def k_ext(tile, par):
            return kv_all[:, tile * LANES:(tile + 1) * LANES] * keep[par]

        def v_ext(tile, par):
            v = jnp.concatenate([kv_all[:, (2 + tile) * LANES:(3 + tile) * LANES], pad_v], axis=0)
            return jnp.concatenate([v * keep[par], jnp.broadcast_to(keep[par], v.shape)], axis=1)

        ok_prev = jnp.logical_and(ki >= qi, has_prev)
        ok_next = jnp.logical_and(ki <= qi, jnp.logical_and(has_next, is_lat))
        q = q_ref[0, sb * qb:(sb + 1) * qb, :]
        outs = []
        for pair in range(ATT_HEADS // 2):
            kvh = (2 * pair) // group
            q_pair = q[:, pair * LANES:(pair + 1) * LANES]
            acc = None
            for par in range(2):
                h = 2 * pair + par
                tile = 0 if kvh == par else 1
                s = _dot_nt(q_pair, k_ext(tile, par))
                s_ext = jnp.concatenate([
                    s[:, 0:n_ctx],
                    jnp.where(ok_prev, s[:, n_ctx:n_ctx + qb], neg),
                    jnp.where(is_lat, s[:, n_ctx + qb:n_ctx + 2 * qb], neg),
                    jnp.where(ok_next, s[:, n_ctx + 2 * qb:nk], neg),
                    jnp.where(lane_q == 0, sink_ref[h], neg)], axis=1)
                m = jnp.max(s_ext, axis=-1, keepdims=True)
                p = jnp.exp(s_ext - m).astype(BF16)
                part = _dot(p, v_ext(tile, par))
                acc = part if acc is None else acc + part
            outs.append(acc[:, 0:LANES] / acc[:, LANES:])
        o_ref[0, sb * qb:(sb + 1) * qb, :] = jnp.concatenate(outs, axis=-1).astype(o_ref.dtype)


def _attention(q, kv, sink, n_ctx):
    B, T, _ = q.shape
    qb = ATT_QB
    nsub = ATT_STEP // qb
    n_blk = T // qb
    nq = T // ATT_STEP
    n_ctx_blk = n_ctx // qb
    kvw = kv.shape[2]
    grid_spec = pltpu.PrefetchScalarGridSpec(
        num_scalar_prefetch=1,
        grid=(B, nq),
        in_specs=[
            pl.BlockSpec((1, ATT_STEP, ATT_Q_DIM), lambda b, j, sk: (b, j, 0)),
            pl.BlockSpec((1, n_ctx, kvw), lambda b, j, sk: (b, 0, 0)),
            pl.BlockSpec((1, qb, kvw), lambda b, j, sk: (b, jnp.maximum(j * nsub - 1, 0), 0)),
            pl.BlockSpec((1, ATT_STEP, kvw), lambda b, j, sk: (b, j, 0)),
            pl.BlockSpec((1, qb, kvw), lambda b, j, sk: (b, jnp.minimum((j + 1) * nsub, n_blk - 1), 0)),
        ],
        out_specs=pl.BlockSpec((1, ATT_STEP, ATT_Q_DIM), lambda b, j, sk: (b, j, 0)),
    )
    return pl.pallas_call(
        functools.partial(_attn_kernel, n_ctx_blk=n_ctx_blk, n_blk=n_blk),
        grid_spec=grid_spec,
        out_shape=jax.ShapeDtypeStruct((B, T, ATT_Q_DIM), BF16),
        compiler_params=_cparams(("parallel", "arbitrary")),
        name="window_attn",
    )(sink, q, kv, kv, kv, kv)


def _log_sigmoid(x):
    return jnp.minimum(x, 0.0) - jnp.log1p(jnp.exp(-jnp.abs(x)))


def _rows_to_lanes(a, base):
    return jnp.concatenate([a[base + h:base + h + 1, :] for h in range(MLSTM_HEADS)], axis=1)


def _mlstm_dir(d, qk, v_t, gcol, grow, ct_bd, n_bd, m_prev):
    ch = qk.shape[0]
    nh = MLSTM_HEADS
    dh = MLSTM_HEAD_DIM
    wide = nh * ch
    fwd = d == 0

    r_i = lax.broadcasted_iota(jnp.int32, (ch, ch), 0)
    c_i = lax.broadcasted_iota(jnp.int32, (ch, ch), 1)
    seen_t = (r_i <= c_i) if fwd else (r_i >= c_i)
    seen_tt = (c_i <= r_i) if fwd else (c_i >= r_i)
    r_w = lax.broadcasted_iota(jnp.int32, (ch, wide), 0)
    s_w = lax.broadcasted_iota(jnp.int32, (ch, wide), 1) % ch
    seen_w = (r_w <= s_w) if fwd else (r_w >= s_w)

    base_i = 2 * d * nh
    base_f = base_i + nh

    lf_col = _log_sigmoid(gcol)
    cum_col = _dot3_l(seen_tt.astype(BF16), lf_col)
    lane16 = lax.broadcasted_iota(jnp.int32, gcol.shape, 1)
    z = jnp.where(jnp.logical_and(lane16 >= base_i, lane16 < base_f), gcol, -cum_col)
    ch16 = lax.broadcasted_iota(jnp.int32, (N_GATE, wide), 0)
    hd16 = lax.broadcasted_iota(jnp.int32, (N_GATE, wide), 1) // ch
    sel = jnp.logical_or(ch16 == base_i + hd16, ch16 == base_f + hd16).astype(BF16)
    x_t = _dot3(z, sel)

    lf_row = _log_sigmoid(grow)
    rhs2 = jnp.concatenate([seen_t.astype(BF16), jnp.ones((ch, ch), BF16)], axis=1)
    rows = _dot3(lf_row, rhs2)
    b_all = _rows_to_lanes(rows[:, 0:ch], base_f)
    g_all = _rows_to_lanes(rows[:, ch:], base_f)
    li_all = _rows_to_lanes(grow, base_i)

    dmat = jnp.where(seen_w, x_t + b_all, -jnp.inf)
    a = b_all + m_prev
    m = jnp.maximum(a, jnp.max(dmat, axis=0, keepdims=True))
    w_intra = jnp.exp(dmat - m)
    e_inter = jnp.exp(a - m)

    q = qk[:, 0:MLSTM_DIM]
    k = qk[:, MLSTM_DIM:]
    lb = lax.broadcasted_iota(jnp.int32, (1, MLSTM_DIM), 1) // dh
    q_bd = jnp.concatenate([q * (lb == h).astype(F32).astype(BF16) for h in range(nh)], axis=0)
    s_t = _dot_nt(k, q_bd) * w_intra
    nq = _rows_to_lanes(_dot_nt(n_bd.astype(BF16), q), 0)
    den = jnp.sum(s_t, axis=0, keepdims=True) + e_inter * nq
    inv = 1.0 / jnp.maximum(jnp.abs(den), jnp.exp(-m))
    inter_t = _dot_nt(ct_bd.astype(BF16), q)
    s_b = s_t.astype(BF16)
    outs = []
    for h in range(nh):
        seg = slice(h * ch, (h + 1) * ch)
        blk = slice(h * dh, (h + 1) * dh)
        num = _dot(v_t[blk, :], s_b[:, seg]) + e_inter[:, seg] * inter_t[blk, :]
        outs.append(num * inv[:, seg])
    h_t = jnp.concatenate(outs, axis=0)

    m_loc = g_all + jnp.max(x_t, axis=0, keepdims=True)
    m_new = jnp.maximum(g_all + m_prev, m_loc)
    sa = jnp.exp(g_all + m_prev - m_new)
    sb = jnp.exp(m_loc - m_new)
    e_loc = jnp.exp(g_all - b_all + li_all - m_loc)
    v_e = jnp.concatenate(
        [v_t[h * dh:(h + 1) * dh, :].astype(F32) * e_loc[:, h * ch:(h + 1) * ch] for h in range(nh)],
        axis=0).astype(BF16)
    ct_loc = _dot(v_e, k)
    reps = MLSTM_DIM // ch

    def per_head_rows(row, nrows):
        return jnp.concatenate(
            [jnp.broadcast_to(jnp.concatenate([row[:, h * ch:(h + 1) * ch]] * reps, axis=1),
                              (nrows, MLSTM_DIM)) for h in range(nh)], axis=0)

    eb = lax.broadcasted_iota(jnp.int32, (MLSTM_DIM, MLSTM_DIM), 0) // dh
    db = lax.broadcasted_iota(jnp.int32, (MLSTM_DIM, MLSTM_DIM), 1) // dh
    ct_new = jnp.where(eb == db, per_head_rows(sa, dh) * ct_bd + per_head_rows(sb, dh) * ct_loc, 0.0)
    nrow = n_bd.shape[0]
    e_rows = jnp.concatenate([e_loc[:, h * ch:(h + 1) * ch] for h in range(nh)]
                             + [jnp.zeros((nrow - nh, ch), F32)], axis=0).astype(BF16)
    n_loc = _dot(e_rows, k)
    pad = jnp.zeros((nrow - nh, MLSTM_DIM), F32)
    sa8 = jnp.concatenate([per_head_rows(sa, 1), pad], axis=0)
    sb8 = jnp.concatenate([per_head_rows(sb, 1), pad], axis=0)
    hb8 = lax.broadcasted_iota(jnp.int32, (nrow, MLSTM_DIM), 0)
    db8 = lax.broadcasted_iota(jnp.int32, (nrow, MLSTM_DIM), 1) // dh
    n_new = jnp.where(hb8 == db8, sa8 * n_bd + sb8 * n_loc, 0.0)
    return h_t, ct_new, n_new, m_new


def _mlstm_kernel(qk_f_ref, qk_b_ref, vt_f_ref, vt_b_ref, gc_f_ref, gc_b_ref, gr_f_ref, gr_b_ref,
                  bcol_ref, brow_ref, hf_ref, hb_ref, c_scr, n_scr, m_scr):
    j = pl.program_id(1)

    @pl.when(j == 0)
    def _():
        c_scr[...] = jnp.zeros_like(c_scr)
        n_scr[...] = jnp.zeros_like(n_scr)
        m_scr[...] = jnp.zeros_like(m_scr)

    states = [(c_scr[d], n_scr[d], m_scr[d]) for d in range(2)]
    dirs = ((qk_f_ref, vt_f_ref, gc_f_ref, gr_f_ref, hf_ref),
            (qk_b_ref, vt_b_ref, gc_b_ref, gr_b_ref, hb_ref))
    ch = MLSTM_CH
    nsub = qk_f_ref.shape[1] // ch
    for d, (qk_ref, vt_ref, gc_ref, gr_ref, out_ref) in enumerate(dirs):
        st = states[d]
        for sc in (range(nsub) if d == 0 else reversed(range(nsub))):
            ts = slice(sc * ch, (sc + 1) * ch)
            h_t, *st = _mlstm_dir(
                d, qk_ref[0, ts, :], vt_ref[0, :, ts],
                gc_ref[0, ts, :] + bcol_ref[...], gr_ref[0, :, ts] + brow_ref[...], *st)
            out_ref[0, :, ts] = h_t
        states[d] = st
    for d in range(2):
        c_scr[d] = states[d][0]
        n_scr[d] = states[d][1]
        m_scr[d] = states[d][2]


def _mlstm(mqk, mvo_t, gcol, grow, gate_b, n_ctx):
    B, T, _ = mqk.shape
    ch = MLSTM_STEP
    nc = T // ch
    ncc = n_ctx // ch

    def rev(j):
        return jnp.where(j < ncc, ncc - 1 - j, nc - 1 - (j - ncc))

    def tok(width, order):
        return pl.BlockSpec((1, ch, width), lambda b, j: (b, order(j), 0))

    def chan(rows, order):
        return pl.BlockSpec((1, rows, ch), lambda b, j: (b, 0, order(j)))

    ident = lambda j: j
    return pl.pallas_call(
        _mlstm_kernel,
        grid=(B, nc),
        in_specs=[
            tok(2 * MLSTM_DIM, ident), tok(2 * MLSTM_DIM, rev),
            chan(MLSTM_DIM, ident), chan(MLSTM_DIM, rev),
            tok(N_GATE, ident), tok(N_GATE, rev),
            chan(N_GATE, ident), chan(N_GATE, rev),
            pl.BlockSpec((1, N_GATE), lambda b, j: (0, 0)),
            pl.BlockSpec((N_GATE, 1), lambda b, j: (0, 0)),
        ],
        out_specs=[chan(MLSTM_DIM, ident), chan(MLSTM_DIM, rev)],
        out_shape=[jax.ShapeDtypeStruct((B, MLSTM_DIM, T), F32)] * 2,
        scratch_shapes=[
            pltpu.VMEM((2, MLSTM_DIM, MLSTM_DIM), F32),
            pltpu.VMEM((2, 8, MLSTM_DIM), F32),
            pltpu.VMEM((2, 1, MLSTM_HEADS * MLSTM_CH), F32),
        ],
        compiler_params=_cparams(("parallel", "arbitrary")),
        name="mlstm",
    )(mqk, mqk, mvo_t, mvo_t, gcol, gcol, grow, grow,
      gate_b.reshape(1, N_GATE), gate_b.reshape(N_GATE, 1))


def _merge_kernel(*refs, n_src, off, n_ctx_tiles, alpha):
    (mod_ref, a0u_ref, up_ref, un_ref, yb_ref, hf_ref, hb_ref, mvo_t_ref,
     convw_ref, normw_ref, wg_ref, wpa_ref, wpb_ref, wpc_ref, wo_ref,
     ln_g_ref, ln_b_ref, wr_ref, br_ref, upper_ref,
     x1_ref, h2_ref, wt_ref, assign_ref, counts_ref, carry) = refs[n_src:]
    i = pl.program_id(1) + off
    nt = pl.num_programs(1) + off
    x_tile = _stream_tile(refs[:n_src], i, n_ctx_tiles)
    tm = x_tile.shape[0]
    d = x_tile.shape[1]

    a0 = a0u_ref[0, :, 0:CONV_DIM].astype(F32)
    u = a0u_ref[0, :, CONV_DIM:].astype(F32)
    prev_ok = jnp.logical_and(i != 0, i != n_ctx_tiles)
    next_ok = jnp.logical_and(i != n_ctx_tiles - 1, i != nt - 1)
    u_prev = jnp.where(prev_ok, up_ref[0, HALO - 1:HALO, CONV_DIM:].astype(F32), 0.0)
    u_next = jnp.where(next_ok, un_ref[0, 0:1, CONV_DIM:].astype(F32), 0.0)
    row = lax.broadcasted_iota(jnp.int32, u.shape, 0)
    u_dn = jnp.where(row == 0, u_prev, pltpu.roll(u, 1, axis=0))
    u_up = jnp.where(row == tm - 1, u_next, pltpu.roll(u, tm - 1, axis=0))
    cw = convw_ref[...]
    ya_all = (a0 * (u_dn * cw[0:1, :] + u * cw[1:2, :] + u_up * cw[2:3, :])).astype(BF16)

    w_hi = wr_ref[...].astype(BF16)
    sub = lax.broadcasted_iota(jnp.int32, (w_hi.shape[0], SUB_TILE), 0)
    big = jnp.int32(2 * LANES)
    is_g = sub < N_GROUPS
    assert tm == SUB_TILE

    @pl.when(jnp.logical_and(pl.program_id(0) == 0, pl.program_id(1) == 0))
    def _():
        carry[...] = jnp.zeros_like(carry)

    for part in range(tm // SUB_TILE):
        rs = slice(part * SUB_TILE, (part + 1) * SUB_TILE)
        x = x_tile[rs, :]
        h = (_ln(x) * (1.0 + mod_ref[0, 1:2, :]) + mod_ref[0, 0:1, :]).astype(BF16)

        hm = hf_ref[0, :, rs] + hb_ref[0, :, rs]
        normed = jnp.concatenate(
            [_ln(hm[hd * MLSTM_HEAD_DIM:(hd + 1) * MLSTM_HEAD_DIM, :], axis=0)
             for hd in range(MLSTM_HEADS)], axis=0)
        yc_t = _sigmoid(mvo_t_ref[0, :, rs].astype(F32)) * (normed * normw_ref[...])
        yc = yc_t.T

        pa = _dot(ya_all[rs, :], wpa_ref[...])
        pb = _dot(yb_ref[0, rs, :], wpb_ref[...])
        pc = _dot(yc.astype(BF16), wpc_ref[...])
        merged = (_sigmoid(_dot(h, wg_ref[:, 0:d])) * pa
                  + _sigmoid(_dot(h, wg_ref[:, d:2 * d])) * pb
                  + _sigmoid(_dot(h, wg_ref[:, 2 * d:3 * d])) * pc)
        yl = _dot(merged.astype(BF16), wo_ref[...])

        x1 = _ln(alpha * x + mod_ref[0, 2:3, :] * yl) * ln_g_ref[...] + ln_b_ref[...]
        x1_ref[0, rs, :] = x1
        h2 = _ln(x1) * (1.0 + mod_ref[0, 4:5, :]) + mod_ref[0, 3:4, :]
        for s in range(d // LANES):
            h2_ref[pl.ds(part * SUB_TILE * ROW_TILE + s, SUB_TILE, stride=ROW_TILE), :] = (
                h2[:, s * LANES:(s + 1) * LANES])

        lg = _dot_nt(w_hi, h2.astype(BF16)) + br_ref[...]
        gl = jnp.where(is_g, lg, -jnp.inf)
        g_max = jnp.max(gl, axis=0, keepdims=True)
        g_sel = jnp.min(jnp.where(gl == g_max, sub, big), axis=0, keepdims=True)
        g_p = 1.0 / jnp.sum(jnp.where(is_g, jnp.exp(gl - g_max), 0.0), axis=0, keepdims=True)
        lo = N_GROUPS + EXPERTS_PER_GROUP * g_sel
        el = jnp.where(jnp.logical_and(sub >= lo, sub < lo + EXPERTS_PER_GROUP), lg, -jnp.inf)
        e1 = jnp.max(el, axis=0, keepdims=True)
        i1 = jnp.min(jnp.where(el == e1, sub, big), axis=0, keepdims=True)
        el2 = jnp.where(sub == i1, -jnp.inf, el)
        e2 = jnp.max(el2, axis=0, keepdims=True)
        i2 = jnp.min(jnp.where(el2 == e2, sub, big), axis=0, keepdims=True)
        t = jnp.exp(e2 - e1)
        w1 = g_p / (1.0 + t)
        w2 = w1 * t
        eid1 = (i1 - N_GROUPS).astype(F32)
        eid2 = (i2 - N_GROUPS).astype(F32)
        rows = jnp.concatenate([eid1, eid2, w1, w2, jnp.zeros((4, SUB_TILE), F32)], axis=0)
        wt_ref[0, rs, :] = rows.T
        eid = jnp.concatenate([eid1, eid2], axis=1)
        sub_e = lax.broadcasted_iota(jnp.int32, (N_EXPERTS, eid.shape[1]), 0).astype(F32)
        onehot = jnp.where(sub_e == eid, 1.0, 0.0)
        earlier = _dot(onehot.astype(BF16), upper_ref[...])
        seen = carry[...]
        rank = jnp.sum(onehot * (earlier + seen), axis=0, keepdims=True)
        assign_ref[0] = jnp.concatenate([eid, rank], axis=0).astype(jnp.int32)
        seen = seen + jnp.sum(onehot, axis=1, keepdims=True)
        carry[...] = seen
        counts_ref[...] = seen


def _merge(streams, mod, a0u, yb, hf_t, hb_t, mvo_t, conv_w, norm_w, w_gate, w_pa, w_pb, w_pc, w_o,
           ln_g, ln_b, w_route, b_route, n_ctx, off, alpha):
    B, _, D = streams[0].shape
    T = a0u.shape[1]
    tm = TOK_TILE
    nct = n_ctx // tm
    nt = T // tm - off
    tn = nt * tm
    hpt = tm // HALO
    nhalo = T // HALO

    def tok(width):
        return pl.BlockSpec((1, tm, width), lambda b, i: (b, i + off, 0))

    def chan(rows):
        return pl.BlockSpec((1, rows, tm), lambda b, i: (b, 0, i + off))

    def full(a):
        return pl.BlockSpec(a.shape, lambda b, i: (0,) * a.ndim)

    def otok(width):
        return pl.BlockSpec((1, tm, width), lambda b, i: (b, i, 0))

    a_i = lax.broadcasted_iota(jnp.int32, (TOP_K * tm, TOP_K * tm), 0)
    b_i = lax.broadcasted_iota(jnp.int32, (TOP_K * tm, TOP_K * tm), 1)
    upper = (a_i < b_i).astype(BF16)
    consts = [conv_w, norm_w, w_gate, w_pa, w_pb, w_pc, w_o, ln_g, ln_b, w_route, b_route, upper]
    return pl.pallas_call(
        functools.partial(_merge_kernel, n_src=len(streams), off=off, n_ctx_tiles=nct, alpha=alpha),
        grid=(B, nt),
        in_specs=_stream_specs(streams, tm, nct, off) + [
            pl.BlockSpec((1, 6, D), lambda b, i: (jnp.where(i + off < nct, B, b), 0, 0)),
            tok(2 * CONV_DIM),
            pl.BlockSpec((1, HALO, 2 * CONV_DIM),
                         lambda b, i: (b, jnp.maximum((i + off) * hpt - 1, 0), 0)),
            pl.BlockSpec((1, HALO, 2 * CONV_DIM),
                         lambda b, i: (b, jnp.minimum((i + off + 1) * hpt, nhalo - 1), 0)),
            tok(ATT_Q_DIM), chan(MLSTM_DIM), chan(MLSTM_DIM),
            pl.BlockSpec((1, MLSTM_DIM, tm), lambda b, i: (b, 1, i + off)),

        ] + [full(a) for a in consts],
        out_specs=[otok(D),
                   pl.BlockSpec((tm * ROW_TILE, LANES), lambda b, i: (b * nt + i, 0)),
                   otok(ROW_TILE),
                   pl.BlockSpec((1, 2, TOP_K * tm), lambda b, i: (b * nt + i, 0, 0)),
                   pl.BlockSpec((N_EXPERTS, 1), lambda b, i: (0, 0))],
        out_shape=[jax.ShapeDtypeStruct((B, tn, D), F32),
                   jax.ShapeDtypeStruct((B * tn * ROW_TILE, LANES), F32),
                   jax.ShapeDtypeStruct((B, tn, ROW_TILE), F32),
                   jax.ShapeDtypeStruct((B * nt, 2, TOP_K * tm), jnp.int32),
                   jax.ShapeDtypeStruct((N_EXPERTS, 1), F32)],
        scratch_shapes=[pltpu.VMEM((N_EXPERTS, 1), F32)],
        compiler_params=_cparams(("arbitrary", "arbitrary")),
        name="merge",
    )(*streams, mod, a0u, a0u, a0u, yb, hf_t, hb_t, mvo_t, *consts)


def _dispatch_kernel(pos_ref, pend_ref, nblk_ref, h_ref, xs_hbm, zbuf, sem, zsem):
    s = pl.program_id(0)
    tm = h_ref.shape[0] // ROW_TILE
    zrows = zbuf.shape[0]
    last_blk = xs_hbm.shape[0] // zrows - 1

    @pl.when(s == 0)
    def _():
        zbuf[...] = jnp.zeros_like(zbuf)

        def zero_copy(start):
            return pltpu.make_async_copy(
                zbuf, xs_hbm.at[pl.ds(pl.multiple_of(start, ROW_TILE), zrows), :], zsem)

        jobs = []
        for e in range(N_EXPERTS):
            before = pend_ref[e - 1] if e > 0 else 0
            jobs.append((pend_ref[e] * ROW_TILE - zrows, pend_ref[e] > before))
        for e in range(N_EXPERTS):
            jobs.append(((nblk_ref[0] + e) * zrows, nblk_ref[0] + e <= last_blk))
        for st, needed in jobs:
            @pl.when(needed)
            def _(st=st):
                zero_copy(st).start()
        for st, needed in jobs:
            @pl.when(needed)
            def _(st=st):
                zero_copy(st).wait()

    def row_copy(r, k):
        src = pl.multiple_of(r * ROW_TILE, ROW_TILE)
        dst = pl.multiple_of(pos_ref[(s * TOP_K + k) * tm + r] * ROW_TILE, ROW_TILE)
        return pltpu.make_async_copy(h_ref.at[pl.ds(src, ROW_TILE), :],
                                     xs_hbm.at[pl.ds(dst, ROW_TILE), :], sem)

    def body(r, carry):
        for k in range(TOP_K):
            row_copy(r, k).start()
        return carry
    lax.fori_loop(0, tm, body, 0, unroll=GATHER_UNROLL)
    for k in range(TOP_K):
        pltpu.make_async_copy(h_ref, xs_hbm.at[pl.ds(0, tm * ROW_TILE), :], sem).wait()


def _dispatch(h2v, pos, pend, nblk, p_rows, tm):
    rows = h2v.shape[0]
    nsteps = rows // (tm * ROW_TILE)
    grid_spec = pltpu.PrefetchScalarGridSpec(
        num_scalar_prefetch=3,
        grid=(nsteps,),
        in_specs=[pl.BlockSpec((tm * ROW_TILE, LANES), lambda s, pos, pend, nbk: (s, 0))],
        out_specs=pl.BlockSpec(memory_space=pl.ANY),
        scratch_shapes=[
            pltpu.VMEM((EXP_BLK * ROW_TILE, LANES), F32),
            pltpu.SemaphoreType.DMA,
            pltpu.SemaphoreType.DMA,
        ],
    )
    return pl.pallas_call(
        _dispatch_kernel,
        grid_spec=grid_spec,
        out_shape=jax.ShapeDtypeStruct((p_rows * ROW_TILE, LANES), F32),
        compiler_params=_cparams(("arbitrary",)),
        name="dispatch",
    )(pos, pend, nblk, h2v)


def _expert_kernel(blke_ref, nblk_ref, xs_ref, wi_ref, wo_ref, y_ref, wi_bf, wo_bf):
    i = pl.program_id(0)
    nb = nblk_ref[0]
    blk = xs_ref.shape[0] // ROW_TILE
    nsl = wi_ref.shape[2] // LANES

    @pl.when(i < nb)
    def _():
        e_now = blke_ref[i]
        e_before = blke_ref[jnp.maximum(i - 1, 0)]

        @pl.when(jnp.logical_or(i == 0, e_now != e_before))
        def _():
            wi_bf[...] = wi_ref[0, 0].astype(BF16)
            wo_bf[...] = wo_ref[0, 0].astype(BF16)

        xin = jnp.concatenate(
            [xs_ref[pl.ds(s, blk, stride=ROW_TILE), :] for s in range(nsl)], axis=1).astype(BF16)
        mid = _dot(xin, wi_bf[...])
        gt = mid[:, 0:D_EXPERT]
        up = mid[:, D_EXPERT:]
        act = (gt * _sigmoid(gt)) * up
        y = _dot(act.astype(BF16), wo_bf[...])
        for s in range(nsl):
            y_ref[pl.ds(s, blk, stride=ROW_TILE), :] = y[:, s * LANES:(s + 1) * LANES]

    @pl.when(i >= nb)
    def _():
        y_ref[...] = jnp.zeros_like(y_ref)


def _experts(xs, blk_e, nblk, w_ei, w_eo, layer):
    d = w_ei.shape[2]
    blk = EXP_BLK
    nb = xs.shape[0] // (blk * ROW_TILE)
    grid_spec = pltpu.PrefetchScalarGridSpec(
        num_scalar_prefetch=2,
        grid=(nb,),
        in_specs=[
            pl.BlockSpec((blk * ROW_TILE, LANES),
                         lambda i, be, nbk: (jnp.minimum(i, jnp.maximum(nbk[0] - 1, 0)), 0)),
            pl.BlockSpec((1, 1, d, 2 * D_EXPERT), lambda i, be, nbk: (layer, be[i], 0, 0)),
            pl.BlockSpec((1, 1, D_EXPERT, d), lambda i, be, nbk: (layer, be[i], 0, 0)),
        ],
        out_specs=pl.BlockSpec((blk * ROW_TILE, LANES), lambda i, be, nbk: (i, 0)),
        scratch_shapes=[
            pltpu.VMEM((d, 2 * D_EXPERT), BF16),
            pltpu.VMEM((D_EXPERT, d), BF16),
        ],
    )
    return pl.pallas_call(
        _expert_kernel,
        grid_spec=grid_spec,
        out_shape=jax.ShapeDtypeStruct(xs.shape, F32),
        compiler_params=_cparams(("arbitrary",)),
        name="experts",
    )(blk_e, nblk, xs, w_ei, w_eo)


def _combine_kernel(pos_ref, x_ref, mod_ref, wt_ref, ln_g_ref, ln_b_ref, y_hbm, o_ref,
                    ybuf, sem, *, alpha):
    b = pl.program_id(0)
    i = pl.program_id(1)
    nt = pl.num_programs(1)
    tm = x_ref.shape[1]
    step = b * nt + i
    nsteps = pl.num_programs(0) * nt

    def start_gather(s, slot):
        def body(r, carry):
            dst = pl.multiple_of(r * ROW_TILE, ROW_TILE)
            for k in range(TOP_K):
                src = pl.multiple_of(pos_ref[(s * TOP_K + k) * tm + r] * ROW_TILE, ROW_TILE)
                pltpu.make_async_copy(y_hbm.at[pl.ds(src, ROW_TILE), :],
                                      ybuf.at[slot, k, pl.ds(dst, ROW_TILE), :], sem.at[slot]).start()
            return carry
        lax.fori_loop(0, tm, body, 0, unroll=GATHER_UNROLL)

    @pl.when(step == 0)
    def _():
        start_gather(0, 0)

    @pl.when(step + 1 < nsteps)
    def _():
        start_gather(step + 1, (step + 1) % 2)

    slot = step % 2
    for k in range(TOP_K):
        pltpu.make_async_copy(y_hbm.at[pl.ds(0, tm * ROW_TILE), :], ybuf.at[slot, k],
                              sem.at[slot]).wait()
    wt = wt_ref[0]
    w0 = wt[:, 2:3]
    w1 = wt[:, 3:4]
    f = jnp.concatenate(
        [w0 * ybuf[slot, 0, pl.ds(j, tm, stride=ROW_TILE), :]
         + w1 * ybuf[slot, 1, pl.ds(j, tm, stride=ROW_TILE), :]
         for j in range(x_ref.shape[2] // LANES)], axis=1)
    x = x_ref[0]
    o_ref[0] = _ln(alpha * x + mod_ref[0, 5:6, :] * f) * ln_g_ref[...] + ln_b_ref[...]


def _combine(x1, mod, wts, pos, y, ln_g, ln_b, n_ctx_tiles, alpha):
    B, tn, D = x1.shape
    tm = TOK_TILE
    nt = tn // tm
    grid_spec = pltpu.PrefetchScalarGridSpec(
        num_scalar_prefetch=1,
        grid=(B, nt),
        in_specs=[
            pl.BlockSpec((1, tm, D), lambda b, i, pos: (b, i, 0)),
            pl.BlockSpec((1, 6, D), lambda b, i, pos: (jnp.where(i < n_ctx_tiles, B, b), 0, 0)),
            pl.BlockSpec((1, tm, ROW_TILE), lambda b, i, pos: (b, i, 0)),
            pl.BlockSpec((1, D), lambda b, i, pos: (0, 0)),
            pl.BlockSpec((1, D), lambda b, i, pos: (0, 0)),
            pl.BlockSpec(memory_space=pl.ANY),
        ],
        out_specs=pl.BlockSpec((1, tm, D), lambda b, i, pos: (b, i, 0)),
        scratch_shapes=[
            pltpu.VMEM((2, TOP_K, tm * ROW_TILE, LANES), F32),
            pltpu.SemaphoreType.DMA((2,)),
        ],
    )
    return pl.pallas_call(
        functools.partial(_combine_kernel, alpha=alpha),
        grid_spec=grid_spec,
        out_shape=jax.ShapeDtypeStruct((B, tn, D), F32),
        compiler_params=_cparams(("arbitrary", "arbitrary")),
        name="combine",
    )(pos, x1, mod, wts, ln_g, ln_b, y)


def _segments(counts, assign):
    counts = counts.reshape(N_EXPERTS).astype(jnp.int32)
    n_assign = assign.shape[0] * assign.shape[2]
    padded = (counts + EXP_BLK - 1) // EXP_BLK * EXP_BLK
    pad_end = jnp.cumsum(padded)
    pad_start = pad_end - padded
    p_rows = n_assign + N_EXPERTS * EXP_BLK
    nb = p_rows // EXP_BLK
    blk_first = jnp.arange(nb, dtype=jnp.int32) * EXP_BLK
    blk_e = jnp.minimum(jnp.sum((pad_end[None, :] <= blk_first[:, None]).astype(jnp.int32), axis=1),
                        N_EXPERTS - 1).astype(jnp.int32)
    nblk = (pad_end[-1] // EXP_BLK).astype(jnp.int32).reshape(1)
    eid, rank = assign[:, 0, :], assign[:, 1, :]
    onehot = eid[:, :, None] == jnp.arange(N_EXPERTS, dtype=jnp.int32)
    pos = rank + jnp.sum(jnp.where(onehot, pad_start.astype(jnp.int32), 0), axis=-1)
    return pos.reshape(-1).astype(jnp.int32), pad_end.astype(jnp.int32), blk_e, nblk, p_rows


def _rope_tables(n_ctx, n_lat):
    nf = ATT_HEAD_DIM // 4
    inv = ROPE_BASE ** (-jnp.arange(nf, dtype=F32) / nf)
    rows = n_lat // GRID_W
    pos_r = jnp.repeat(jnp.arange(rows, dtype=F32), GRID_W)
    pos_c = jnp.tile(jnp.arange(GRID_W, dtype=F32), rows)
    ang_r = pos_r[:, None] * inv
    ang_c = pos_c[:, None] * inv
    cos_h = jnp.concatenate([jnp.cos(ang_r)] * 2 + [jnp.cos(ang_c)] * 2, axis=-1)
    sin_h = jnp.concatenate([-jnp.sin(ang_r), jnp.sin(ang_r),
                             -jnp.sin(ang_c), jnp.sin(ang_c)], axis=-1)
    reps = LANES // ATT_HEAD_DIM
    cos_l = jnp.tile(cos_h, (1, reps))
    sin_l = jnp.tile(sin_h, (1, reps))
    cos_t = jnp.concatenate([jnp.ones((n_ctx, LANES), F32), cos_l], axis=0)
    sin_t = jnp.concatenate([jnp.zeros((n_ctx, LANES), F32), sin_l], axis=0)
    return cos_t, sin_t


def _projection_weights(w_in_l):
    offs = np.cumsum((0, 3 * CONV_DIM, ATT_Q_DIM, ATT_KV_DIM, ATT_KV_DIM,
                      MLSTM_DIM, MLSTM_DIM, MLSTM_DIM, MLSTM_DIM, N_GATE)).tolist()
    a, q, k, v, mq, mk, mv, mo, g = [w_in_l[:, offs[n]:offs[n + 1]] for n in range(9)]
    hd = ATT_HEAD_DIM

    def swap(w):
        return jnp.concatenate([w[:, hd:], w[:, :hd]], axis=1)

    w_tok = jnp.concatenate([a, q, k, swap(k), v, swap(v), mq, mk], axis=1).astype(BF16)
    w_chan = jnp.concatenate([mv, mo, g], axis=1).T.astype(BF16)
    w_gate = w_in_l[:, offs[9]:].astype(BF16)
    return w_tok, w_chan, w_gate


def kernel(x, c, ctx, c_ctx, w_ada, b_ada, w_in, conv_w, attn_sink, mlstm_gate_b, mlstm_norm_w,
           w_proj_a, w_proj_b, w_proj_c, w_out, ln1_g, ln1_b, w_route_group, b_route_group,
           w_route_expert, b_route_expert, w_expert_in, w_expert_out, ln2_g, ln2_b):
    B, L, D = x.shape
    n_ctx = ctx.shape[1]
    depth = w_ada.shape[0]
    T = n_ctx + L
    alpha = (2 * depth) ** 0.25
    assert D == D_MODEL and n_ctx % TOK_TILE == 0 and L % TOK_TILE == 0 and L % GRID_W == 0
    assert MLSTM_CH == LANES and ATT_QB == LANES
    nct = n_ctx // TOK_TILE

    nrows = -(-(B + 1) // 8) * 8
    cond = jnp.concatenate([c, c_ctx[None, :], jnp.zeros((nrows - B - 1, D), F32)], axis=0)
    mod_all = _ada(cond, w_ada, b_ada).reshape(depth, nrows, 6, D)

    cos_t, sin_t = _rope_tables(n_ctx, L)
    streams = (ctx, x)

    for i in range(depth):
        need_ctx = i < depth - 1
        mod = mod_all[i]
        w_tok, w_chan, w_gate = _projection_weights(w_in[i])

        a0u, q, kv, mqk, mvo_t, gcol, grow = _inproj(streams, mod, w_tok, w_chan, cos_t, sin_t, n_ctx)
        yb = _attention(q, kv, attn_sink[i], n_ctx)
        hf_t, hb_t = _mlstm(mqk, mvo_t, gcol, grow, mlstm_gate_b[i], n_ctx)

        off = 0 if need_ctx else nct
        w_route = jnp.pad(jnp.concatenate([w_route_group[i], w_route_expert[i]], axis=1).T,
                          ((0, LANES - N_ROUTE), (0, 0)))
        b_route = jnp.pad(jnp.concatenate([b_route_group[i], b_route_expert[i]]),
                          (0, LANES - N_ROUTE)).reshape(LANES, 1)
        x1, h2v, wts, assign, counts = _merge(
            streams, mod, a0u, yb, hf_t, hb_t, mvo_t, conv_w[i], mlstm_norm_w[i].reshape(MLSTM_DIM, 1),
            w_gate, w_proj_a[i].astype(BF16), w_proj_b[i].astype(BF16), w_proj_c[i].astype(BF16),
            w_out[i].astype(BF16), ln1_g[i].reshape(1, D), ln1_b[i].reshape(1, D),
            w_route, b_route, n_ctx, off, alpha)

        tn = x1.shape[1]
        pos, pend, blk_e, nblk, p_rows = _segments(counts, assign)
        xs = _dispatch(h2v, pos, pend, nblk, p_rows, TOK_TILE)
        y = _experts(xs, blk_e, nblk, w_expert_in, w_expert_out, i)
        streams = (_combine(x1, mod, wts, pos, y,
                            ln2_g[i].reshape(1, D), ln2_b[i].reshape(1, D),
                            nct if need_ctx else 0, alpha),)
    return streams[0]
```

```python
import functools

import jax
import jax.numpy as jnp
import numpy as np
from jax import lax
from jax.experimental import pallas as pl
from jax.experimental.pallas import tpu as pltpu

D_MODEL = 1024
GRID_W = 64
CONV_DIM = 256
ATT_HEADS = 8
ATT_KV_HEADS = 2
ATT_HEAD_DIM = 64
ATT_WINDOW = 128
ROPE_BASE = 10000.0
MLSTM_HEADS = 4
MLSTM_HEAD_DIM = 64
MLSTM_DIM = MLSTM_HEADS * MLSTM_HEAD_DIM
N_GROUPS = 4
EXPERTS_PER_GROUP = 8
N_EXPERTS = N_GROUPS * EXPERTS_PER_GROUP
TOP_K = 2
D_EXPERT = D_MODEL // 2
LN_EPS = 1e-6
NEG_INF = -1e30

ATT_Q_DIM = ATT_HEADS * ATT_HEAD_DIM
ATT_KV_DIM = ATT_KV_HEADS * ATT_HEAD_DIM
N_GATE = 4 * MLSTM_HEADS
MIX_COLS = 3 * CONV_DIM + ATT_Q_DIM + 2 * ATT_KV_DIM + 4 * MLSTM_DIM
N_ROUTE = N_GROUPS + N_EXPERTS

LANES = 128
VMEM_LIMIT = 56 * 1024 * 1024
TOK_TILE = 256
SUB_TILE = 256
ATT_QB = 128
ATT_STEP = 256
ATT_LAG_SOFTMAX = 1
ATT_LAG_VALUES = 2
MLSTM_CH = 128
MLSTM_STEP = 256
EXP_BLK = 512
EXP_ROWS = 256
HALO = 16
ROW_TILE = 8
GATHER_UNROLL = 8

F32 = jnp.float32
BF16 = jnp.bfloat16


def _cparams(sem):
    return pltpu.CompilerParams(dimension_semantics=sem, vmem_limit_bytes=VMEM_LIMIT)


def _ln(x, axis=-1):
    mu = jnp.mean(x, axis=axis, keepdims=True)
    xc = x - mu
    var = jnp.mean(xc * xc, axis=axis, keepdims=True)
    return xc * lax.rsqrt(var + LN_EPS)


def _sigmoid(x):
    return 0.5 * jnp.tanh(0.5 * x) + 0.5


def _split3(x):
    hi = x.astype(BF16)
    r1 = x - hi.astype(F32)
    mid = r1.astype(BF16)
    lo = (r1 - mid.astype(F32)).astype(BF16)
    return hi, mid, lo


def _dot(a, b):
    return jnp.dot(a, b, preferred_element_type=F32)


def _dot_nt(a, b):
    return lax.dot_general(a, b, (((1,), (1,)), ((), ())), preferred_element_type=F32)


def _dot3(x, rhs_b):
    return sum(_dot(part, rhs_b) for part in _split3(x))


def _dot3_l(lhs_b, x):
    return sum(_dot(lhs_b, part) for part in _split3(x))


def _ada_kernel(c_ref, w_ref, b_ref, o_ref):
    cv = c_ref[...]
    s = cv * _sigmoid(cv)
    o_ref[0] = _dot(s.astype(BF16), w_ref[0].astype(BF16)) + b_ref[0]


def _ada(cond, w_ada, b_ada):
    depth, d, n6 = w_ada.shape
    rows = cond.shape[0]
    nt = n6 // d
    return pl.pallas_call(
        _ada_kernel,
        grid=(depth, nt),
        in_specs=[
            pl.BlockSpec((rows, d), lambda l, j: (0, 0)),
            pl.BlockSpec((1, d, d), lambda l, j: (l, 0, j)),
            pl.BlockSpec((1, 1, d), lambda l, j: (l, 0, j)),
        ],
        out_specs=pl.BlockSpec((1, rows, d), lambda l, j: (l, 0, j)),
        out_shape=jax.ShapeDtypeStruct((depth, rows, n6), F32),
        compiler_params=_cparams(("arbitrary", "arbitrary")),
        name="ada_mod",
    )(cond, w_ada, b_ada.reshape(depth, 1, n6))


def _rope(x, cos, sin_signed, lane):
    swapped = jnp.where(lane % 32 < 16,
                        pltpu.roll(x, LANES - 16, axis=1),
                        pltpu.roll(x, 16, axis=1))
    return x * cos + swapped * sin_signed


_OFF_A = 0
_OFF_Q = _OFF_A + 3 * CONV_DIM
_OFF_K = _OFF_Q + ATT_Q_DIM
_OFF_V = _OFF_K + 2 * ATT_KV_DIM
_OFF_MQK = _OFF_V + 2 * ATT_KV_DIM
_W_TOK_COLS = _OFF_MQK + 2 * MLSTM_DIM


def _stream_specs(streams, tm, nct, off=0, nb=1):
    d = streams[0].shape[2]
    if len(streams) == 1:
        return [pl.BlockSpec((nb, tm, d), lambda b, i, *_: (b, i + off, 0))]
    return [pl.BlockSpec((nb, tm, d), lambda b, i, *_: (b, jnp.minimum(i + off, nct - 1), 0)),
            pl.BlockSpec((nb, tm, d), lambda b, i, *_: (b, jnp.maximum(i + off - nct, 0), 0))]


def _stream_tile(refs, i, nct):
    if len(refs) == 1:
        return refs[0][0]
    return jnp.where(i < nct, refs[0][0], refs[1][0])


def _inproj_kernel(*refs, n_src, nct):
    (mod_ref, w_ref, wt_ref, cos_ref, sin_ref,
     a0u_ref, q_ref, kv_ref, mqk_ref, mvo_t_ref, gcol_ref, grow_ref) = refs[n_src:]
    x = _stream_tile(refs[:n_src], pl.program_id(1), nct)
    shift = mod_ref[0, 0:1, :]
    scale = mod_ref[0, 1:2, :]
    h = (_ln(x) * (1.0 + scale) + shift).astype(BF16)
    cos = cos_ref[...]
    sin = sin_ref[...]
    lane = lax.broadcasted_iota(jnp.int32, cos.shape, 1)

    def cols(lo, n):
        return _dot(h, w_ref[:, lo:lo + n])

    za = cols(_OFF_A, 3 * CONV_DIM)
    a0u_ref[0, :, 0:CONV_DIM] = za[:, 0:CONV_DIM].astype(BF16)
    a0u_ref[0, :, CONV_DIM:2 * CONV_DIM] = (
        za[:, CONV_DIM:2 * CONV_DIM] * za[:, 2 * CONV_DIM:3 * CONV_DIM]).astype(BF16)
    zq = cols(_OFF_Q, ATT_Q_DIM)
    qscale = ATT_HEAD_DIM ** -0.5
    for j in range(ATT_Q_DIM // LANES):
        piece = _rope(zq[:, j * LANES:(j + 1) * LANES], cos, sin, lane)
        q_ref[0, :, j * LANES:(j + 1) * LANES] = (piece * qscale).astype(BF16)
    zk = cols(_OFF_K, 2 * ATT_KV_DIM)
    for j in range(2):
        kv_ref[0, :, j * LANES:(j + 1) * LANES] = _rope(
            zk[:, j * LANES:(j + 1) * LANES], cos, sin, lane).astype(BF16)
    kv_ref[0, :, 2 * LANES:] = cols(_OFF_V, 2 * ATT_KV_DIM).astype(BF16)
    zm = cols(_OFF_MQK, 2 * MLSTM_DIM)
    mqk_ref[0, :, 0:MLSTM_DIM] = zm[:, 0:MLSTM_DIM].astype(BF16)
    mqk_ref[0, :, MLSTM_DIM:] = (zm[:, MLSTM_DIM:] * (MLSTM_HEAD_DIM ** -0.5)).astype(BF16)
    nchan = mvo_t_ref.shape[1]
    zt = _dot_nt(wt_ref[...], h)
    mvo_t_ref[0] = zt[0:nchan, :].astype(BF16)
    grow = zt[nchan:, :]
    grow_ref[0] = grow
    gcol_ref[0] = grow.T


def _inproj(streams, mod, w_tok, w_chan, cos_t, sin_t, n_ctx):
    B, _, D = streams[0].shape
    T = sum(s.shape[1] for s in streams) if len(streams) > 1 else streams[0].shape[1]
    tm = TOK_TILE
    nct = n_ctx // tm
    nchan = w_chan.shape[0] - N_GATE

    def tok(width):
        return pl.BlockSpec((1, tm, width), lambda b, i: (b, i, 0))

    def chan(rows):
        return pl.BlockSpec((1, rows, tm), lambda b, i: (b, 0, i))

    outs = [(2 * CONV_DIM, BF16), (ATT_Q_DIM, BF16), (4 * ATT_KV_DIM, BF16), (2 * MLSTM_DIM, BF16)]
    return pl.pallas_call(
        functools.partial(_inproj_kernel, n_src=len(streams), nct=nct),
        grid=(B, T // tm),
        in_specs=_stream_specs(streams, tm, nct) + [
            pl.BlockSpec((1, 6, D), lambda b, i: (jnp.where(i < nct, B, b), 0, 0)),
            pl.BlockSpec(w_tok.shape, lambda b, i: (0, 0)),
            pl.BlockSpec(w_chan.shape, lambda b, i: (0, 0)),
            pl.BlockSpec((tm, LANES), lambda b, i: (i, 0)),
            pl.BlockSpec((tm, LANES), lambda b, i: (i, 0)),
        ],
        out_specs=[tok(w) for w, _ in outs] + [
            chan(nchan), tok(N_GATE), chan(N_GATE)],
        out_shape=[jax.ShapeDtypeStruct((B, T, w), dt) for w, dt in outs] + [
            jax.ShapeDtypeStruct((B, nchan, T), BF16),
            jax.ShapeDtypeStruct((B, T, N_GATE), F32), jax.ShapeDtypeStruct((B, N_GATE, T), F32)],
        compiler_params=_cparams(("parallel", "arbitrary")),
        name="in_proj",
    )(*streams, mod, w_tok, w_chan, cos_t, sin_t)


def _attn_kernel(sink_ref, q_ref, kvc_ref, kvp_ref, kvm_ref, kvn_ref, o_ref, *, n_ctx_blk, n_blk):
    n_ctx = kvc_ref.shape[1]
    qb = kvp_ref.shape[1]
    nsub = q_ref.shape[1] // qb
    nk = n_ctx + 3 * qb
    half = LANES // 2

    lane_row = lax.broadcasted_iota(jnp.int32, (1, LANES), 1)
    keep = [(lane_row < half).astype(F32).astype(BF16), (lane_row >= half).astype(F32).astype(BF16)]
    pad_v = jnp.zeros((qb, LANES), BF16)
    qi = lax.broadcasted_iota(jnp.int32, (qb, qb), 0)
    ki = lax.broadcasted_iota(jnp.int32, (qb, qb), 1)
    lane_q = lax.broadcasted_iota(jnp.int32, (qb, LANES), 1)
    neg = jnp.full((qb, LANES), NEG_INF, F32)
    group = ATT_HEADS // ATT_KV_HEADS
    around = [kvp_ref[0]] + [kvm_ref[0, sb * qb:(sb + 1) * qb, :] for sb in range(nsub)] + [kvn_ref[0]]

    blocks = []
    for sb in range(nsub):
        g = pl.program_id(1) * nsub + sb
        is_lat = g >= n_ctx_blk
        has_prev = g >= n_ctx_blk + 1
        has_next = g < n_blk - 1
        kv_all = jnp.concatenate([kvc_ref[0]] + around[sb:sb + 3], axis=0)
        k_ext, v_ext = {}, {}
        for tile in range(2):
            for par in range(2):
                k_ext[tile, par] = kv_all[:, tile * LANES:(tile + 1) * LANES] * keep[par]
                v = jnp.concatenate([kv_all[:, (2 + tile) * LANES:(3 + tile) * LANES], pad_v], axis=0)
                v_ext[tile, par] = jnp.concatenate(
                    [v * keep[par], jnp.broadcast_to(keep[par], v.shape)], axis=1)
        blocks.append(dict(
            is_lat=is_lat, ok_prev=jnp.logical_and(ki >= qi, has_prev),
            ok_next=jnp.logical_and(ki <= qi, jnp.logical_and(has_next, is_lat)),
            k_ext=k_ext, v_ext=v_ext, q=q_ref[0, sb * qb:(sb + 1) * qb, :]))

    jobs = [(sb, pair, par) for sb in range(nsub) for pair in range(ATT_HEADS // 2) for par in range(2)]
    scores, probs, accs, outs = {}, {}, {}, {}

    def operands(job):
        sb, pair, par = job
        kvh = (2 * pair) // group
        tile = 0 if kvh == par else 1
        return blocks[sb], tile

    def stage_scores(job):
        sb, pair, par = job
        blk, tile = operands(job)
        scores[job] = _dot_nt(blk["q"][:, pair * LANES:(pair + 1) * LANES], blk["k_ext"][tile, par])

    def stage_softmax(job):
        sb, pair, par = job
        blk, _ = operands(job)
        s = scores.pop(job)
        s_ext = jnp.concatenate([
            s[:, 0:n_ctx],
            jnp.where(blk["ok_prev"], s[:, n_ctx:n_ctx + qb], neg),
            jnp.where(blk["is_lat"], s[:, n_ctx + qb:n_ctx + 2 * qb], neg),
            jnp.where(blk["ok_next"], s[:, n_ctx + 2 * qb:nk], neg),
            jnp.where(lane_q == 0, sink_ref[2 * pair + par], neg)], axis=1)
        m = jnp.max(s_ext, axis=-1, keepdims=True)
        probs[job] = jnp.exp(s_ext - m).astype(BF16)

    def stage_values(job):
        sb, pair, par = job
        blk, tile = operands(job)
        part = _dot(probs.pop(job), blk["v_ext"][tile, par])
        if par == 0:
            accs[sb, pair] = part
        else:
            acc = accs.pop((sb, pair)) + part
            outs[sb, pair] = acc[:, 0:LANES] / acc[:, LANES:]

    for t in range(len(jobs) + ATT_LAG_VALUES):
        if t < len(jobs):
            stage_scores(jobs[t])
        if 0 <= t - ATT_LAG_SOFTMAX < len(jobs):
            stage_softmax(jobs[t - ATT_LAG_SOFTMAX])
        if 0 <= t - ATT_LAG_VALUES < len(jobs):
            stage_values(jobs[t - ATT_LAG_VALUES])
    for sb in range(nsub):
        o_ref[0, sb * qb:(sb + 1) * qb, :] = jnp.concatenate(
            [outs[sb, pair] for pair in range(ATT_HEADS // 2)], axis=-1).astype(o_ref.dtype)


def _attention(q, kv, sink, n_ctx):
    B, T, _ = q.shape
    qb = ATT_QB
    nsub = ATT_STEP // qb
    n_blk = T // qb
    nq = T // ATT_STEP
    n_ctx_blk = n_ctx // qb
    kvw = kv.shape[2]
    grid_spec = pltpu.PrefetchScalarGridSpec(
        num_scalar_prefetch=1,
        grid=(B, nq),
        in_specs=[
            pl.BlockSpec((1, ATT_STEP, ATT_Q_DIM), lambda b, j, sk: (b, j, 0)),
            pl.BlockSpec((1, n_ctx, kvw), lambda b, j, sk: (b, 0, 0)),
            pl.BlockSpec((1, qb, kvw), lambda b, j, sk: (b, jnp.maximum(j * nsub - 1, 0), 0)),
            pl.BlockSpec((1, ATT_STEP, kvw), lambda b, j, sk: (b, j, 0)),
            pl.BlockSpec((1, qb, kvw), lambda b, j, sk: (b, jnp.minimum((j + 1) * nsub, n_blk - 1), 0)),
        ],
        out_specs=pl.BlockSpec((1, ATT_STEP, ATT_Q_DIM), lambda b, j, sk: (b, j, 0)),
    )
    return pl.pallas_call(
        functools.partial(_attn_kernel, n_ctx_blk=n_ctx_blk, n_blk=n_blk),
        grid_spec=grid_spec,
        out_shape=jax.ShapeDtypeStruct((B, T, ATT_Q_DIM), BF16),
        compiler_params=_cparams(("parallel", "arbitrary")),
        name="window_attn",
    )(sink, q, kv, kv, kv, kv)


def _log_sigmoid(x):
    return jnp.minimum(x, 0.0) - jnp.log1p(jnp.exp(-jnp.abs(x)))


def _rows_to_lanes(a, base):
    return jnp.concatenate([a[base + h:base + h + 1, :] for h in range(MLSTM_HEADS)], axis=1)


def _mlstm_dir(d, qk, v_t, gcol, grow, ct_bd, n_bd, m_prev):
    ch = qk.shape[0]
    nh = MLSTM_HEADS
    dh = MLSTM_HEAD_DIM
    wide = nh * ch
    fwd = d == 0

    r_i = lax.broadcasted_iota(jnp.int32, (ch, ch), 0)
    c_i = lax.broadcasted_iota(jnp.int32, (ch, ch), 1)
    seen_t = (r_i <= c_i) if fwd else (r_i >= c_i)
    seen_tt = (c_i <= r_i) if fwd else (c_i >= r_i)
    r_w = lax.broadcasted_iota(jnp.int32, (ch, wide), 0)
    s_w = lax.broadcasted_iota(jnp.int32, (ch, wide), 1) % ch
    seen_w = (r_w <= s_w) if fwd else (r_w >= s_w)

    base_i = 2 * d * nh
    base_f = base_i + nh

    lf_col = _log_sigmoid(gcol)
    cum_col = _dot3_l(seen_tt.astype(BF16), lf_col)
    lane16 = lax.broadcasted_iota(jnp.int32, gcol.shape, 1)
    z = jnp.where(jnp.logical_and(lane16 >= base_i, lane16 < base_f), gcol, -cum_col)
    ch16 = lax.broadcasted_iota(jnp.int32, (N_GATE, wide), 0)
    hd16 = lax.broadcasted_iota(jnp.int32, (N_GATE, wide), 1) // ch
    sel = jnp.logical_or(ch16 == base_i + hd16, ch16 == base_f + hd16).astype(BF16)
    x_t = _dot3(z, sel)

    lf_row = _log_sigmoid(grow)
    rhs2 = jnp.concatenate([seen_t.astype(BF16), jnp.ones((ch, ch), BF16)], axis=1)
    rows = _dot3(lf_row, rhs2)
    b_all = _rows_to_lanes(rows[:, 0:ch], base_f)
    g_all = _rows_to_lanes(rows[:, ch:], base_f)
    li_all = _rows_to_lanes(grow, base_i)
    yield

    dmat = jnp.where(seen_w, x_t + b_all, -jnp.inf)
    a = b_all + m_prev
    m = jnp.maximum(a, jnp.max(dmat, axis=0, keepdims=True))
    w_intra = jnp.exp(dmat - m)
    e_inter = jnp.exp(a - m)
    yield

    q = qk[:, 0:MLSTM_DIM]
    k = qk[:, MLSTM_DIM:]
    lb = lax.broadcasted_iota(jnp.int32, (1, MLSTM_DIM), 1) // dh
    q_bd = jnp.concatenate([q * (lb == h).astype(F32).astype(BF16) for h in range(nh)], axis=0)
    s_t = _dot_nt(k, q_bd) * w_intra
    nq = _rows_to_lanes(_dot_nt(n_bd.astype(BF16), q), 0)
    den = jnp.sum(s_t, axis=0, keepdims=True) + e_inter * nq
    inv = 1.0 / jnp.maximum(jnp.abs(den), jnp.exp(-m))
    inter_t = _dot_nt(ct_bd.astype(BF16), q)
    s_b = s_t.astype(BF16)
    yield
    outs = []
    for h in range(nh):
        seg = slice(h * ch, (h + 1) * ch)
        blk = slice(h * dh, (h + 1) * dh)
        num = _dot(v_t[blk, :], s_b[:, seg]) + e_inter[:, seg] * inter_t[blk, :]
        outs.append(num * inv[:, seg])
    h_t = jnp.concatenate(outs, axis=0)
    yield

    m_loc = g_all + jnp.max(x_t, axis=0, keepdims=True)
    m_new = jnp.maximum(g_all + m_prev, m_loc)
    sa = jnp.exp(g_all + m_prev - m_new)
    sb = jnp.exp(m_loc - m_new)
    e_loc = jnp.exp(g_all - b_all + li_all - m_loc)
    v_e = jnp.concatenate(
        [v_t[h * dh:(h + 1) * dh, :].astype(F32) * e_loc[:, h * ch:(h + 1) * ch] for h in range(nh)],
        axis=0).astype(BF16)
    ct_loc = _dot(v_e, k)
    yield
    reps = MLSTM_DIM // ch

    def per_head_rows(row, nrows):
        return jnp.concatenate(
            [jnp.broadcast_to(jnp.concatenate([row[:, h * ch:(h + 1) * ch]] * reps, axis=1),
                              (nrows, MLSTM_DIM)) for h in range(nh)], axis=0)

    eb = lax.broadcasted_iota(jnp.int32, (MLSTM_DIM, MLSTM_DIM), 0) // dh
    db = lax.broadcasted_iota(jnp.int32, (MLSTM_DIM, MLSTM_DIM), 1) // dh
    ct_new = jnp.where(eb == db, per_head_rows(sa, dh) * ct_bd + per_head_rows(sb, dh) * ct_loc, 0.0)
    nrow = n_bd.shape[0]
    e_rows = jnp.concatenate([e_loc[:, h * ch:(h + 1) * ch] for h in range(nh)]
                             + [jnp.zeros((nrow - nh, ch), F32)], axis=0).astype(BF16)
    n_loc = _dot(e_rows, k)
    pad = jnp.zeros((nrow - nh, MLSTM_DIM), F32)
    sa8 = jnp.concatenate([per_head_rows(sa, 1), pad], axis=0)
    sb8 = jnp.concatenate([per_head_rows(sb, 1), pad], axis=0)
    hb8 = lax.broadcasted_iota(jnp.int32, (nrow, MLSTM_DIM), 0)
    db8 = lax.broadcasted_iota(jnp.int32, (nrow, MLSTM_DIM), 1) // dh
    n_new = jnp.where(hb8 == db8, sa8 * n_bd + sb8 * n_loc, 0.0)
    return h_t, ct_new, n_new, m_new


def _mlstm_kernel(qk_f_ref, qk_b_ref, vt_f_ref, vt_b_ref, gc_f_ref, gc_b_ref, gr_f_ref, gr_b_ref,
                  bcol_ref, brow_ref, hf_ref, hb_ref, c_scr, n_scr, m_scr):
    j = pl.program_id(1)

    @pl.when(j == 0)
    def _():
        c_scr[...] = jnp.zeros_like(c_scr)
        n_scr[...] = jnp.zeros_like(n_scr)
        m_scr[...] = jnp.zeros_like(m_scr)

    states = [(c_scr[d], n_scr[d], m_scr[d]) for d in range(2)]
    dirs = ((qk_f_ref, vt_f_ref, gc_f_ref, gr_f_ref, hf_ref),
            (qk_b_ref, vt_b_ref, gc_b_ref, gr_b_ref, hb_ref))
    ch = MLSTM_CH
    nsub = qk_f_ref.shape[1] // ch

    def scan(d):
        qk_ref, vt_ref, gc_ref, gr_ref, out_ref = dirs[d]
        st = states[d]
        for sc in (range(nsub) if d == 0 else reversed(range(nsub))):
            ts = slice(sc * ch, (sc + 1) * ch)
            h_t, *st = yield from _mlstm_dir(
                d, qk_ref[0, ts, :], vt_ref[0, :, ts],
                gc_ref[0, ts, :] + bcol_ref[...], gr_ref[0, :, ts] + brow_ref[...], *st)
            out_ref[0, :, ts] = h_t
        states[d] = st

    live = [scan(0), scan(1)]
    while live:
        for g in list(live):
            if next(g, live) is live:
                live.remove(g)
    for d in range(2):
        c_scr[d] = states[d][0]
        n_scr[d] = states[d][1]
        m_scr[d] = states[d][2]


def _mlstm(mqk, mvo_t, gcol, grow, gate_b, n_ctx):
    B, T, _ = mqk.shape
    ch = MLSTM_STEP
    nc = T // ch
    ncc = n_ctx // ch

    def rev(j):
        return jnp.where(j < ncc, ncc - 1 - j, nc - 1 - (j - ncc))

    def tok(width, order):
        return pl.BlockSpec((1, ch, width), lambda b, j: (b, order(j), 0))

    def chan(rows, order):
        return pl.BlockSpec((1, rows, ch), lambda b, j: (b, 0, order(j)))

    ident = lambda j: j
    return pl.pallas_call(
        _mlstm_kernel,
        grid=(B, nc),
        in_specs=[
            tok(2 * MLSTM_DIM, ident), tok(2 * MLSTM_DIM, rev),
            chan(MLSTM_DIM, ident), chan(MLSTM_DIM, rev),
            tok(N_GATE, ident), tok(N_GATE, rev),
            chan(N_GATE, ident), chan(N_GATE, rev),
            pl.BlockSpec((1, N_GATE), lambda b, j: (0, 0)),
            pl.BlockSpec((N_GATE, 1), lambda b, j: (0, 0)),
        ],
        out_specs=[chan(MLSTM_DIM, ident), chan(MLSTM_DIM, rev)],
        out_shape=[jax.ShapeDtypeStruct((B, MLSTM_DIM, T), F32)] * 2,
        scratch_shapes=[
            pltpu.VMEM((2, MLSTM_DIM, MLSTM_DIM), F32),
            pltpu.VMEM((2, 8, MLSTM_DIM), F32),
            pltpu.VMEM((2, 1, MLSTM_HEADS * MLSTM_CH), F32),
        ],
        compiler_params=_cparams(("parallel", "arbitrary")),
        name="mlstm",
    )(mqk, mqk, mvo_t, mvo_t, gcol, gcol, grow, grow,
      gate_b.reshape(1, N_GATE), gate_b.reshape(N_GATE, 1))


MERGE_SAMPLES = 2


def _merge_kernel(*refs, n_src, off, n_ctx_tiles, alpha, n_batch):
    n_in = n_src + 7
    mod_ref = refs[n_in]
    consts = refs[n_in + 1:n_in + 13]
    outs = refs[n_in + 13:n_in + 17]
    counts_ref, carry = refs[n_in + 17], refs[n_in + 18]

    @pl.when(jnp.logical_and(pl.program_id(0) == 0, pl.program_id(1) == 0))
    def _():
        carry[...] = jnp.zeros_like(carry)

    is_ctx = pl.program_id(1) + off < n_ctx_tiles
    tiles = []
    for smp in range(MERGE_SAMPLES):
        row = jnp.where(is_ctx, n_batch, pl.program_id(0) * MERGE_SAMPLES + smp)
        tiles.append(_merge_tile(
            *[r.at[pl.ds(smp, 1)] for r in refs[:n_in]], mod_ref.at[pl.ds(row, 1)], *consts,
            *[r.at[pl.ds(smp, 1)] for r in outs], counts_ref, carry,
            n_src=n_src, off=off, n_ctx_tiles=n_ctx_tiles, alpha=alpha))
    live = list(tiles)
    while live:
        for g in list(live):
            if next(g, live) is live:
                live.remove(g)


def _merge_tile(*refs, n_src, off, n_ctx_tiles, alpha):
    (a0u_ref, up_ref, un_ref, yb_ref, hf_ref, hb_ref, mvo_t_ref, mod_ref,
     convw_ref, normw_ref, wg_ref, wpa_ref, wpb_ref, wpc_ref, wo_ref,
     ln_g_ref, ln_b_ref, wr_ref, br_ref, upper_ref,
     x1_ref, h2_ref, wt_ref, assign_ref, counts_ref, carry) = refs[n_src:]
    i = pl.program_id(1) + off
    nt = pl.num_programs(1) + off
    x_tile = _stream_tile(refs[:n_src], i, n_ctx_tiles)
    tm = x_tile.shape[0]
    d = x_tile.shape[1]

    a0 = a0u_ref[0, :, 0:CONV_DIM].astype(F32)
    u = a0u_ref[0, :, CONV_DIM:].astype(F32)
    prev_ok = jnp.logical_and(i != 0, i != n_ctx_tiles)
    next_ok = jnp.logical_and(i != n_ctx_tiles - 1, i != nt - 1)
    u_prev = jnp.where(prev_ok, up_ref[0, HALO - 1:HALO, CONV_DIM:].astype(F32), 0.0)
    u_next = jnp.where(next_ok, un_ref[0, 0:1, CONV_DIM:].astype(F32), 0.0)
    row = lax.broadcasted_iota(jnp.int32, u.shape, 0)
    u_dn = jnp.where(row == 0, u_prev, pltpu.roll(u, 1, axis=0))
    u_up = jnp.where(row == tm - 1, u_next, pltpu.roll(u, tm - 1, axis=0))
    cw = convw_ref[...]
    ya_all = (a0 * (u_dn * cw[0:1, :] + u * cw[1:2, :] + u_up * cw[2:3, :])).astype(BF16)

    w_hi = wr_ref[...].astype(BF16)
    sub = lax.broadcasted_iota(jnp.int32, (w_hi.shape[0], SUB_TILE), 0)
    big = jnp.int32(2 * LANES)
    is_g = sub < N_GROUPS
    assert tm == SUB_TILE
    yield

    for part in range(tm // SUB_TILE):
        rs = slice(part * SUB_TILE, (part + 1) * SUB_TILE)
        x = x_tile[rs, :]
        h = (_ln(x) * (1.0 + mod_ref[0, 1:2, :]) + mod_ref[0, 0:1, :]).astype(BF16)

        hm = hf_ref[0, :, rs] + hb_ref[0, :, rs]
        normed = jnp.concatenate(
            [_ln(hm[hd * MLSTM_HEAD_DIM:(hd + 1) * MLSTM_HEAD_DIM, :], axis=0)
             for hd in range(MLSTM_HEADS)], axis=0)
        yc_t = _sigmoid(mvo_t_ref[0, :, rs].astype(F32)) * (normed * normw_ref[...])
        yc = yc_t.T
        yield

        pa = _dot(ya_all[rs, :], wpa_ref[...])
        pb = _dot(yb_ref[0, rs, :], wpb_ref[...])
        pc = _dot(yc.astype(BF16), wpc_ref[...])
        yield
        merged = None
        for n, proj in enumerate((pa, pb, pc)):
            gated = _sigmoid(_dot(h, wg_ref[:, n * d:(n + 1) * d])) * proj
            merged = gated if merged is None else merged + gated
            yield
        yl = _dot(merged.astype(BF16), wo_ref[...])
        yield

        x1 = _ln(alpha * x + mod_ref[0, 2:3, :] * yl) * ln_g_ref[...] + ln_b_ref[...]
        x1_ref[0, rs, :] = x1
        yield
        h2 = _ln(x1) * (1.0 + mod_ref[0, 4:5, :]) + mod_ref[0, 3:4, :]
        for s in range(d // LANES):
            h2_ref[0, pl.ds(part * SUB_TILE * ROW_TILE + s, SUB_TILE, stride=ROW_TILE), :] = (
                h2[:, s * LANES:(s + 1) * LANES])
        yield

        lg = _dot_nt(w_hi, h2.astype(BF16)) + br_ref[...]
        gl = jnp.where(is_g, lg, -jnp.inf)
        g_max = jnp.max(gl, axis=0, keepdims=True)
        g_sel = jnp.min(jnp.where(gl == g_max, sub, big), axis=0, keepdims=True)
        g_p = 1.0 / jnp.sum(jnp.where(is_g, jnp.exp(gl - g_max), 0.0), axis=0, keepdims=True)
        lo = N_GROUPS + EXPERTS_PER_GROUP * g_sel
        el = jnp.where(jnp.logical_and(sub >= lo, sub < lo + EXPERTS_PER_GROUP), lg, -jnp.inf)
        e1 = jnp.max(el, axis=0, keepdims=True)
        i1 = jnp.min(jnp.where(el == e1, sub, big), axis=0, keepdims=True)
        el2 = jnp.where(sub == i1, -jnp.inf, el)
        e2 = jnp.max(el2, axis=0, keepdims=True)
        i2 = jnp.min(jnp.where(el2 == e2, sub, big), axis=0, keepdims=True)
        t = jnp.exp(e2 - e1)
        w1 = g_p / (1.0 + t)
        w2 = w1 * t
        eid1 = (i1 - N_GROUPS).astype(F32)
        eid2 = (i2 - N_GROUPS).astype(F32)
        rows = jnp.concatenate([eid1, eid2, w1, w2, jnp.zeros((4, SUB_TILE), F32)], axis=0)
        wt_ref[0, rs, :] = rows.T
        yield
        eid = jnp.concatenate([eid1, eid2], axis=1)
        sub_e = lax.broadcasted_iota(jnp.int32, (N_EXPERTS, eid.shape[1]), 0).astype(F32)
        onehot = jnp.where(sub_e == eid, 1.0, 0.0)
        earlier = _dot(onehot.astype(BF16), upper_ref[...])
        seen = carry[...]
        rank = jnp.sum(onehot * (earlier + seen), axis=0, keepdims=True)
        assign_ref[0, 0] = jnp.concatenate([eid, rank], axis=0).astype(jnp.int32)
        seen = seen + jnp.sum(onehot, axis=1, keepdims=True)
        carry[...] = seen
        counts_ref[...] = seen


def _merge(streams, mod, a0u, yb, hf_t, hb_t, mvo_t, conv_w, norm_w, w_gate, w_pa, w_pb, w_pc, w_o,
           ln_g, ln_b, w_route, b_route, n_ctx, off, alpha):
    B, _, D = streams[0].shape
    T = a0u.shape[1]
    tm = TOK_TILE
    nct = n_ctx // tm
    nt = T // tm - off
    tn = nt * tm
    hpt = tm // HALO
    nhalo = T // HALO

    ns = MERGE_SAMPLES
    assert B % ns == 0

    def tok(width):
        return pl.BlockSpec((ns, tm, width), lambda b, i: (b, i + off, 0))

    def chan(rows):
        return pl.BlockSpec((ns, rows, tm), lambda b, i: (b, 0, i + off))

    def full(a):
        return pl.BlockSpec(a.shape, lambda b, i: (0,) * a.ndim)

    def otok(width):
        return pl.BlockSpec((ns, tm, width), lambda b, i: (b, i, 0))

    a_i = lax.broadcasted_iota(jnp.int32, (TOP_K * tm, TOP_K * tm), 0)
    b_i = lax.broadcasted_iota(jnp.int32, (TOP_K * tm, TOP_K * tm), 1)
    upper = (a_i < b_i).astype(BF16)
    consts = [conv_w, norm_w, w_gate, w_pa, w_pb, w_pc, w_o, ln_g, ln_b, w_route, b_route, upper]
    return pl.pallas_call(
        functools.partial(_merge_kernel, n_src=len(streams), off=off, n_ctx_tiles=nct, alpha=alpha,
                          n_batch=B),
        grid=(B // ns, nt),
        in_specs=_stream_specs(streams, tm, nct, off, ns) + [
            tok(2 * CONV_DIM),
            pl.BlockSpec((ns, HALO, 2 * CONV_DIM),
                         lambda b, i: (b, jnp.maximum((i + off) * hpt - 1, 0), 0)),
            pl.BlockSpec((ns, HALO, 2 * CONV_DIM),
                         lambda b, i: (b, jnp.minimum((i + off + 1) * hpt, nhalo - 1), 0)),
            tok(ATT_Q_DIM), chan(MLSTM_DIM), chan(MLSTM_DIM),
            pl.BlockSpec((ns, MLSTM_DIM, tm), lambda b, i: (b, 1, i + off)),
            full(mod),
        ] + [full(a) for a in consts],
        out_specs=[otok(D),
                   pl.BlockSpec((ns, tm * ROW_TILE, LANES), lambda b, i: (b, i, 0)),
                   otok(ROW_TILE),
                   pl.BlockSpec((ns, 1, 2, TOP_K * tm), lambda b, i: (b, i, 0, 0)),
                   pl.BlockSpec((N_EXPERTS, 1), lambda b, i: (0, 0))],
        out_shape=[jax.ShapeDtypeStruct((B, tn, D), F32),
                   jax.ShapeDtypeStruct((B, tn * ROW_TILE, LANES), F32),
                   jax.ShapeDtypeStruct((B, tn, ROW_TILE), F32),
                   jax.ShapeDtypeStruct((B, nt, 2, TOP_K * tm), jnp.int32),
                   jax.ShapeDtypeStruct((N_EXPERTS, 1), F32)],
        scratch_shapes=[pltpu.VMEM((N_EXPERTS, 1), F32)],
        compiler_params=_cparams(("arbitrary", "arbitrary")),
        name="merge",
    )(*streams, a0u, a0u, a0u, yb, hf_t, hb_t, mvo_t, mod, *consts)


def _dispatch_kernel(pos_ref, pend_ref, nblk_ref, h_ref, xs_hbm, zbuf, sem, zsem):
    s = pl.program_id(0)
    tm = h_ref.shape[0] // ROW_TILE
    zrows = zbuf.shape[0]
    last_blk = xs_hbm.shape[0] // zrows - 1

    @pl.when(s == 0)
    def _():
        zbuf[...] = jnp.zeros_like(zbuf)

        def zero_copy(start):
            return pltpu.make_async_copy(
                zbuf, xs_hbm.at[pl.ds(pl.multiple_of(start, ROW_TILE), zrows), :], zsem)

        jobs = []
        for e in range(N_EXPERTS):
            before = pend_ref[e - 1] if e > 0 else 0
            jobs.append((pend_ref[e] * ROW_TILE - zrows, pend_ref[e] > before))
        for e in range(N_EXPERTS):
            jobs.append(((nblk_ref[0] + e) * zrows, nblk_ref[0] + e <= last_blk))
        for st, needed in jobs:
            @pl.when(needed)
            def _(st=st):
                zero_copy(st).start()
        for st, needed in jobs:
            @pl.when(needed)
            def _(st=st):
                zero_copy(st).wait()

    def row_copy(r, k):
        src = pl.multiple_of(r * ROW_TILE, ROW_TILE)
        dst = pl.multiple_of(pos_ref[(s * TOP_K + k) * tm + r] * ROW_TILE, ROW_TILE)
        return pltpu.make_async_copy(h_ref.at[pl.ds(src, ROW_TILE), :],
                                     xs_hbm.at[pl.ds(dst, ROW_TILE), :], sem)

    def body(r, carry):
        for k in range(TOP_K):
            row_copy(r, k).start()
        return carry
    lax.fori_loop(0, tm, body, 0, unroll=GATHER_UNROLL)
    for k in range(TOP_K):
        pltpu.make_async_copy(h_ref, xs_hbm.at[pl.ds(0, tm * ROW_TILE), :], sem).wait()


def _dispatch(h2v, pos, pend, nblk, p_rows, tm):
    rows = h2v.shape[0]
    nsteps = rows // (tm * ROW_TILE)
    grid_spec = pltpu.PrefetchScalarGridSpec(
        num_scalar_prefetch=3,
        grid=(nsteps,),
        in_specs=[pl.BlockSpec((tm * ROW_TILE, LANES), lambda s, pos, pend, nbk: (s, 0))],
        out_specs=pl.BlockSpec(memory_space=pl.ANY),
        scratch_shapes=[
            pltpu.VMEM((EXP_BLK * ROW_TILE, LANES), F32),
            pltpu.SemaphoreType.DMA,
            pltpu.SemaphoreType.DMA,
        ],
    )
    return pl.pallas_call(
        _dispatch_kernel,
        grid_spec=grid_spec,
        out_shape=jax.ShapeDtypeStruct((p_rows * ROW_TILE, LANES), F32),
        compiler_params=_cparams(("arbitrary",)),
        name="dispatch",
    )(pos, pend, nblk, h2v)


def _expert_kernel(blke_ref, nblk_ref, xs_ref, wi_ref, wo_ref, y_ref, wi_bf, wo_bf):
    i = pl.program_id(0)
    nb = nblk_ref[0]
    blk = xs_ref.shape[0] // ROW_TILE
    nsl = wi_ref.shape[2] // LANES

    @pl.when(i < nb)
    def _():
        e_now = blke_ref[i]
        e_before = blke_ref[jnp.maximum(i - 1, 0)]

        @pl.when(jnp.logical_or(i == 0, e_now != e_before))
        def _():
            wi_bf[...] = wi_ref[0, 0].astype(BF16)
            wo_bf[...] = wo_ref[0, 0].astype(BF16)

        def rows_part(part):
            r0 = part * EXP_ROWS * ROW_TILE
            xin = jnp.concatenate(
                [xs_ref[pl.ds(r0 + s, EXP_ROWS, stride=ROW_TILE), :] for s in range(nsl)],
                axis=1).astype(BF16)
            yield
            mid = _dot(xin, wi_bf[...])
            yield
            gt = mid[:, 0:D_EXPERT]
            up = mid[:, D_EXPERT:]
            act = ((gt * _sigmoid(gt)) * up).astype(BF16)
            yield
            y = _dot(act, wo_bf[...])
            yield
            for s in range(nsl):
                y_ref[pl.ds(r0 + s, EXP_ROWS, stride=ROW_TILE), :] = y[:, s * LANES:(s + 1) * LANES]

        parts = [rows_part(p) for p in range(blk // EXP_ROWS)]
        for t in range(len(parts) + 4):
            for p, gen in enumerate(parts):
                if 0 <= t - p <= 4:
                    next(gen, None)

    @pl.when(i >= nb)
    def _():
        y_ref[...] = jnp.zeros_like(y_ref)


def _experts(xs, blk_e, nblk, w_ei, w_eo, layer):
    d = w_ei.shape[2]
    blk = EXP_BLK
    nb = xs.shape[0] // (blk * ROW_TILE)
    grid_spec = pltpu.PrefetchScalarGridSpec(
        num_scalar_prefetch=2,
        grid=(nb,),
        in_specs=[
            pl.BlockSpec((blk * ROW_TILE, LANES),
                         lambda i, be, nbk: (jnp.minimum(i, jnp.maximum(nbk[0] - 1, 0)), 0)),
            pl.BlockSpec((1, 1, d, 2 * D_EXPERT), lambda i, be, nbk: (layer, be[i], 0, 0)),
            pl.BlockSpec((1, 1, D_EXPERT, d), lambda i, be, nbk: (layer, be[i], 0, 0)),
        ],
        out_specs=pl.BlockSpec((blk * ROW_TILE, LANES), lambda i, be, nbk: (i, 0)),
        scratch_shapes=[
            pltpu.VMEM((d, 2 * D_EXPERT), BF16),
            pltpu.VMEM((D_EXPERT, d), BF16),
        ],
    )
    return pl.pallas_call(
        _expert_kernel,
        grid_spec=grid_spec,
        out_shape=jax.ShapeDtypeStruct(xs.shape, F32),
        compiler_params=_cparams(("arbitrary",)),
        name="experts",
    )(blk_e, nblk, xs, w_ei, w_eo)


def _combine_kernel(pos_ref, x_ref, mod_ref, wt_ref, ln_g_ref, ln_b_ref, y_hbm, o_ref,
                    ybuf, sem, *, alpha):
    b = pl.program_id(0)
    i = pl.program_id(1)
    nt = pl.num_programs(1)
    tm = x_ref.shape[1]
    step = b * nt + i
    nsteps = pl.num_programs(0) * nt

    def start_gather(s, slot):
        def body(r, carry):
            dst = pl.multiple_of(r * ROW_TILE, ROW_TILE)
            for k in range(TOP_K):
                src = pl.multiple_of(pos_ref[(s * TOP_K + k) * tm + r] * ROW_TILE, ROW_TILE)
                pltpu.make_async_copy(y_hbm.at[pl.ds(src, ROW_TILE), :],
                                      ybuf.at[slot, k, pl.ds(dst, ROW_TILE), :], sem.at[slot]).start()
            return carry
        lax.fori_loop(0, tm, body, 0, unroll=GATHER_UNROLL)

    @pl.when(step == 0)
    def _():
        start_gather(0, 0)

    @pl.when(step + 1 < nsteps)
    def _():
        start_gather(step + 1, (step + 1) % 2)

    slot = step % 2
    for k in range(TOP_K):
        pltpu.make_async_copy(y_hbm.at[pl.ds(0, tm * ROW_TILE), :], ybuf.at[slot, k],
                              sem.at[slot]).wait()
    wt = wt_ref[0]
    w0 = wt[:, 2:3]
    w1 = wt[:, 3:4]
    f = jnp.concatenate(
        [w0 * ybuf[slot, 0, pl.ds(j, tm, stride=ROW_TILE), :]
         + w1 * ybuf[slot, 1, pl.ds(j, tm, stride=ROW_TILE), :]
         for j in range(x_ref.shape[2] // LANES)], axis=1)
    x = x_ref[0]
    o_ref[0] = _ln(alpha * x + mod_ref[0, 5:6, :] * f) * ln_g_ref[...] + ln_b_ref[...]


def _combine(x1, mod, wts, pos, y, ln_g, ln_b, n_ctx_tiles, alpha):
    B, tn, D = x1.shape
    tm = TOK_TILE
    nt = tn // tm
    grid_spec = pltpu.PrefetchScalarGridSpec(
        num_scalar_prefetch=1,
        grid=(B, nt),
        in_specs=[
            pl.BlockSpec((1, tm, D), lambda b, i, pos: (b, i, 0)),
            pl.BlockSpec((1, 6, D), lambda b, i, pos: (jnp.where(i < n_ctx_tiles, B, b), 0, 0)),
            pl.BlockSpec((1, tm, ROW_TILE), lambda b, i, pos: (b, i, 0)),
            pl.BlockSpec((1, D), lambda b, i, pos: (0, 0)),
            pl.BlockSpec((1, D), lambda b, i, pos: (0, 0)),
            pl.BlockSpec(memory_space=pl.ANY),
        ],
        out_specs=pl.BlockSpec((1, tm, D), lambda b, i, pos: (b, i, 0)),
        scratch_shapes=[
            pltpu.VMEM((2, TOP_K, tm * ROW_TILE, LANES), F32),
            pltpu.SemaphoreType.DMA((2,)),
        ],
    )
    return pl.pallas_call(
        functools.partial(_combine_kernel, alpha=alpha),
        grid_spec=grid_spec,
        out_shape=jax.ShapeDtypeStruct((B, tn, D), F32),
        compiler_params=_cparams(("arbitrary", "arbitrary")),
        name="combine",
    )(pos, x1, mod, wts, ln_g, ln_b, y)


def _segments(counts, assign):
    counts = counts.reshape(N_EXPERTS).astype(jnp.int32)
    n_assign = assign.shape[0] * assign.shape[2]
    padded = (counts + EXP_BLK - 1) // EXP_BLK * EXP_BLK
    pad_end = jnp.cumsum(padded)
    pad_start = pad_end - padded
    p_rows = n_assign + N_EXPERTS * EXP_BLK
    nb = p_rows // EXP_BLK
    blk_first = jnp.arange(nb, dtype=jnp.int32) * EXP_BLK
    blk_e = jnp.minimum(jnp.sum((pad_end[None, :] <= blk_first[:, None]).astype(jnp.int32), axis=1),
                        N_EXPERTS - 1).astype(jnp.int32)
    nblk = (pad_end[-1] // EXP_BLK).astype(jnp.int32).reshape(1)
    eid, rank = assign[:, 0, :], assign[:, 1, :]
    onehot = eid[:, :, None] == jnp.arange(N_EXPERTS, dtype=jnp.int32)
    pos = rank + jnp.sum(jnp.where(onehot, pad_start.astype(jnp.int32), 0), axis=-1)
    return pos.reshape(-1).astype(jnp.int32), pad_end.astype(jnp.int32), blk_e, nblk, p_rows


def _rope_tables(n_ctx, n_lat):
    nf = ATT_HEAD_DIM // 4
    inv = ROPE_BASE ** (-jnp.arange(nf, dtype=F32) / nf)
    rows = n_lat // GRID_W
    pos_r = jnp.repeat(jnp.arange(rows, dtype=F32), GRID_W)
    pos_c = jnp.tile(jnp.arange(GRID_W, dtype=F32), rows)
    ang_r = pos_r[:, None] * inv
    ang_c = pos_c[:, None] * inv
    cos_h = jnp.concatenate([jnp.cos(ang_r)] * 2 + [jnp.cos(ang_c)] * 2, axis=-1)
    sin_h = jnp.concatenate([-jnp.sin(ang_r), jnp.sin(ang_r),
                             -jnp.sin(ang_c), jnp.sin(ang_c)], axis=-1)
    reps = LANES // ATT_HEAD_DIM
    cos_l = jnp.tile(cos_h, (1, reps))
    sin_l = jnp.tile(sin_h, (1, reps))
    cos_t = jnp.concatenate([jnp.ones((n_ctx, LANES), F32), cos_l], axis=0)
    sin_t = jnp.concatenate([jnp.zeros((n_ctx, LANES), F32), sin_l], axis=0)
    return cos_t, sin_t


def _projection_weights(w_in_l):
    offs = np.cumsum((0, 3 * CONV_DIM, ATT_Q_DIM, ATT_KV_DIM, ATT_KV_DIM,
                      MLSTM_DIM, MLSTM_DIM, MLSTM_DIM, MLSTM_DIM, N_GATE)).tolist()
    a, q, k, v, mq, mk, mv, mo, g = [w_in_l[:, offs[n]:offs[n + 1]] for n in range(9)]
    hd = ATT_HEAD_DIM

    def swap(w):
        return jnp.concatenate([w[:, hd:], w[:, :hd]], axis=1)

    w_tok = jnp.concatenate([a, q, k, swap(k), v, swap(v), mq, mk], axis=1).astype(BF16)
    w_chan = jnp.concatenate([mv, mo, g], axis=1).T.astype(BF16)
    w_gate = w_in_l[:, offs[9]:].astype(BF16)
    return w_tok, w_chan, w_gate


def kernel(x, c, ctx, c_ctx, w_ada, b_ada, w_in, conv_w, attn_sink, mlstm_gate_b, mlstm_norm_w,
           w_proj_a, w_proj_b, w_proj_c, w_out, ln1_g, ln1_b, w_route_group, b_route_group,
           w_route_expert, b_route_expert, w_expert_in, w_expert_out, ln2_g, ln2_b):
    B, L, D = x.shape
    n_ctx = ctx.shape[1]
    depth = w_ada.shape[0]
    T = n_ctx + L
    alpha = (2 * depth) ** 0.25
    assert D == D_MODEL and n_ctx % TOK_TILE == 0 and L % TOK_TILE == 0 and L % GRID_W == 0
    assert MLSTM_CH == LANES and ATT_QB == LANES
    nct = n_ctx // TOK_TILE

    nrows = -(-(B + 1) // 8) * 8
    cond = jnp.concatenate([c, c_ctx[None, :], jnp.zeros((nrows - B - 1, D), F32)], axis=0)
    mod_all = _ada(cond, w_ada, b_ada).reshape(depth, nrows, 6, D)

    cos_t, sin_t = _rope_tables(n_ctx, L)
    streams = (ctx, x)

    for i in range(depth):
        need_ctx = i < depth - 1
        mod = mod_all[i]
        w_tok, w_chan, w_gate = _projection_weights(w_in[i])

        a0u, q, kv, mqk, mvo_t, gcol, grow = _inproj(streams, mod, w_tok, w_chan, cos_t, sin_t, n_ctx)
        yb = _attention(q, kv, attn_sink[i], n_ctx)
        hf_t, hb_t = _mlstm(mqk, mvo_t, gcol, grow, mlstm_gate_b[i], n_ctx)

        off = 0 if need_ctx else nct
        w_route = jnp.pad(jnp.concatenate([w_route_group[i], w_route_expert[i]], axis=1).T,
                          ((0, LANES - N_ROUTE), (0, 0)))
        b_route = jnp.pad(jnp.concatenate([b_route_group[i], b_route_expert[i]]),
                          (0, LANES - N_ROUTE)).reshape(LANES, 1)
        x1, h2v, wts, assign, counts = _merge(
            streams, mod, a0u, yb, hf_t, hb_t, mvo_t, conv_w[i], mlstm_norm_w[i].reshape(MLSTM_DIM, 1),
            w_gate, w_proj_a[i].astype(BF16), w_proj_b[i].astype(BF16), w_proj_c[i].astype(BF16),
            w_out[i].astype(BF16), ln1_g[i].reshape(1, D), ln1_b[i].reshape(1, D),
            w_route, b_route, n_ctx, off, alpha)

        tn = x1.shape[1]
        pos, pend, blk_e, nblk, p_rows = _segments(counts, assign.reshape(-1, 2, TOP_K * TOK_TILE))
        xs = _dispatch(h2v.reshape(-1, LANES), pos, pend, nblk, p_rows, TOK_TILE)
        y = _experts(xs, blk_e, nblk, w_expert_in, w_expert_out, i)
        streams = (_combine(x1, mod, wts, pos, y,
                            ln2_g[i].reshape(1, D), ln2_b[i].reshape(1, D),
                            nct if need_ctx else 0, alpha),)
    return streams[0]
```

```python
import functools

import jax
import jax.numpy as jnp
import numpy as np
from jax import lax
from jax.experimental import pallas as pl
from jax.experimental.pallas import tpu as pltpu

D_MODEL = 1024
GRID_W = 64
CONV_DIM = 256
ATT_HEADS = 8
ATT_KV_HEADS = 2
ATT_HEAD_DIM = 64
ATT_WINDOW = 128
ROPE_BASE = 10000.0
MLSTM_HEADS = 4
MLSTM_HEAD_DIM = 64
MLSTM_DIM = MLSTM_HEADS * MLSTM_HEAD_DIM
N_GROUPS = 4
EXPERTS_PER_GROUP = 8
N_EXPERTS = N_GROUPS * EXPERTS_PER_GROUP
TOP_K = 2
D_EXPERT = D_MODEL // 2
LN_EPS = 1e-6
NEG_INF = -1e30

ATT_Q_DIM = ATT_HEADS * ATT_HEAD_DIM
ATT_KV_DIM = ATT_KV_HEADS * ATT_HEAD_DIM
N_GATE = 4 * MLSTM_HEADS
MIX_COLS = 3 * CONV_DIM + ATT_Q_DIM + 2 * ATT_KV_DIM + 4 * MLSTM_DIM
N_ROUTE = N_GROUPS + N_EXPERTS

LANES = 128
VMEM_LIMIT = 56 * 1024 * 1024
TOK_TILE = 256
PAIR_SAMPLES = 2
SUB_TILE = 256
ATT_QB = 128
ATT_STEP = 256
ATT_LAG_SOFTMAX = 1
ATT_LAG_VALUES = 2
MLSTM_CH = 128
MLSTM_STEP = 256
EXP_BLK = 512
EXP_ROWS = 256
HALO = 16
ROW_TILE = 8
GATHER_UNROLL = 8

F32 = jnp.float32
BF16 = jnp.bfloat16


def _cparams(sem):
    return pltpu.CompilerParams(dimension_semantics=sem, vmem_limit_bytes=VMEM_LIMIT)


def _ln(x, axis=-1):
    mu = jnp.mean(x, axis=axis, keepdims=True)
    xc = x - mu
    var = jnp.mean(xc * xc, axis=axis, keepdims=True)
    return xc * lax.rsqrt(var + LN_EPS)


def _sigmoid(x):
    return 0.5 * jnp.tanh(0.5 * x) + 0.5


def _split3(x):
    hi = x.astype(BF16)
    r1 = x - hi.astype(F32)
    mid = r1.astype(BF16)
    lo = (r1 - mid.astype(F32)).astype(BF16)
    return hi, mid, lo


def _dot(a, b):
    return jnp.dot(a, b, preferred_element_type=F32)


def _dot_nt(a, b):
    return lax.dot_general(a, b, (((1,), (1,)), ((), ())), preferred_element_type=F32)


def _dot3(x, rhs_b):
    return sum(_dot(part, rhs_b) for part in _split3(x))


def _dot3_l(lhs_b, x):
    return sum(_dot(lhs_b, part) for part in _split3(x))


def _ada_kernel(c_ref, w_ref, b_ref, o_ref):
    cv = c_ref[...]
    s = cv * _sigmoid(cv)
    o_ref[0] = _dot(s.astype(BF16), w_ref[0].astype(BF16)) + b_ref[0]


def _ada(cond, w_ada, b_ada):
    depth, d, n6 = w_ada.shape
    rows = cond.shape[0]
    nt = n6 // d
    return pl.pallas_call(
        _ada_kernel,
        grid=(depth, nt),
        in_specs=[
            pl.BlockSpec((rows, d), lambda l, j: (0, 0)),
            pl.BlockSpec((1, d, d), lambda l, j: (l, 0, j)),
            pl.BlockSpec((1, 1, d), lambda l, j: (l, 0, j)),
        ],
        out_specs=pl.BlockSpec((1, rows, d), lambda l, j: (l, 0, j)),
        out_shape=jax.ShapeDtypeStruct((depth, rows, n6), F32),
        compiler_params=_cparams(("arbitrary", "arbitrary")),
        name="ada_mod",
    )(cond, w_ada, b_ada.reshape(depth, 1, n6))


def _rope(x, cos, sin_signed, lane):
    swapped = jnp.where(lane % 32 < 16,
                        pltpu.roll(x, LANES - 16, axis=1),
                        pltpu.roll(x, 16, axis=1))
    return x * cos + swapped * sin_signed


_OFF_A = 0
_OFF_Q = _OFF_A + 3 * CONV_DIM
_OFF_K = _OFF_Q + ATT_Q_DIM
_OFF_V = _OFF_K + 2 * ATT_KV_DIM
_OFF_MQK = _OFF_V + 2 * ATT_KV_DIM
_W_TOK_COLS = _OFF_MQK + 2 * MLSTM_DIM


def _stream_specs(streams, tm, nct, off=0, nb=1):
    d = streams[0].shape[2]
    if len(streams) == 1:
        return [pl.BlockSpec((nb, tm, d), lambda b, i, *_: (b, i + off, 0))]
    return [pl.BlockSpec((nb, tm, d), lambda b, i, *_: (b, jnp.minimum(i + off, nct - 1), 0)),
            pl.BlockSpec((nb, tm, d), lambda b, i, *_: (b, jnp.maximum(i + off - nct, 0), 0))]


def _stream_tile(refs, i, nct):
    if len(refs) == 1:
        return refs[0][0]
    return jnp.where(i < nct, refs[0][0], refs[1][0])


def _inproj_kernel(*refs, n_src, nct, n_batch):
    mod_ref = refs[n_src]
    shared = refs[n_src + 1:n_src + 5]
    outs = refs[n_src + 5:]
    is_ctx = pl.program_id(1) < nct
    tiles = []
    for smp in range(PAIR_SAMPLES):
        row = jnp.where(is_ctx, n_batch, pl.program_id(0) * PAIR_SAMPLES + smp)
        tiles.append(_inproj_tile(
            *[r.at[pl.ds(smp, 1)] for r in refs[:n_src]], mod_ref.at[pl.ds(row, 1)], *shared,
            *[r.at[pl.ds(smp, 1)] for r in outs], n_src=n_src, nct=nct))
    live = list(tiles)
    while live:
        for g in list(live):
            if next(g, live) is live:
                live.remove(g)


def _inproj_tile(*refs, n_src, nct):
    (mod_ref, w_ref, wt_ref, cos_ref, sin_ref,
     a0u_ref, q_ref, kv_ref, mqk_ref, mvo_t_ref, gcol_ref, grow_ref) = refs[n_src:]
    x = _stream_tile(refs[:n_src], pl.program_id(1), nct)
    shift = mod_ref[0, 0:1, :]
    scale = mod_ref[0, 1:2, :]
    h = (_ln(x) * (1.0 + scale) + shift).astype(BF16)
    cos = cos_ref[...]
    sin = sin_ref[...]
    lane = lax.broadcasted_iota(jnp.int32, cos.shape, 1)
    yield

    def cols(lo, n):
        return _dot(h, w_ref[:, lo:lo + n])

    za = cols(_OFF_A, 3 * CONV_DIM)
    yield
    a0u_ref[0, :, 0:CONV_DIM] = za[:, 0:CONV_DIM].astype(BF16)
    a0u_ref[0, :, CONV_DIM:2 * CONV_DIM] = (
        za[:, CONV_DIM:2 * CONV_DIM] * za[:, 2 * CONV_DIM:3 * CONV_DIM]).astype(BF16)
    zq = cols(_OFF_Q, ATT_Q_DIM)
    yield
    qscale = ATT_HEAD_DIM ** -0.5
    for j in range(ATT_Q_DIM // LANES):
        piece = _rope(zq[:, j * LANES:(j + 1) * LANES], cos, sin, lane)
        q_ref[0, :, j * LANES:(j + 1) * LANES] = (piece * qscale).astype(BF16)
    zk = cols(_OFF_K, 2 * ATT_KV_DIM)
    yield
    for j in range(2):
        kv_ref[0, :, j * LANES:(j + 1) * LANES] = _rope(
            zk[:, j * LANES:(j + 1) * LANES], cos, sin, lane).astype(BF16)
    kv_ref[0, :, 2 * LANES:] = cols(_OFF_V, 2 * ATT_KV_DIM).astype(BF16)
    yield
    zm = cols(_OFF_MQK, 2 * MLSTM_DIM)
    yield
    mqk_ref[0, :, 0:MLSTM_DIM] = zm[:, 0:MLSTM_DIM].astype(BF16)
    mqk_ref[0, :, MLSTM_DIM:] = (zm[:, MLSTM_DIM:] * (MLSTM_HEAD_DIM ** -0.5)).astype(BF16)
    nchan = mvo_t_ref.shape[1]
    zt = _dot_nt(wt_ref[...], h)
    yield
    mvo_t_ref[0] = zt[0:nchan, :].astype(BF16)
    grow = zt[nchan:, :]
    grow_ref[0] = grow
    gcol_ref[0] = grow.T


def _inproj(streams, mod, w_tok, w_chan, cos_t, sin_t, n_ctx):
    B, _, D = streams[0].shape
    T = sum(s.shape[1] for s in streams) if len(streams) > 1 else streams[0].shape[1]
    tm = TOK_TILE
    nct = n_ctx // tm
    nchan = w_chan.shape[0] - N_GATE
    ns = PAIR_SAMPLES
    assert B % ns == 0

    def tok(width):
        return pl.BlockSpec((ns, tm, width), lambda b, i: (b, i, 0))

    def chan(rows):
        return pl.BlockSpec((ns, rows, tm), lambda b, i: (b, 0, i))

    outs = [(2 * CONV_DIM, BF16), (ATT_Q_DIM, BF16), (4 * ATT_KV_DIM, BF16), (2 * MLSTM_DIM, BF16)]
    return pl.pallas_call(
        functools.partial(_inproj_kernel, n_src=len(streams), nct=nct, n_batch=B),
        grid=(B // ns, T // tm),
        in_specs=_stream_specs(streams, tm, nct, 0, ns) + [
            pl.BlockSpec(mod.shape, lambda b, i: (0, 0, 0)),
            pl.BlockSpec(w_tok.shape, lambda b, i: (0, 0)),
            pl.BlockSpec(w_chan.shape, lambda b, i: (0, 0)),
            pl.BlockSpec((tm, LANES), lambda b, i: (i, 0)),
            pl.BlockSpec((tm, LANES), lambda b, i: (i, 0)),
        ],
        out_specs=[tok(w) for w, _ in outs] + [
            chan(nchan), tok(N_GATE), chan(N_GATE)],
        out_shape=[jax.ShapeDtypeStruct((B, T, w), dt) for w, dt in outs] + [
            jax.ShapeDtypeStruct((B, nchan, T), BF16),
            jax.ShapeDtypeStruct((B, T, N_GATE), F32), jax.ShapeDtypeStruct((B, N_GATE, T), F32)],
        compiler_params=_cparams(("parallel", "arbitrary")),
        name="in_proj",
    )(*streams, mod, w_tok, w_chan, cos_t, sin_t)


def _attn_kernel(sink_ref, q_ref, kvc_ref, kvp_ref, kvm_ref, kvn_ref, o_ref, *, n_ctx_blk, n_blk):
    n_ctx = kvc_ref.shape[1]
    qb = kvp_ref.shape[1]
    nsub = q_ref.shape[1] // qb
    nk = n_ctx + 3 * qb
    half = LANES // 2

    lane_row = lax.broadcasted_iota(jnp.int32, (1, LANES), 1)
    keep = [(lane_row < half).astype(F32).astype(BF16), (lane_row >= half).astype(F32).astype(BF16)]
    pad_v = jnp.zeros((qb, LANES), BF16)
    qi = lax.broadcasted_iota(jnp.int32, (qb, qb), 0)
    ki = lax.broadcasted_iota(jnp.int32, (qb, qb), 1)
    lane_q = lax.broadcasted_iota(jnp.int32, (qb, LANES), 1)
    neg = jnp.full((qb, LANES), NEG_INF, F32)
    group = ATT_HEADS // ATT_KV_HEADS
    around = [kvp_ref[0]] + [kvm_ref[0, sb * qb:(sb + 1) * qb, :] for sb in range(nsub)] + [kvn_ref[0]]

    blocks = []
    for sb in range(nsub):
        g = pl.program_id(1) * nsub + sb
        is_lat = g >= n_ctx_blk
        has_prev = g >= n_ctx_blk + 1
        has_next = g < n_blk - 1
        kv_all = jnp.concatenate([kvc_ref[0]] + around[sb:sb + 3], axis=0)
        k_ext, v_ext = {}, {}
        for tile in range(2):
            for par in range(2):
                k_ext[tile, par] = kv_all[:, tile * LANES:(tile + 1) * LANES] * keep[par]
                v = jnp.concatenate([kv_all[:, (2 + tile) * LANES:(3 + tile) * LANES], pad_v], axis=0)
                v_ext[tile, par] = jnp.concatenate(
                    [v * keep[par], jnp.broadcast_to(keep[par], v.shape)], axis=1)
        blocks.append(dict(
            is_lat=is_lat, ok_prev=jnp.logical_and(ki >= qi, has_prev),
            ok_next=jnp.logical_and(ki <= qi, jnp.logical_and(has_next, is_lat)),
            k_ext=k_ext, v_ext=v_ext, q=q_ref[0, sb * qb:(sb + 1) * qb, :]))

    jobs = [(sb, pair, par) for sb in range(nsub) for pair in range(ATT_HEADS // 2) for par in range(2)]
    scores, probs, accs, outs = {}, {}, {}, {}

    def operands(job):
        sb, pair, par = job
        kvh = (2 * pair) // group
        tile = 0 if kvh == par else 1
        return blocks[sb], tile

    def stage_scores(job):
        sb, pair, par = job
        blk, tile = operands(job)
        scores[job] = _dot_nt(blk["q"][:, pair * LANES:(pair + 1) * LANES], blk["k_ext"][tile, par])

    def stage_softmax(job):
        sb, pair, par = job
        blk, _ = operands(job)
        s = scores.pop(job)
        s_ext = jnp.concatenate([
            s[:, 0:n_ctx],
            jnp.where(blk["ok_prev"], s[:, n_ctx:n_ctx + qb], neg),
            jnp.where(blk["is_lat"], s[:, n_ctx + qb:n_ctx + 2 * qb], neg),
            jnp.where(blk["ok_next"], s[:, n_ctx + 2 * qb:nk], neg),
            jnp.where(lane_q == 0, sink_ref[2 * pair + par], neg)], axis=1)
        m = jnp.max(s_ext, axis=-1, keepdims=True)
        probs[job] = jnp.exp(s_ext - m).astype(BF16)

    def stage_values(job):
        sb, pair, par = job
        blk, tile = operands(job)
        part = _dot(probs.pop(job), blk["v_ext"][tile, par])
        if par == 0:
            accs[sb, pair] = part
        else:
            acc = accs.pop((sb, pair)) + part
            outs[sb, pair] = acc[:, 0:LANES] / acc[:, LANES:]

    for t in range(len(jobs) + ATT_LAG_VALUES):
        if t < len(jobs):
            stage_scores(jobs[t])
        if 0 <= t - ATT_LAG_SOFTMAX < len(jobs):
            stage_softmax(jobs[t - ATT_LAG_SOFTMAX])
        if 0 <= t - ATT_LAG_VALUES < len(jobs):
            stage_values(jobs[t - ATT_LAG_VALUES])
    for sb in range(nsub):
        o_ref[0, sb * qb:(sb + 1) * qb, :] = jnp.concatenate(
            [outs[sb, pair] for pair in range(ATT_HEADS // 2)], axis=-1).astype(o_ref.dtype)


def _attention(q, kv, sink, n_ctx):
    B, T, _ = q.shape
    qb = ATT_QB
    nsub = ATT_STEP // qb
    n_blk = T // qb
    nq = T // ATT_STEP
    n_ctx_blk = n_ctx // qb
    kvw = kv.shape[2]
    grid_spec = pltpu.PrefetchScalarGridSpec(
        num_scalar_prefetch=1,
        grid=(B, nq),
        in_specs=[
            pl.BlockSpec((1, ATT_STEP, ATT_Q_DIM), lambda b, j, sk: (b, j, 0)),
            pl.BlockSpec((1, n_ctx, kvw), lambda b, j, sk: (b, 0, 0)),
            pl.BlockSpec((1, qb, kvw), lambda b, j, sk: (b, jnp.maximum(j * nsub - 1, 0), 0)),
            pl.BlockSpec((1, ATT_STEP, kvw), lambda b, j, sk: (b, j, 0)),
            pl.BlockSpec((1, qb, kvw), lambda b, j, sk: (b, jnp.minimum((j + 1) * nsub, n_blk - 1), 0)),
        ],
        out_specs=pl.BlockSpec((1, ATT_STEP, ATT_Q_DIM), lambda b, j, sk: (b, j, 0)),
    )
    return pl.pallas_call(
        functools.partial(_attn_kernel, n_ctx_blk=n_ctx_blk, n_blk=n_blk),
        grid_spec=grid_spec,
        out_shape=jax.ShapeDtypeStruct((B, T, ATT_Q_DIM), BF16),
        compiler_params=_cparams(("parallel", "arbitrary")),
        name="window_attn",
    )(sink, q, kv, kv, kv, kv)


def _log_sigmoid(x):
    return jnp.minimum(x, 0.0) - jnp.log1p(jnp.exp(-jnp.abs(x)))


def _rows_to_lanes(a, base):
    return jnp.concatenate([a[base + h:base + h + 1, :] for h in range(MLSTM_HEADS)], axis=1)


def _mlstm_dir(d, qk, v_t, gcol, grow, ct_bd, n_bd, m_prev):
    ch = qk.shape[0]
    nh = MLSTM_HEADS
    dh = MLSTM_HEAD_DIM
    wide = nh * ch
    fwd = d == 0

    r_i = lax.broadcasted_iota(jnp.int32, (ch, ch), 0)
    c_i = lax.broadcasted_iota(jnp.int32, (ch, ch), 1)
    seen_t = (r_i <= c_i) if fwd else (r_i >= c_i)
    seen_tt = (c_i <= r_i) if fwd else (c_i >= r_i)
    r_w = lax.broadcasted_iota(jnp.int32, (ch, wide), 0)
    s_w = lax.broadcasted_iota(jnp.int32, (ch, wide), 1) % ch
    seen_w = (r_w <= s_w) if fwd else (r_w >= s_w)

    base_i = 2 * d * nh
    base_f = base_i + nh

    lf_col = _log_sigmoid(gcol)
    cum_col = _dot3_l(seen_tt.astype(BF16), lf_col)
    lane16 = lax.broadcasted_iota(jnp.int32, gcol.shape, 1)
    z = jnp.where(jnp.logical_and(lane16 >= base_i, lane16 < base_f), gcol, -cum_col)
    ch16 = lax.broadcasted_iota(jnp.int32, (N_GATE, wide), 0)
    hd16 = lax.broadcasted_iota(jnp.int32, (N_GATE, wide), 1) // ch
    sel = jnp.logical_or(ch16 == base_i + hd16, ch16 == base_f + hd16).astype(BF16)
    x_t = _dot3(z, sel)

    lf_row = _log_sigmoid(grow)
    rhs2 = jnp.concatenate([seen_t.astype(BF16), jnp.ones((ch, ch), BF16)], axis=1)
    rows = _dot3(lf_row, rhs2)
    b_all = _rows_to_lanes(rows[:, 0:ch], base_f)
    g_all = _rows_to_lanes(rows[:, ch:], base_f)
    li_all = _rows_to_lanes(grow, base_i)
    yield

    dmat = jnp.where(seen_w, x_t + b_all, -jnp.inf)
    a = b_all + m_prev
    m = jnp.maximum(a, jnp.max(dmat, axis=0, keepdims=True))
    w_intra = jnp.exp(dmat - m)
    e_inter = jnp.exp(a - m)
    yield

    q = qk[:, 0:MLSTM_DIM]
    k = qk[:, MLSTM_DIM:]
    lb = lax.broadcasted_iota(jnp.int32, (1, MLSTM_DIM), 1) // dh
    q_bd = jnp.concatenate([q * (lb == h).astype(F32).astype(BF16) for h in range(nh)], axis=0)
    s_t = _dot_nt(k, q_bd) * w_intra
    nq = _rows_to_lanes(_dot_nt(n_bd.astype(BF16), q), 0)
    den = jnp.sum(s_t, axis=0, keepdims=True) + e_inter * nq
    inv = 1.0 / jnp.maximum(jnp.abs(den), jnp.exp(-m))
    inter_t = _dot_nt(ct_bd.astype(BF16), q)
    s_b = s_t.astype(BF16)
    yield
    outs = []
    for h in range(nh):
        seg = slice(h * ch, (h + 1) * ch)
        blk = slice(h * dh, (h + 1) * dh)
        num = _dot(v_t[blk, :], s_b[:, seg]) + e_inter[:, seg] * inter_t[blk, :]
        outs.append(num * inv[:, seg])
    h_t = jnp.concatenate(outs, axis=0)
    yield

    m_loc = g_all + jnp.max(x_t, axis=0, keepdims=True)
    m_new = jnp.maximum(g_all + m_prev, m_loc)
    sa = jnp.exp(g_all + m_prev - m_new)
    sb = jnp.exp(m_loc - m_new)
    e_loc = jnp.exp(g_all - b_all + li_all - m_loc)
    v_e = jnp.concatenate(
        [v_t[h * dh:(h + 1) * dh, :].astype(F32) * e_loc[:, h * ch:(h + 1) * ch] for h in range(nh)],
        axis=0).astype(BF16)
    ct_loc = _dot(v_e, k)
    yield
    reps = MLSTM_DIM // ch

    def per_head_rows(row, nrows):
        return jnp.concatenate(
            [jnp.broadcast_to(jnp.concatenate([row[:, h * ch:(h + 1) * ch]] * reps, axis=1),
                              (nrows, MLSTM_DIM)) for h in range(nh)], axis=0)

    eb = lax.broadcasted_iota(jnp.int32, (MLSTM_DIM, MLSTM_DIM), 0) // dh
    db = lax.broadcasted_iota(jnp.int32, (MLSTM_DIM, MLSTM_DIM), 1) // dh
    ct_new = jnp.where(eb == db, per_head_rows(sa, dh) * ct_bd + per_head_rows(sb, dh) * ct_loc, 0.0)
    nrow = n_bd.shape[0]
    e_rows = jnp.concatenate([e_loc[:, h * ch:(h + 1) * ch] for h in range(nh)]
                             + [jnp.zeros((nrow - nh, ch), F32)], axis=0).astype(BF16)
    n_loc = _dot(e_rows, k)
    pad = jnp.zeros((nrow - nh, MLSTM_DIM), F32)
    sa8 = jnp.concatenate([per_head_rows(sa, 1), pad], axis=0)
    sb8 = jnp.concatenate([per_head_rows(sb, 1), pad], axis=0)
    hb8 = lax.broadcasted_iota(jnp.int32, (nrow, MLSTM_DIM), 0)
    db8 = lax.broadcasted_iota(jnp.int32, (nrow, MLSTM_DIM), 1) // dh
    n_new = jnp.where(hb8 == db8, sa8 * n_bd + sb8 * n_loc, 0.0)
    return h_t, ct_new, n_new, m_new


def _mlstm_kernel(qk_f_ref, qk_b_ref, vt_f_ref, vt_b_ref, gc_f_ref, gc_b_ref, gr_f_ref, gr_b_ref,
                  bcol_ref, brow_ref, hf_ref, hb_ref, c_scr, n_scr, m_scr):
    j = pl.program_id(1)

    @pl.when(j == 0)
    def _():
        c_scr[...] = jnp.zeros_like(c_scr)
        n_scr[...] = jnp.zeros_like(n_scr)
        m_scr[...] = jnp.zeros_like(m_scr)

    states = [(c_scr[d], n_scr[d], m_scr[d]) for d in range(2)]
    dirs = ((qk_f_ref, vt_f_ref, gc_f_ref, gr_f_ref, hf_ref),
            (qk_b_ref, vt_b_ref, gc_b_ref, gr_b_ref, hb_ref))
    ch = MLSTM_CH
    nsub = qk_f_ref.shape[1] // ch

    def scan(d):
        qk_ref, vt_ref, gc_ref, gr_ref, out_ref = dirs[d]
        st = states[d]
        for sc in (range(nsub) if d == 0 else reversed(range(nsub))):
            ts = slice(sc * ch, (sc + 1) * ch)
            h_t, *st = yield from _mlstm_dir(
                d, qk_ref[0, ts, :], vt_ref[0, :, ts],
                gc_ref[0, ts, :] + bcol_ref[...], gr_ref[0, :, ts] + brow_ref[...], *st)
            out_ref[0, :, ts] = h_t
        states[d] = st

    live = [scan(0), scan(1)]
    while live:
        for g in list(live):
            if next(g, live) is live:
                live.remove(g)
    for d in range(2):
        c_scr[d] = states[d][0]
        n_scr[d] = states[d][1]
        m_scr[d] = states[d][2]


def _mlstm(mqk, mvo_t, gcol, grow, gate_b, n_ctx):
    B, T, _ = mqk.shape
    ch = MLSTM_STEP
    nc = T // ch
    ncc = n_ctx // ch

    def rev(j):
        return jnp.where(j < ncc, ncc - 1 - j, nc - 1 - (j - ncc))

    def tok(width, order):
        return pl.BlockSpec((1, ch, width), lambda b, j: (b, order(j), 0))

    def chan(rows, order):
        return pl.BlockSpec((1, rows, ch), lambda b, j: (b, 0, order(j)))

    ident = lambda j: j
    return pl.pallas_call(
        _mlstm_kernel,
        grid=(B, nc),
        in_specs=[
            tok(2 * MLSTM_DIM, ident), tok(2 * MLSTM_DIM, rev),
            chan(MLSTM_DIM, ident), chan(MLSTM_DIM, rev),
            tok(N_GATE, ident), tok(N_GATE, rev),
            chan(N_GATE, ident), chan(N_GATE, rev),
            pl.BlockSpec((1, N_GATE), lambda b, j: (0, 0)),
            pl.BlockSpec((N_GATE, 1), lambda b, j: (0, 0)),
        ],
        out_specs=[chan(MLSTM_DIM, ident), chan(MLSTM_DIM, rev)],
        out_shape=[jax.ShapeDtypeStruct((B, MLSTM_DIM, T), F32)] * 2,
        scratch_shapes=[
            pltpu.VMEM((2, MLSTM_DIM, MLSTM_DIM), F32),
            pltpu.VMEM((2, 8, MLSTM_DIM), F32),
            pltpu.VMEM((2, 1, MLSTM_HEADS * MLSTM_CH), F32),
        ],
        compiler_params=_cparams(("parallel", "arbitrary")),
        name="mlstm",
    )(mqk, mqk, mvo_t, mvo_t, gcol, gcol, grow, grow,
      gate_b.reshape(1, N_GATE), gate_b.reshape(N_GATE, 1))


def _merge_kernel(*refs, n_src, off, n_ctx_tiles, alpha, n_batch):
    n_in = n_src + 7
    mod_ref = refs[n_in]
    consts = refs[n_in + 1:n_in + 13]
    outs = refs[n_in + 13:n_in + 17]
    counts_ref, carry = refs[n_in + 17], refs[n_in + 18]

    @pl.when(jnp.logical_and(pl.program_id(0) == 0, pl.program_id(1) == 0))
    def _():
        carry[...] = jnp.zeros_like(carry)

    is_ctx = pl.program_id(1) + off < n_ctx_tiles
    tiles = []
    for smp in range(PAIR_SAMPLES):
        row = jnp.where(is_ctx, n_batch, pl.program_id(0) * PAIR_SAMPLES + smp)
        tiles.append(_merge_tile(
            *[r.at[pl.ds(smp, 1)] for r in refs[:n_in]], mod_ref.at[pl.ds(row, 1)], *consts,
            *[r.at[pl.ds(smp, 1)] for r in outs], counts_ref, carry,
            n_src=n_src, off=off, n_ctx_tiles=n_ctx_tiles, alpha=alpha))
    live = list(tiles)
    while live:
        for g in list(live):
            if next(g, live) is live:
                live.remove(g)


def _merge_tile(*refs, n_src, off, n_ctx_tiles, alpha):
    (a0u_ref, up_ref, un_ref, yb_ref, hf_ref, hb_ref, mvo_t_ref, mod_ref,
     convw_ref, normw_ref, wg_ref, wpa_ref, wpb_ref, wpc_ref, wo_ref,
     ln_g_ref, ln_b_ref, wr_ref, br_ref, upper_ref,
     x1_ref, h2_ref, wt_ref, assign_ref, counts_ref, carry) = refs[n_src:]
    i = pl.program_id(1) + off
    nt = pl.num_programs(1) + off
    x_tile = _stream_tile(refs[:n_src], i, n_ctx_tiles)
    tm = x_tile.shape[0]
    d = x_tile.shape[1]

    a0 = a0u_ref[0, :, 0:CONV_DIM].astype(F32)
    u = a0u_ref[0, :, CONV_DIM:].astype(F32)
    prev_ok = jnp.logical_and(i != 0, i != n_ctx_tiles)
    next_ok = jnp.logical_and(i != n_ctx_tiles - 1, i != nt - 1)
    u_prev = jnp.where(prev_ok, up_ref[0, HALO - 1:HALO, CONV_DIM:].astype(F32), 0.0)
    u_next = jnp.where(next_ok, un_ref[0, 0:1, CONV_DIM:].astype(F32), 0.0)
    row = lax.broadcasted_iota(jnp.int32, u.shape, 0)
    u_dn = jnp.where(row == 0, u_prev, pltpu.roll(u, 1, axis=0))
    u_up = jnp.where(row == tm - 1, u_next, pltpu.roll(u, tm - 1, axis=0))
    cw = convw_ref[...]
    ya_all = (a0 * (u_dn * cw[0:1, :] + u * cw[1:2, :] + u_up * cw[2:3, :])).astype(BF16)

    w_hi = wr_ref[...].astype(BF16)
    sub = lax.broadcasted_iota(jnp.int32, (w_hi.shape[0], SUB_TILE), 0)
    big = jnp.int32(2 * LANES)
    is_g = sub < N_GROUPS
    assert tm == SUB_TILE
    yield

    for part in range(tm // SUB_TILE):
        rs = slice(part * SUB_TILE, (part + 1) * SUB_TILE)
        x = x_tile[rs, :]
        h = (_ln(x) * (1.0 + mod_ref[0, 1:2, :]) + mod_ref[0, 0:1, :]).astype(BF16)

        hm = hf_ref[0, :, rs] + hb_ref[0, :, rs]
        normed = jnp.concatenate(
            [_ln(hm[hd * MLSTM_HEAD_DIM:(hd + 1) * MLSTM_HEAD_DIM, :], axis=0)
             for hd in range(MLSTM_HEADS)], axis=0)
        yc_t = _sigmoid(mvo_t_ref[0, :, rs].astype(F32)) * (normed * normw_ref[...])
        yc = yc_t.T
        yield

        pa = _dot(ya_all[rs, :], wpa_ref[...])
        pb = _dot(yb_ref[0, rs, :], wpb_ref[...])
        pc = _dot(yc.astype(BF16), wpc_ref[...])
        yield
        merged = None
        for n, proj in enumerate((pa, pb, pc)):
            gated = _sigmoid(_dot(h, wg_ref[:, n * d:(n + 1) * d])) * proj
            merged = gated if merged is None else merged + gated
            yield
        yl = _dot(merged.astype(BF16), wo_ref[...])
        yield

        x1 = _ln(alpha * x + mod_ref[0, 2:3, :] * yl) * ln_g_ref[...] + ln_b_ref[...]
        x1_ref[0, rs, :] = x1
        yield
        h2 = _ln(x1) * (1.0 + mod_ref[0, 4:5, :]) + mod_ref[0, 3:4, :]
        for s in range(d // LANES):
            h2_ref[0, pl.ds(part * SUB_TILE * ROW_TILE + s, SUB_TILE, stride=ROW_TILE), :] = (
                h2[:, s * LANES:(s + 1) * LANES])
        yield

        lg = _dot_nt(w_hi, h2.astype(BF16)) + br_ref[...]
        gl = jnp.where(is_g, lg, -jnp.inf)
        g_max = jnp.max(gl, axis=0, keepdims=True)
        g_sel = jnp.min(jnp.where(gl == g_max, sub, big), axis=0, keepdims=True)
        g_p = 1.0 / jnp.sum(jnp.where(is_g, jnp.exp(gl - g_max), 0.0), axis=0, keepdims=True)
        lo = N_GROUPS + EXPERTS_PER_GROUP * g_sel
        el = jnp.where(jnp.logical_and(sub >= lo, sub < lo + EXPERTS_PER_GROUP), lg, -jnp.inf)
        e1 = jnp.max(el, axis=0, keepdims=True)
        i1 = jnp.min(jnp.where(el == e1, sub, big), axis=0, keepdims=True)
        el2 = jnp.where(sub == i1, -jnp.inf, el)
        e2 = jnp.max(el2, axis=0, keepdims=True)
        i2 = jnp.min(jnp.where(el2 == e2, sub, big), axis=0, keepdims=True)
        t = jnp.exp(e2 - e1)
        w1 = g_p / (1.0 + t)
        w2 = w1 * t
        eid1 = (i1 - N_GROUPS).astype(F32)
        eid2 = (i2 - N_GROUPS).astype(F32)
        rows = jnp.concatenate([eid1, eid2, w1, w2, jnp.zeros((4, SUB_TILE), F32)], axis=0)
        wt_ref[0, rs, :] = rows.T
        yield
        eid = jnp.concatenate([eid1, eid2], axis=1)
        sub_e = lax.broadcasted_iota(jnp.int32, (N_EXPERTS, eid.shape[1]), 0).astype(F32)
        onehot = jnp.where(sub_e == eid, 1.0, 0.0)
        earlier = _dot(onehot.astype(BF16), upper_ref[...])
        seen = carry[...]
        rank = jnp.sum(onehot * (earlier + seen), axis=0, keepdims=True)
        assign_ref[0, 0] = jnp.concatenate([eid, rank], axis=0).astype(jnp.int32)
        seen = seen + jnp.sum(onehot, axis=1, keepdims=True)
        carry[...] = seen
        counts_ref[...] = seen


def _merge(streams, mod, a0u, yb, hf_t, hb_t, mvo_t, conv_w, norm_w, w_gate, w_pa, w_pb, w_pc, w_o,
           ln_g, ln_b, w_route, b_route, n_ctx, off, alpha):
    B, _, D = streams[0].shape
    T = a0u.shape[1]
    tm = TOK_TILE
    nct = n_ctx // tm
    nt = T // tm - off
    tn = nt * tm
    hpt = tm // HALO
    nhalo = T // HALO

    ns = PAIR_SAMPLES
    assert B % ns == 0

    def tok(width):
        return pl.BlockSpec((ns, tm, width), lambda b, i: (b, i + off, 0))

    def chan(rows):
        return pl.BlockSpec((ns, rows, tm), lambda b, i: (b, 0, i + off))

    def full(a):
        return pl.BlockSpec(a.shape, lambda b, i: (0,) * a.ndim)

    def otok(width):
        return pl.BlockSpec((ns, tm, width), lambda b, i: (b, i, 0))

    a_i = lax.broadcasted_iota(jnp.int32, (TOP_K * tm, TOP_K * tm), 0)
    b_i = lax.broadcasted_iota(jnp.int32, (TOP_K * tm, TOP_K * tm), 1)
    upper = (a_i < b_i).astype(BF16)
    consts = [conv_w, norm_w, w_gate, w_pa, w_pb, w_pc, w_o, ln_g, ln_b, w_route, b_route, upper]
    return pl.pallas_call(
        functools.partial(_merge_kernel, n_src=len(streams), off=off, n_ctx_tiles=nct, alpha=alpha,
                          n_batch=B),
        grid=(B // ns, nt),
        in_specs=_stream_specs(streams, tm, nct, off, ns) + [
            tok(2 * CONV_DIM),
            pl.BlockSpec((ns, HALO, 2 * CONV_DIM),
                         lambda b, i: (b, jnp.maximum((i + off) * hpt - 1, 0), 0)),
            pl.BlockSpec((ns, HALO, 2 * CONV_DIM),
                         lambda b, i: (b, jnp.minimum((i + off + 1) * hpt, nhalo - 1), 0)),
            tok(ATT_Q_DIM), chan(MLSTM_DIM), chan(MLSTM_DIM),
            pl.BlockSpec((ns, MLSTM_DIM, tm), lambda b, i: (b, 1, i + off)),
            full(mod),
        ] + [full(a) for a in consts],
        out_specs=[otok(D),
                   pl.BlockSpec((ns, tm * ROW_TILE, LANES), lambda b, i: (b, i, 0)),
                   otok(ROW_TILE),
                   pl.BlockSpec((ns, 1, 2, TOP_K * tm), lambda b, i: (b, i, 0, 0)),
                   pl.BlockSpec((N_EXPERTS, 1), lambda b, i: (0, 0))],
        out_shape=[jax.ShapeDtypeStruct((B, tn, D), F32),
                   jax.ShapeDtypeStruct((B, tn * ROW_TILE, LANES), F32),
                   jax.ShapeDtypeStruct((B, tn, ROW_TILE), F32),
                   jax.ShapeDtypeStruct((B, nt, 2, TOP_K * tm), jnp.int32),
                   jax.ShapeDtypeStruct((N_EXPERTS, 1), F32)],
        scratch_shapes=[pltpu.VMEM((N_EXPERTS, 1), F32)],
        compiler_params=_cparams(("arbitrary", "arbitrary")),
        name="merge",
    )(*streams, a0u, a0u, a0u, yb, hf_t, hb_t, mvo_t, mod, *consts)


def _dispatch_kernel(pos_ref, pend_ref, nblk_ref, h_ref, xs_hbm, zbuf, sem, zsem):
    s = pl.program_id(0)
    tm = h_ref.shape[0] // ROW_TILE
    zrows = zbuf.shape[0]
    last_blk = xs_hbm.shape[0] // zrows - 1

    @pl.when(s == 0)
    def _():
        zbuf[...] = jnp.zeros_like(zbuf)

        def zero_copy(start):
            return pltpu.make_async_copy(
                zbuf, xs_hbm.at[pl.ds(pl.multiple_of(start, ROW_TILE), zrows), :], zsem)

        jobs = []
        for e in range(N_EXPERTS):
            before = pend_ref[e - 1] if e > 0 else 0
            jobs.append((pend_ref[e] * ROW_TILE - zrows, pend_ref[e] > before))
        for e in range(N_EXPERTS):
            jobs.append(((nblk_ref[0] + e) * zrows, nblk_ref[0] + e <= last_blk))
        for st, needed in jobs:
            @pl.when(needed)
            def _(st=st):
                zero_copy(st).start()
        for st, needed in jobs:
            @pl.when(needed)
            def _(st=st):
                zero_copy(st).wait()

    def row_copy(r, k):
        src = pl.multiple_of(r * ROW_TILE, ROW_TILE)
        dst = pl.multiple_of(pos_ref[(s * TOP_K + k) * tm + r] * ROW_TILE, ROW_TILE)
        return pltpu.make_async_copy(h_ref.at[pl.ds(src, ROW_TILE), :],
                                     xs_hbm.at[pl.ds(dst, ROW_TILE), :], sem)

    def body(r, carry):
        for k in range(TOP_K):
            row_copy(r, k).start()
        return carry
    lax.fori_loop(0, tm, body, 0, unroll=GATHER_UNROLL)
    for k in range(TOP_K):
        pltpu.make_async_copy(h_ref, xs_hbm.at[pl.ds(0, tm * ROW_TILE), :], sem).wait()


def _dispatch(h2v, pos, pend, nblk, p_rows, tm):
    rows = h2v.shape[0]
    nsteps = rows // (tm * ROW_TILE)
    grid_spec = pltpu.PrefetchScalarGridSpec(
        num_scalar_prefetch=3,
        grid=(nsteps,),
        in_specs=[pl.BlockSpec((tm * ROW_TILE, LANES), lambda s, pos, pend, nbk: (s, 0))],
        out_specs=pl.BlockSpec(memory_space=pl.ANY),
        scratch_shapes=[
            pltpu.VMEM((EXP_BLK * ROW_TILE, LANES), F32),
            pltpu.SemaphoreType.DMA,
            pltpu.SemaphoreType.DMA,
        ],
    )
    return pl.pallas_call(
        _dispatch_kernel,
        grid_spec=grid_spec,
        out_shape=jax.ShapeDtypeStruct((p_rows * ROW_TILE, LANES), F32),
        compiler_params=_cparams(("arbitrary",)),
        name="dispatch",
    )(pos, pend, nblk, h2v)


def _expert_kernel(blke_ref, nblk_ref, xs_ref, wi_ref, wo_ref, y_ref, wi_bf, wo_bf):
    i = pl.program_id(0)
    nb = nblk_ref[0]
    blk = xs_ref.shape[0] // ROW_TILE
    nsl = wi_ref.shape[2] // LANES

    @pl.when(i < nb)
    def _():
        e_now = blke_ref[i]
        e_before = blke_ref[jnp.maximum(i - 1, 0)]

        @pl.when(jnp.logical_or(i == 0, e_now != e_before))
        def _():
            wi_bf[...] = wi_ref[0, 0].astype(BF16)
            wo_bf[...] = wo_ref[0, 0].astype(BF16)

        def rows_part(part):
            r0 = part * EXP_ROWS * ROW_TILE
            xin = jnp.concatenate(
                [xs_ref[pl.ds(r0 + s, EXP_ROWS, stride=ROW_TILE), :] for s in range(nsl)],
                axis=1).astype(BF16)
            yield
            mid = _dot(xin, wi_bf[...])
            yield
            gt = mid[:, 0:D_EXPERT]
            up = mid[:, D_EXPERT:]
            act = ((gt * _sigmoid(gt)) * up).astype(BF16)
            yield
            y = _dot(act, wo_bf[...])
            yield
            for s in range(nsl):
                y_ref[pl.ds(r0 + s, EXP_ROWS, stride=ROW_TILE), :] = y[:, s * LANES:(s + 1) * LANES]

        parts = [rows_part(p) for p in range(blk // EXP_ROWS)]
        for t in range(len(parts) + 4):
            for p, gen in enumerate(parts):
                if 0 <= t - p <= 4:
                    next(gen, None)

    @pl.when(i >= nb)
    def _():
        y_ref[...] = jnp.zeros_like(y_ref)


def _experts(xs, blk_e, nblk, w_ei, w_eo, layer):
    d = w_ei.shape[2]
    blk = EXP_BLK
    nb = xs.shape[0] // (blk * ROW_TILE)
    grid_spec = pltpu.PrefetchScalarGridSpec(
        num_scalar_prefetch=2,
        grid=(nb,),
        in_specs=[
            pl.BlockSpec((blk * ROW_TILE, LANES),
                         lambda i, be, nbk: (jnp.minimum(i, jnp.maximum(nbk[0] - 1, 0)), 0)),
            pl.BlockSpec((1, 1, d, 2 * D_EXPERT), lambda i, be, nbk: (layer, be[i], 0, 0)),
            pl.BlockSpec((1, 1, D_EXPERT, d), lambda i, be, nbk: (layer, be[i], 0, 0)),
        ],
        out_specs=pl.BlockSpec((blk * ROW_TILE, LANES), lambda i, be, nbk: (i, 0)),
        scratch_shapes=[
            pltpu.VMEM((d, 2 * D_EXPERT), BF16),
            pltpu.VMEM((D_EXPERT, d), BF16),
        ],
    )
    return pl.pallas_call(
        _expert_kernel,
        grid_spec=grid_spec,
        out_shape=jax.ShapeDtypeStruct(xs.shape, F32),
        compiler_params=_cparams(("arbitrary",)),
        name="experts",
    )(blk_e, nblk, xs, w_ei, w_eo)


def _combine_kernel(pos_ref, x_ref, mod_ref, wt_ref, ln_g_ref, ln_b_ref, y_hbm, o_ref,
                    ybuf, sem, *, alpha):
    b = pl.program_id(0)
    i = pl.program_id(1)
    nt = pl.num_programs(1)
    tm = x_ref.shape[1]
    step = b * nt + i
    nsteps = pl.num_programs(0) * nt

    def start_gather(s, slot):
        def body(r, carry):
            dst = pl.multiple_of(r * ROW_TILE, ROW_TILE)
            for k in range(TOP_K):
                src = pl.multiple_of(pos_ref[(s * TOP_K + k) * tm + r] * ROW_TILE, ROW_TILE)
                pltpu.make_async_copy(y_hbm.at[pl.ds(src, ROW_TILE), :],
                                      ybuf.at[slot, k, pl.ds(dst, ROW_TILE), :], sem.at[slot]).start()
            return carry
        lax.fori_loop(0, tm, body, 0, unroll=GATHER_UNROLL)

    @pl.when(step == 0)
    def _():
        start_gather(0, 0)

    @pl.when(step + 1 < nsteps)
    def _():
        start_gather(step + 1, (step + 1) % 2)

    slot = step % 2
    for k in range(TOP_K):
        pltpu.make_async_copy(y_hbm.at[pl.ds(0, tm * ROW_TILE), :], ybuf.at[slot, k],
                              sem.at[slot]).wait()
    wt = wt_ref[0]
    w0 = wt[:, 2:3]
    w1 = wt[:, 3:4]
    f = jnp.concatenate(
        [w0 * ybuf[slot, 0, pl.ds(j, tm, stride=ROW_TILE), :]
         + w1 * ybuf[slot, 1, pl.ds(j, tm, stride=ROW_TILE), :]
         for j in range(x_ref.shape[2] // LANES)], axis=1)
    x = x_ref[0]
    o_ref[0] = _ln(alpha * x + mod_ref[0, 5:6, :] * f) * ln_g_ref[...] + ln_b_ref[...]


def _combine(x1, mod, wts, pos, y, ln_g, ln_b, n_ctx_tiles, alpha):
    B, tn, D = x1.shape
    tm = TOK_TILE
    nt = tn // tm
    grid_spec = pltpu.PrefetchScalarGridSpec(
        num_scalar_prefetch=1,
        grid=(B, nt),
        in_specs=[
            pl.BlockSpec((1, tm, D), lambda b, i, pos: (b, i, 0)),
            pl.BlockSpec((1, 6, D), lambda b, i, pos: (jnp.where(i < n_ctx_tiles, B, b), 0, 0)),
            pl.BlockSpec((1, tm, ROW_TILE), lambda b, i, pos: (b, i, 0)),
            pl.BlockSpec((1, D), lambda b, i, pos: (0, 0)),
            pl.BlockSpec((1, D), lambda b, i, pos: (0, 0)),
            pl.BlockSpec(memory_space=pl.ANY),
        ],
        out_specs=pl.BlockSpec((1, tm, D), lambda b, i, pos: (b, i, 0)),
        scratch_shapes=[
            pltpu.VMEM((2, TOP_K, tm * ROW_TILE, LANES), F32),
            pltpu.SemaphoreType.DMA((2,)),
        ],
    )
    return pl.pallas_call(
        functools.partial(_combine_kernel, alpha=alpha),
        grid_spec=grid_spec,
        out_shape=jax.ShapeDtypeStruct((B, tn, D), F32),
        compiler_params=_cparams(("arbitrary", "arbitrary")),
        name="combine",
    )(pos, x1, mod, wts, ln_g, ln_b, y)


def _segments(counts, assign):
    counts = counts.reshape(N_EXPERTS).astype(jnp.int32)
    n_assign = assign.shape[0] * assign.shape[2]
    padded = (counts + EXP_BLK - 1) // EXP_BLK * EXP_BLK
    pad_end = jnp.cumsum(padded)
    pad_start = pad_end - padded
    p_rows = n_assign + N_EXPERTS * EXP_BLK
    nb = p_rows // EXP_BLK
    blk_first = jnp.arange(nb, dtype=jnp.int32) * EXP_BLK
    blk_e = jnp.minimum(jnp.sum((pad_end[None, :] <= blk_first[:, None]).astype(jnp.int32), axis=1),
                        N_EXPERTS - 1).astype(jnp.int32)
    nblk = (pad_end[-1] // EXP_BLK).astype(jnp.int32).reshape(1)
    eid, rank = assign[:, 0, :], assign[:, 1, :]
    onehot = eid[:, :, None] == jnp.arange(N_EXPERTS, dtype=jnp.int32)
    pos = rank + jnp.sum(jnp.where(onehot, pad_start.astype(jnp.int32), 0), axis=-1)
    return pos.reshape(-1).astype(jnp.int32), pad_end.astype(jnp.int32), blk_e, nblk, p_rows


def _rope_tables(n_ctx, n_lat):
    nf = ATT_HEAD_DIM // 4
    inv = ROPE_BASE ** (-jnp.arange(nf, dtype=F32) / nf)
    rows = n_lat // GRID_W
    pos_r = jnp.repeat(jnp.arange(rows, dtype=F32), GRID_W)
    pos_c = jnp.tile(jnp.arange(GRID_W, dtype=F32), rows)
    ang_r = pos_r[:, None] * inv
    ang_c = pos_c[:, None] * inv
    cos_h = jnp.concatenate([jnp.cos(ang_r)] * 2 + [jnp.cos(ang_c)] * 2, axis=-1)
    sin_h = jnp.concatenate([-jnp.sin(ang_r), jnp.sin(ang_r),
                             -jnp.sin(ang_c), jnp.sin(ang_c)], axis=-1)
    reps = LANES // ATT_HEAD_DIM
    cos_l = jnp.tile(cos_h, (1, reps))
    sin_l = jnp.tile(sin_h, (1, reps))
    cos_t = jnp.concatenate([jnp.ones((n_ctx, LANES), F32), cos_l], axis=0)
    sin_t = jnp.concatenate([jnp.zeros((n_ctx, LANES), F32), sin_l], axis=0)
    return cos_t, sin_t


def _projection_weights(w_in_l):
    offs = np.cumsum((0, 3 * CONV_DIM, ATT_Q_DIM, ATT_KV_DIM, ATT_KV_DIM,
                      MLSTM_DIM, MLSTM_DIM, MLSTM_DIM, MLSTM_DIM, N_GATE)).tolist()
    a, q, k, v, mq, mk, mv, mo, g = [w_in_l[:, offs[n]:offs[n + 1]] for n in range(9)]
    hd = ATT_HEAD_DIM

    def swap(w):
        return jnp.concatenate([w[:, hd:], w[:, :hd]], axis=1)

    w_tok = jnp.concatenate([a, q, k, swap(k), v, swap(v), mq, mk], axis=1).astype(BF16)
    w_chan = jnp.concatenate([mv, mo, g], axis=1).T.astype(BF16)
    w_gate = w_in_l[:, offs[9]:].astype(BF16)
    return w_tok, w_chan, w_gate


def kernel(x, c, ctx, c_ctx, w_ada, b_ada, w_in, conv_w, attn_sink, mlstm_gate_b, mlstm_norm_w,
           w_proj_a, w_proj_b, w_proj_c, w_out, ln1_g, ln1_b, w_route_group, b_route_group,
           w_route_expert, b_route_expert, w_expert_in, w_expert_out, ln2_g, ln2_b):
    B, L, D = x.shape
    n_ctx = ctx.shape[1]
    depth = w_ada.shape[0]
    T = n_ctx + L
    alpha = (2 * depth) ** 0.25
    assert D == D_MODEL and n_ctx % TOK_TILE == 0 and L % TOK_TILE == 0 and L % GRID_W == 0
    assert MLSTM_CH == LANES and ATT_QB == LANES
    nct = n_ctx // TOK_TILE

    nrows = -(-(B + 1) // 8) * 8
    cond = jnp.concatenate([c, c_ctx[None, :], jnp.zeros((nrows - B - 1, D), F32)], axis=0)
    mod_all = _ada(cond, w_ada, b_ada).reshape(depth, nrows, 6, D)

    cos_t, sin_t = _rope_tables(n_ctx, L)
    streams = (ctx, x)

    for i in range(depth):
        need_ctx = i < depth - 1
        mod = mod_all[i]
        w_tok, w_chan, w_gate = _projection_weights(w_in[i])

        a0u, q, kv, mqk, mvo_t, gcol, grow = _inproj(streams, mod, w_tok, w_chan, cos_t, sin_t, n_ctx)
        yb = _attention(q, kv, attn_sink[i], n_ctx)
        hf_t, hb_t = _mlstm(mqk, mvo_t, gcol, grow, mlstm_gate_b[i], n_ctx)

        off = 0 if need_ctx else nct
        w_route = jnp.pad(jnp.concatenate([w_route_group[i], w_route_expert[i]], axis=1).T,
                          ((0, LANES - N_ROUTE), (0, 0)))
        b_route = jnp.pad(jnp.concatenate([b_route_group[i], b_route_expert[i]]),
                          (0, LANES - N_ROUTE)).reshape(LANES, 1)
        x1, h2v, wts, assign, counts = _merge(
            streams, mod, a0u, yb, hf_t, hb_t, mvo_t, conv_w[i], mlstm_norm_w[i].reshape(MLSTM_DIM, 1),
            w_gate, w_proj_a[i].astype(BF16), w_proj_b[i].astype(BF16), w_proj_c[i].astype(BF16),
            w_out[i].astype(BF16), ln1_g[i].reshape(1, D), ln1_b[i].reshape(1, D),
            w_route, b_route, n_ctx, off, alpha)

        tn = x1.shape[1]
        pos, pend, blk_e, nblk, p_rows = _segments(counts, assign.reshape(-1, 2, TOP_K * TOK_TILE))
        xs = _dispatch(h2v.reshape(-1, LANES), pos, pend, nblk, p_rows, TOK_TILE)
        y = _experts(xs, blk_e, nblk, w_expert_in, w_expert_out, i)
        streams = (_combine(x1, mod, wts, pos, y,
                            ln2_g[i].reshape(1, D), ln2_b[i].reshape(1, D),
                            nct if need_ctx else 0, alpha),)
    return streams[0]
```

```python
import functools

import jax
import jax.numpy as jnp
import numpy as np
from jax import lax
from jax.experimental import pallas as pl
from jax.experimental.pallas import tpu as pltpu

D_MODEL = 1024
GRID_W = 64
CONV_DIM = 256
ATT_HEADS = 8
ATT_KV_HEADS = 2
ATT_HEAD_DIM = 64
ATT_WINDOW = 128
ROPE_BASE = 10000.0
MLSTM_HEADS = 4
MLSTM_HEAD_DIM = 64
MLSTM_DIM = MLSTM_HEADS * MLSTM_HEAD_DIM
N_GROUPS = 4
EXPERTS_PER_GROUP = 8
N_EXPERTS = N_GROUPS * EXPERTS_PER_GROUP
TOP_K = 2
D_EXPERT = D_MODEL // 2
LN_EPS = 1e-6
NEG_INF = -1e30

ATT_Q_DIM = ATT_HEADS * ATT_HEAD_DIM
ATT_KV_DIM = ATT_KV_HEADS * ATT_HEAD_DIM
N_GATE = 4 * MLSTM_HEADS
MIX_COLS = 3 * CONV_DIM + ATT_Q_DIM + 2 * ATT_KV_DIM + 4 * MLSTM_DIM
N_ROUTE = N_GROUPS + N_EXPERTS

LANES = 128
VMEM_LIMIT = 56 * 1024 * 1024
TOK_TILE = 256
PAIR_SAMPLES = 2
ATT_QB = 128
ATT_STEP = 256
ATT_LAG_SOFTMAX = 1
ATT_LAG_VALUES = 2
MLSTM_CH = 128
MLSTM_STEP = 256
EXP_BLK = 512
EXP_ROWS = 256
HALO = 16
ROW_TILE = 8
GATHER_UNROLL = 8

F32 = jnp.float32
BF16 = jnp.bfloat16


def _cparams(sem):
    return pltpu.CompilerParams(dimension_semantics=sem, vmem_limit_bytes=VMEM_LIMIT)


def _ln(x, axis=-1):
    mu = jnp.mean(x, axis=axis, keepdims=True)
    xc = x - mu
    var = jnp.mean(xc * xc, axis=axis, keepdims=True)
    return xc * lax.rsqrt(var + LN_EPS)


def _sigmoid(x):
    return 0.5 * jnp.tanh(0.5 * x) + 0.5


def _split3(x):
    hi = x.astype(BF16)
    r1 = x - hi.astype(F32)
    mid = r1.astype(BF16)
    lo = (r1 - mid.astype(F32)).astype(BF16)
    return hi, mid, lo


def _dot(a, b):
    return jnp.dot(a, b, preferred_element_type=F32)


def _dot_nt(a, b):
    return lax.dot_general(a, b, (((1,), (1,)), ((), ())), preferred_element_type=F32)


def _dot3(x, rhs_b):
    return sum(_dot(part, rhs_b) for part in _split3(x))


def _dot3_l(lhs_b, x):
    return sum(_dot(lhs_b, part) for part in _split3(x))


def _ada_kernel(c_ref, w_ref, b_ref, o_ref):
    cv = c_ref[...]
    s = cv * _sigmoid(cv)
    o_ref[0] = _dot(s.astype(BF16), w_ref[0].astype(BF16)) + b_ref[0]


def _ada(cond, w_ada, b_ada):
    depth, d, n6 = w_ada.shape
    rows = cond.shape[0]
    nt = n6 // d
    return pl.pallas_call(
        _ada_kernel,
        grid=(depth, nt),
        in_specs=[
            pl.BlockSpec((rows, d), lambda l, j: (0, 0)),
            pl.BlockSpec((1, d, d), lambda l, j: (l, 0, j)),
            pl.BlockSpec((1, 1, d), lambda l, j: (l, 0, j)),
        ],
        out_specs=pl.BlockSpec((1, rows, d), lambda l, j: (l, 0, j)),
        out_shape=jax.ShapeDtypeStruct((depth, rows, n6), F32),
        compiler_params=_cparams(("arbitrary", "arbitrary")),
        name="ada_mod",
    )(cond, w_ada, b_ada.reshape(depth, 1, n6))


def _rope(x, cos, sin_signed, lane):
    swapped = jnp.where(lane % 32 < 16,
                        pltpu.roll(x, LANES - 16, axis=1),
                        pltpu.roll(x, 16, axis=1))
    return x * cos + swapped * sin_signed


_OFF_A = 0
_OFF_Q = _OFF_A + 3 * CONV_DIM
_OFF_K = _OFF_Q + ATT_Q_DIM
_OFF_V = _OFF_K + 2 * ATT_KV_DIM
_OFF_MQK = _OFF_V + 2 * ATT_KV_DIM
_W_TOK_COLS = _OFF_MQK + 2 * MLSTM_DIM


def _stream_specs(streams, tm, nct, off=0, nb=1):
    d = streams[0].shape[2]
    if len(streams) == 1:
        return [pl.BlockSpec((nb, tm, d), lambda b, i, *_: (b, i + off, 0))]
    return [pl.BlockSpec((nb, tm, d), lambda b, i, *_: (b, jnp.minimum(i + off, nct - 1), 0)),
            pl.BlockSpec((nb, tm, d), lambda b, i, *_: (b, jnp.maximum(i + off - nct, 0), 0))]


def _stream_tile(refs, i, nct):
    if len(refs) == 1:
        return refs[0][0]
    return jnp.where(i < nct, refs[0][0], refs[1][0])


def _inproj_kernel(*refs, n_src, nct, n_batch):
    mod_ref = refs[n_src]
    shared = refs[n_src + 1:n_src + 5]
    outs = refs[n_src + 5:]
    is_ctx = pl.program_id(1) < nct
    tiles = []
    for smp in range(PAIR_SAMPLES):
        row = jnp.where(is_ctx, n_batch, pl.program_id(0) * PAIR_SAMPLES + smp)
        tiles.append(_inproj_tile(
            *[r.at[pl.ds(smp, 1)] for r in refs[:n_src]], mod_ref.at[pl.ds(row, 1)], *shared,
            *[r.at[pl.ds(smp, 1)] for r in outs], n_src=n_src, nct=nct))
    live = list(tiles)
    while live:
        for g in list(live):
            if next(g, live) is live:
                live.remove(g)


def _inproj_tile(*refs, n_src, nct):
    (mod_ref, w_ref, wt_ref, cos_ref, sin_ref,
     a0u_ref, q_ref, kv_ref, mqk_ref, mvo_t_ref, gcol_ref, grow_ref) = refs[n_src:]
    x = _stream_tile(refs[:n_src], pl.program_id(1), nct)
    shift = mod_ref[0, 0:1, :]
    scale = mod_ref[0, 1:2, :]
    h = (_ln(x) * (1.0 + scale) + shift).astype(BF16)
    cos = cos_ref[...]
    sin = sin_ref[...]
    lane = lax.broadcasted_iota(jnp.int32, cos.shape, 1)
    yield

    def cols(lo, n):
        return _dot(h, w_ref[:, lo:lo + n])

    za = cols(_OFF_A, 3 * CONV_DIM)
    yield
    a0u_ref[0, :, 0:CONV_DIM] = za[:, 0:CONV_DIM].astype(BF16)
    a0u_ref[0, :, CONV_DIM:2 * CONV_DIM] = (
        za[:, CONV_DIM:2 * CONV_DIM] * za[:, 2 * CONV_DIM:3 * CONV_DIM]).astype(BF16)
    zq = cols(_OFF_Q, ATT_Q_DIM)
    yield
    qscale = ATT_HEAD_DIM ** -0.5
    for j in range(ATT_Q_DIM // LANES):
        piece = _rope(zq[:, j * LANES:(j + 1) * LANES], cos, sin, lane)
        q_ref[0, :, j * LANES:(j + 1) * LANES] = (piece * qscale).astype(BF16)
    zk = cols(_OFF_K, 2 * ATT_KV_DIM)
    yield
    for j in range(2):
        kv_ref[0, :, j * LANES:(j + 1) * LANES] = _rope(
            zk[:, j * LANES:(j + 1) * LANES], cos, sin, lane).astype(BF16)
    kv_ref[0, :, 2 * LANES:] = cols(_OFF_V, 2 * ATT_KV_DIM).astype(BF16)
    yield
    zm = cols(_OFF_MQK, 2 * MLSTM_DIM)
    yield
    mqk_ref[0, :, 0:MLSTM_DIM] = zm[:, 0:MLSTM_DIM].astype(BF16)
    mqk_ref[0, :, MLSTM_DIM:] = (zm[:, MLSTM_DIM:] * (MLSTM_HEAD_DIM ** -0.5)).astype(BF16)
    nchan = mvo_t_ref.shape[1]
    zt = _dot_nt(wt_ref[...], h)
    yield
    mvo_t_ref[0] = zt[0:nchan, :].astype(BF16)
    grow = zt[nchan:, :]
    grow_ref[0] = grow
    gcol_ref[0] = grow.T


def _inproj(streams, mod, w_tok, w_chan, cos_t, sin_t, n_ctx):
    B, _, D = streams[0].shape
    T = sum(s.shape[1] for s in streams) if len(streams) > 1 else streams[0].shape[1]
    tm = TOK_TILE
    nct = n_ctx // tm
    nchan = w_chan.shape[0] - N_GATE
    ns = PAIR_SAMPLES
    assert B % ns == 0

    def tok(width):
        return pl.BlockSpec((ns, tm, width), lambda b, i: (b, i, 0))

    def chan(rows):
        return pl.BlockSpec((ns, rows, tm), lambda b, i: (b, 0, i))

    outs = [(2 * CONV_DIM, BF16), (ATT_Q_DIM, BF16), (4 * ATT_KV_DIM, BF16), (2 * MLSTM_DIM, BF16)]
    return pl.pallas_call(
        functools.partial(_inproj_kernel, n_src=len(streams), nct=nct, n_batch=B),
        grid=(B // ns, T // tm),
        in_specs=_stream_specs(streams, tm, nct, 0, ns) + [
            pl.BlockSpec(mod.shape, lambda b, i: (0, 0, 0)),
            pl.BlockSpec(w_tok.shape, lambda b, i: (0, 0)),
            pl.BlockSpec(w_chan.shape, lambda b, i: (0, 0)),
            pl.BlockSpec((tm, LANES), lambda b, i: (i, 0)),
            pl.BlockSpec((tm, LANES), lambda b, i: (i, 0)),
        ],
        out_specs=[tok(w) for w, _ in outs] + [
            chan(nchan), tok(N_GATE), chan(N_GATE)],
        out_shape=[jax.ShapeDtypeStruct((B, T, w), dt) for w, dt in outs] + [
            jax.ShapeDtypeStruct((B, nchan, T), BF16),
            jax.ShapeDtypeStruct((B, T, N_GATE), F32), jax.ShapeDtypeStruct((B, N_GATE, T), F32)],
        compiler_params=_cparams(("parallel", "arbitrary")),
        name="in_proj",
    )(*streams, mod, w_tok, w_chan, cos_t, sin_t)


def _attn_kernel(sink_ref, q_ref, kvc_ref, kvp_ref, kvm_ref, kvn_ref, o_ref, *, n_ctx_blk, n_blk):
    n_ctx = kvc_ref.shape[1]
    qb = kvp_ref.shape[1]
    nsub = q_ref.shape[1] // qb
    nk = n_ctx + 3 * qb
    half = LANES // 2

    lane_row = lax.broadcasted_iota(jnp.int32, (1, LANES), 1)
    keep = [(lane_row < half).astype(F32).astype(BF16), (lane_row >= half).astype(F32).astype(BF16)]
    pad_v = jnp.zeros((qb, LANES), BF16)
    qi = lax.broadcasted_iota(jnp.int32, (qb, qb), 0)
    ki = lax.broadcasted_iota(jnp.int32, (qb, qb), 1)
    lane_q = lax.broadcasted_iota(jnp.int32, (qb, LANES), 1)
    neg = jnp.full((qb, LANES), NEG_INF, F32)
    group = ATT_HEADS // ATT_KV_HEADS
    blocks = []
    for blk_i in range(q_ref.shape[0] * nsub):
        smp, sb = divmod(blk_i, nsub)
        around = ([kvp_ref[smp]] + [kvm_ref[smp, n * qb:(n + 1) * qb, :] for n in range(nsub)]
                  + [kvn_ref[smp]])
        g = pl.program_id(1) * nsub + sb
        is_lat = g >= n_ctx_blk
        has_prev = g >= n_ctx_blk + 1
        has_next = g < n_blk - 1
        kv_all = jnp.concatenate([kvc_ref[smp]] + around[sb:sb + 3], axis=0)
        k_ext, v_ext = {}, {}
        for tile in range(2):
            for par in range(2):
                k_ext[tile, par] = kv_all[:, tile * LANES:(tile + 1) * LANES] * keep[par]
                v = jnp.concatenate([kv_all[:, (2 + tile) * LANES:(3 + tile) * LANES], pad_v], axis=0)
                v_ext[tile, par] = jnp.concatenate(
                    [v * keep[par], jnp.broadcast_to(keep[par], v.shape)], axis=1)
        blocks.append(dict(
            is_lat=is_lat, ok_prev=jnp.logical_and(ki >= qi, has_prev),
            ok_next=jnp.logical_and(ki <= qi, jnp.logical_and(has_next, is_lat)),
            k_ext=k_ext, v_ext=v_ext, q=q_ref[smp, sb * qb:(sb + 1) * qb, :]))

    jobs = [(sb, pair, par) for sb in range(len(blocks)) for pair in range(ATT_HEADS // 2)
            for par in range(2)]
    scores, probs, accs, outs = {}, {}, {}, {}

    def operands(job):
        sb, pair, par = job
        kvh = (2 * pair) // group
        tile = 0 if kvh == par else 1
        return blocks[sb], tile

    def stage_scores(job):
        sb, pair, par = job
        blk, tile = operands(job)
        scores[job] = _dot_nt(blk["q"][:, pair * LANES:(pair + 1) * LANES], blk["k_ext"][tile, par])

    def stage_softmax(job):
        sb, pair, par = job
        blk, _ = operands(job)
        s = scores.pop(job)
        s_ext = jnp.concatenate([
            s[:, 0:n_ctx],
            jnp.where(blk["ok_prev"], s[:, n_ctx:n_ctx + qb], neg),
            jnp.where(blk["is_lat"], s[:, n_ctx + qb:n_ctx + 2 * qb], neg),
            jnp.where(blk["ok_next"], s[:, n_ctx + 2 * qb:nk], neg),
            jnp.where(lane_q == 0, sink_ref[2 * pair + par], neg)], axis=1)
        m = jnp.max(s_ext, axis=-1, keepdims=True)
        probs[job] = jnp.exp(s_ext - m).astype(BF16)

    def stage_values(job):
        sb, pair, par = job
        blk, tile = operands(job)
        part = _dot(probs.pop(job), blk["v_ext"][tile, par])
        if par == 0:
            accs[sb, pair] = part
        else:
            acc = accs.pop((sb, pair)) + part
            outs[sb, pair] = acc[:, 0:LANES] / acc[:, LANES:]

    for t in range(len(jobs) + ATT_LAG_VALUES):
        if t < len(jobs):
            stage_scores(jobs[t])
        if 0 <= t - ATT_LAG_SOFTMAX < len(jobs):
            stage_softmax(jobs[t - ATT_LAG_SOFTMAX])
        if 0 <= t - ATT_LAG_VALUES < len(jobs):
            stage_values(jobs[t - ATT_LAG_VALUES])
    for blk_i in range(len(blocks)):
        smp, sb = divmod(blk_i, nsub)
        o_ref[smp, sb * qb:(sb + 1) * qb, :] = jnp.concatenate(
            [outs[blk_i, pair] for pair in range(ATT_HEADS // 2)], axis=-1).astype(o_ref.dtype)


def _attention(q, kv, sink, n_ctx):
    B, T, _ = q.shape
    qb = ATT_QB
    nsub = ATT_STEP // qb
    n_blk = T // qb
    nq = T // ATT_STEP
    n_ctx_blk = n_ctx // qb
    kvw = kv.shape[2]
    ns = PAIR_SAMPLES
    assert B % ns == 0
    grid_spec = pltpu.PrefetchScalarGridSpec(
        num_scalar_prefetch=1,
        grid=(B // ns, nq),
        in_specs=[
            pl.BlockSpec((ns, ATT_STEP, ATT_Q_DIM), lambda b, j, sk: (b, j, 0)),
            pl.BlockSpec((ns, n_ctx, kvw), lambda b, j, sk: (b, 0, 0)),
            pl.BlockSpec((ns, qb, kvw), lambda b, j, sk: (b, jnp.maximum(j * nsub - 1, 0), 0)),
            pl.BlockSpec((ns, ATT_STEP, kvw), lambda b, j, sk: (b, j, 0)),
            pl.BlockSpec((ns, qb, kvw), lambda b, j, sk: (b, jnp.minimum((j + 1) * nsub, n_blk - 1), 0)),
        ],
        out_specs=pl.BlockSpec((ns, ATT_STEP, ATT_Q_DIM), lambda b, j, sk: (b, j, 0)),
    )
    return pl.pallas_call(
        functools.partial(_attn_kernel, n_ctx_blk=n_ctx_blk, n_blk=n_blk),
        grid_spec=grid_spec,
        out_shape=jax.ShapeDtypeStruct((B, T, ATT_Q_DIM), BF16),
        compiler_params=_cparams(("parallel", "arbitrary")),
        name="window_attn",
    )(sink, q, kv, kv, kv, kv)


def _log_sigmoid(x):
    return jnp.minimum(x, 0.0) - jnp.log1p(jnp.exp(-jnp.abs(x)))


def _rows_to_lanes(a, base):
    return jnp.concatenate([a[base + h:base + h + 1, :] for h in range(MLSTM_HEADS)], axis=1)


def _mlstm_gates(d, gcol, grow):
    ch = gcol.shape[0]
    nh = MLSTM_HEADS
    wide = nh * ch
    fwd = d == 0

    r_i = lax.broadcasted_iota(jnp.int32, (ch, ch), 0)
    c_i = lax.broadcasted_iota(jnp.int32, (ch, ch), 1)
    seen_t = (r_i <= c_i) if fwd else (r_i >= c_i)
    seen_tt = (c_i <= r_i) if fwd else (c_i >= r_i)
    r_w = lax.broadcasted_iota(jnp.int32, (ch, wide), 0)
    s_w = lax.broadcasted_iota(jnp.int32, (ch, wide), 1) % ch
    seen_w = (r_w <= s_w) if fwd else (r_w >= s_w)

    base_i = 2 * d * nh
    base_f = base_i + nh

    lf_col = _log_sigmoid(gcol)
    cum_col = _dot3_l(seen_tt.astype(BF16), lf_col)
    lane16 = lax.broadcasted_iota(jnp.int32, gcol.shape, 1)
    z = jnp.where(jnp.logical_and(lane16 >= base_i, lane16 < base_f), gcol, -cum_col)
    ch16 = lax.broadcasted_iota(jnp.int32, (N_GATE, wide), 0)
    hd16 = lax.broadcasted_iota(jnp.int32, (N_GATE, wide), 1) // ch
    sel = jnp.logical_or(ch16 == base_i + hd16, ch16 == base_f + hd16).astype(BF16)
    x_t = _dot3(z, sel)

    lf_row = _log_sigmoid(grow)
    rhs2 = jnp.concatenate([seen_t.astype(BF16), jnp.ones((ch, ch), BF16)], axis=1)
    rows = _dot3(lf_row, rhs2)
    b_all = _rows_to_lanes(rows[:, 0:ch], base_f)
    g_all = _rows_to_lanes(rows[:, ch:], base_f)
    li_all = _rows_to_lanes(grow, base_i)
    yield
    dmat = jnp.where(seen_w, x_t + b_all, -jnp.inf)
    return (dmat, jnp.max(dmat, axis=0, keepdims=True), jnp.max(x_t, axis=0, keepdims=True),
            b_all, g_all, li_all)


def _mlstm_dir(qk, v_t, gates, ct_bd, n_bd, m_prev):
    dmat, dmat_max, x_max, b_all, g_all, li_all = gates
    ch = qk.shape[0]
    nh = MLSTM_HEADS
    dh = MLSTM_HEAD_DIM
    a = b_all + m_prev
    m = jnp.maximum(a, dmat_max)
    w_intra = jnp.exp(dmat - m)
    e_inter = jnp.exp(a - m)
    yield

    q = qk[:, 0:MLSTM_DIM]
    k = qk[:, MLSTM_DIM:]
    lb = lax.broadcasted_iota(jnp.int32, (1, MLSTM_DIM), 1) // dh
    q_bd = jnp.concatenate([q * (lb == h).astype(F32).astype(BF16) for h in range(nh)], axis=0)
    s_t = _dot_nt(k, q_bd) * w_intra
    nq = _rows_to_lanes(_dot_nt(n_bd.astype(BF16), q), 0)
    den = jnp.sum(s_t, axis=0, keepdims=True) + e_inter * nq
    inv = 1.0 / jnp.maximum(jnp.abs(den), jnp.exp(-m))
    inter_t = _dot_nt(ct_bd.astype(BF16), q)
    s_b = s_t.astype(BF16)
    yield
    outs = []
    for h in range(nh):
        seg = slice(h * ch, (h + 1) * ch)
        blk = slice(h * dh, (h + 1) * dh)
        num = _dot(v_t[blk, :], s_b[:, seg]) + e_inter[:, seg] * inter_t[blk, :]
        outs.append(num * inv[:, seg])
    h_t = jnp.concatenate(outs, axis=0)
    yield

    m_loc = g_all + x_max
    m_new = jnp.maximum(g_all + m_prev, m_loc)
    sa = jnp.exp(g_all + m_prev - m_new)
    sb = jnp.exp(m_loc - m_new)
    e_loc = jnp.exp(g_all - b_all + li_all - m_loc)
    v_e = jnp.concatenate(
        [v_t[h * dh:(h + 1) * dh, :].astype(F32) * e_loc[:, h * ch:(h + 1) * ch] for h in range(nh)],
        axis=0).astype(BF16)
    ct_loc = _dot(v_e, k)
    yield
    reps = MLSTM_DIM // ch

    def per_head_rows(row, nrows):
        return jnp.concatenate(
            [jnp.broadcast_to(jnp.concatenate([row[:, h * ch:(h + 1) * ch]] * reps, axis=1),
                              (nrows, MLSTM_DIM)) for h in range(nh)], axis=0)

    eb = lax.broadcasted_iota(jnp.int32, (MLSTM_DIM, MLSTM_DIM), 0) // dh
    db = lax.broadcasted_iota(jnp.int32, (MLSTM_DIM, MLSTM_DIM), 1) // dh
    ct_new = jnp.where(eb == db, per_head_rows(sa, dh) * ct_bd + per_head_rows(sb, dh) * ct_loc, 0.0)
    nrow = n_bd.shape[0]
    e_rows = jnp.concatenate([e_loc[:, h * ch:(h + 1) * ch] for h in range(nh)]
                             + [jnp.zeros((nrow - nh, ch), F32)], axis=0).astype(BF16)
    n_loc = _dot(e_rows, k)
    pad = jnp.zeros((nrow - nh, MLSTM_DIM), F32)
    sa8 = jnp.concatenate([per_head_rows(sa, 1), pad], axis=0)
    sb8 = jnp.concatenate([per_head_rows(sb, 1), pad], axis=0)
    hb8 = lax.broadcasted_iota(jnp.int32, (nrow, MLSTM_DIM), 0)
    db8 = lax.broadcasted_iota(jnp.int32, (nrow, MLSTM_DIM), 1) // dh
    n_new = jnp.where(hb8 == db8, sa8 * n_bd + sb8 * n_loc, 0.0)
    return h_t, ct_new, n_new, m_new


def _mlstm_kernel(qk_f_ref, qk_b_ref, vt_f_ref, vt_b_ref, gc_f_ref, gc_b_ref, gr_f_ref, gr_b_ref,
                  bcol_ref, brow_ref, hf_ref, hb_ref, c_scr, n_scr, m_scr):
    j = pl.program_id(1)

    @pl.when(j == 0)
    def _():
        c_scr[...] = jnp.zeros_like(c_scr)
        n_scr[...] = jnp.zeros_like(n_scr)
        m_scr[...] = jnp.zeros_like(m_scr)

    nscan = c_scr.shape[0]
    states = [(c_scr[k], n_scr[k], m_scr[k]) for k in range(nscan)]
    dirs = ((qk_f_ref, vt_f_ref, gc_f_ref, gr_f_ref, hf_ref),
            (qk_b_ref, vt_b_ref, gc_b_ref, gr_b_ref, hb_ref))
    ch = MLSTM_CH
    nsub = qk_f_ref.shape[1] // ch

    def scan(k):
        d = k % 2
        qk_ref, vt_ref, gc_ref, gr_ref, out_ref = [r.at[pl.ds(k // 2, 1)] for r in dirs[d]]
        st = states[k]
        order = [slice(sc * ch, (sc + 1) * ch)
                 for sc in (range(nsub) if d == 0 else reversed(range(nsub)))]
        gates = []
        for ts in order:
            gates.append((yield from _mlstm_gates(
                d, gc_ref[0, ts, :] + bcol_ref[...], gr_ref[0, :, ts] + brow_ref[...])))
        for ts, gt in zip(order, gates):
            h_t, *st = yield from _mlstm_dir(qk_ref[0, ts, :], vt_ref[0, :, ts], gt, *st)
            out_ref[0, :, ts] = h_t
        states[k] = st

    live = [scan(k) for k in range(nscan)]
    while live:
        for g in list(live):
            if next(g, live) is live:
                live.remove(g)
    for k in range(nscan):
        c_scr[k] = states[k][0]
        n_scr[k] = states[k][1]
        m_scr[k] = states[k][2]


def _mlstm(mqk, mvo_t, gcol, grow, gate_b, n_ctx):
    B, T, _ = mqk.shape
    ch = MLSTM_STEP
    nc = T // ch
    ncc = n_ctx // ch
    ns = PAIR_SAMPLES
    assert B % ns == 0

    def rev(j):
        return jnp.where(j < ncc, ncc - 1 - j, nc - 1 - (j - ncc))

    def tok(width, order):
        return pl.BlockSpec((ns, ch, width), lambda b, j: (b, order(j), 0))

    def chan(rows, order):
        return pl.BlockSpec((ns, rows, ch), lambda b, j: (b, 0, order(j)))

    ident = lambda j: j
    return pl.pallas_call(
        _mlstm_kernel,
        grid=(B // ns, nc),
        in_specs=[
            tok(2 * MLSTM_DIM, ident), tok(2 * MLSTM_DIM, rev),
            chan(MLSTM_DIM, ident), chan(MLSTM_DIM, rev),
            tok(N_GATE, ident), tok(N_GATE, rev),
            chan(N_GATE, ident), chan(N_GATE, rev),
            pl.BlockSpec((1, N_GATE), lambda b, j: (0, 0)),
            pl.BlockSpec((N_GATE, 1), lambda b, j: (0, 0)),
        ],
        out_specs=[chan(MLSTM_DIM, ident), chan(MLSTM_DIM, rev)],
        out_shape=[jax.ShapeDtypeStruct((B, MLSTM_DIM, T), F32)] * 2,
        scratch_shapes=[
            pltpu.VMEM((2 * ns, MLSTM_DIM, MLSTM_DIM), F32),
            pltpu.VMEM((2 * ns, 8, MLSTM_DIM), F32),
            pltpu.VMEM((2 * ns, 1, MLSTM_HEADS * MLSTM_CH), F32),
        ],
        compiler_params=_cparams(("parallel", "arbitrary")),
        name="mlstm",
    )(mqk, mqk, mvo_t, mvo_t, gcol, gcol, grow, grow,
      gate_b.reshape(1, N_GATE), gate_b.reshape(N_GATE, 1))


def _merge_kernel(*refs, n_src, off, n_ctx_tiles, alpha, n_batch):
    n_in = n_src + 7
    mod_ref = refs[n_in]
    consts = refs[n_in + 1:n_in + 13]
    outs = refs[n_in + 13:n_in + 17]
    counts_ref, carry = refs[n_in + 17], refs[n_in + 18]

    @pl.when(jnp.logical_and(pl.program_id(0) == 0, pl.program_id(1) == 0))
    def _():
        carry[...] = jnp.zeros_like(carry)

    is_ctx = pl.program_id(1) + off < n_ctx_tiles
    tiles = []
    for smp in range(PAIR_SAMPLES):
        row = jnp.where(is_ctx, n_batch, pl.program_id(0) * PAIR_SAMPLES + smp)
        tiles.append(_merge_tile(
            *[r.at[pl.ds(smp, 1)] for r in refs[:n_in]], mod_ref.at[pl.ds(row, 1)], *consts,
            *[r.at[pl.ds(smp, 1)] for r in outs], counts_ref, carry,
            n_src=n_src, off=off, n_ctx_tiles=n_ctx_tiles, alpha=alpha))
    live = list(tiles)
    while live:
        for g in list(live):
            if next(g, live) is live:
                live.remove(g)


def _merge_tile(*refs, n_src, off, n_ctx_tiles, alpha):
    (a0u_ref, up_ref, un_ref, yb_ref, hf_ref, hb_ref, mvo_t_ref, mod_ref,
     convw_ref, normw_ref, wg_ref, wpa_ref, wpb_ref, wpc_ref, wo_ref,
     ln_g_ref, ln_b_ref, wr_ref, br_ref, upper_ref,
     x1_ref, h2_ref, wt_ref, assign_ref, counts_ref, carry) = refs[n_src:]
    i = pl.program_id(1) + off
    nt = pl.num_programs(1) + off
    x_tile = _stream_tile(refs[:n_src], i, n_ctx_tiles)
    tm = x_tile.shape[0]
    d = x_tile.shape[1]

    a0 = a0u_ref[0, :, 0:CONV_DIM].astype(F32)
    u = a0u_ref[0, :, CONV_DIM:].astype(F32)
    prev_ok = jnp.logical_and(i != 0, i != n_ctx_tiles)
    next_ok = jnp.logical_and(i != n_ctx_tiles - 1, i != nt - 1)
    u_prev = jnp.where(prev_ok, up_ref[0, HALO - 1:HALO, CONV_DIM:].astype(F32), 0.0)
    u_next = jnp.where(next_ok, un_ref[0, 0:1, CONV_DIM:].astype(F32), 0.0)
    row = lax.broadcasted_iota(jnp.int32, u.shape, 0)
    u_dn = jnp.where(row == 0, u_prev, pltpu.roll(u, 1, axis=0))
    u_up = jnp.where(row == tm - 1, u_next, pltpu.roll(u, tm - 1, axis=0))
    cw = convw_ref[...]
    ya_all = (a0 * (u_dn * cw[0:1, :] + u * cw[1:2, :] + u_up * cw[2:3, :])).astype(BF16)

    w_hi = wr_ref[...].astype(BF16)
    sub = lax.broadcasted_iota(jnp.int32, (w_hi.shape[0], tm), 0)
    big = jnp.int32(2 * LANES)
    is_g = sub < N_GROUPS
    yield

    x = x_tile
    h = (_ln(x) * (1.0 + mod_ref[0, 1:2, :]) + mod_ref[0, 0:1, :]).astype(BF16)

    hm = hf_ref[0] + hb_ref[0]
    normed = jnp.concatenate(
        [_ln(hm[hd * MLSTM_HEAD_DIM:(hd + 1) * MLSTM_HEAD_DIM, :], axis=0)
         for hd in range(MLSTM_HEADS)], axis=0)
    yc_t = _sigmoid(mvo_t_ref[0].astype(F32)) * (normed * normw_ref[...])
    yc = yc_t.T
    yield

    pa = _dot(ya_all, wpa_ref[...])
    pb = _dot(yb_ref[0], wpb_ref[...])
    pc = _dot(yc.astype(BF16), wpc_ref[...])
    yield
    merged = None
    for n, proj in enumerate((pa, pb, pc)):
        gated = _sigmoid(_dot(h, wg_ref[:, n * d:(n + 1) * d])) * proj
        merged = gated if merged is None else merged + gated
        yield
    yl = _dot(merged.astype(BF16), wo_ref[...])
    yield

    x1 = _ln(alpha * x + mod_ref[0, 2:3, :] * yl) * ln_g_ref[...] + ln_b_ref[...]
    x1_ref[0] = x1
    yield
    h2 = _ln(x1) * (1.0 + mod_ref[0, 4:5, :]) + mod_ref[0, 3:4, :]
    for s in range(d // LANES):
        h2_ref[0, pl.ds(s, tm, stride=ROW_TILE), :] = h2[:, s * LANES:(s + 1) * LANES]
    yield

    lg = _dot_nt(w_hi, h2.astype(BF16)) + br_ref[...]
    gl = jnp.where(is_g, lg, -jnp.inf)
    g_max = jnp.max(gl, axis=0, keepdims=True)
    g_sel = jnp.min(jnp.where(gl == g_max, sub, big), axis=0, keepdims=True)
    g_p = 1.0 / jnp.sum(jnp.where(is_g, jnp.exp(gl - g_max), 0.0), axis=0, keepdims=True)
    lo = N_GROUPS + EXPERTS_PER_GROUP * g_sel
    el = jnp.where(jnp.logical_and(sub >= lo, sub < lo + EXPERTS_PER_GROUP), lg, -jnp.inf)
    e1 = jnp.max(el, axis=0, keepdims=True)
    i1 = jnp.min(jnp.where(el == e1, sub, big), axis=0, keepdims=True)
    el2 = jnp.where(sub == i1, -jnp.inf, el)
    e2 = jnp.max(el2, axis=0, keepdims=True)
    i2 = jnp.min(jnp.where(el2 == e2, sub, big), axis=0, keepdims=True)
    t = jnp.exp(e2 - e1)
    w1 = g_p / (1.0 + t)
    w2 = w1 * t
    eid1 = (i1 - N_GROUPS).astype(F32)
    eid2 = (i2 - N_GROUPS).astype(F32)
    rows = jnp.concatenate([eid1, eid2, w1, w2, jnp.zeros((4, tm), F32)], axis=0)
    wt_ref[0] = rows.T
    yield
    eid = jnp.concatenate([eid1, eid2], axis=1)
    sub_e = lax.broadcasted_iota(jnp.int32, (N_EXPERTS, eid.shape[1]), 0).astype(F32)
    onehot = jnp.where(sub_e == eid, 1.0, 0.0)
    earlier = _dot(onehot.astype(BF16), upper_ref[...])
    seen = carry[...]
    rank = jnp.sum(onehot * (earlier + seen), axis=0, keepdims=True)
    assign_ref[0, 0] = jnp.concatenate([eid, rank], axis=0).astype(jnp.int32)
    seen = seen + jnp.sum(onehot, axis=1, keepdims=True)
    carry[...] = seen
    counts_ref[...] = seen


def _merge(streams, mod, a0u, yb, hf_t, hb_t, mvo_t, conv_w, norm_w, w_gate, w_pa, w_pb, w_pc, w_o,
           ln_g, ln_b, w_route, b_route, n_ctx, off, alpha):
    B, _, D = streams[0].shape
    T = a0u.shape[1]
    tm = TOK_TILE
    nct = n_ctx // tm
    nt = T // tm - off
    tn = nt * tm
    hpt = tm // HALO
    nhalo = T // HALO

    ns = PAIR_SAMPLES
    assert B % ns == 0

    def tok(width):
        return pl.BlockSpec((ns, tm, width), lambda b, i: (b, i + off, 0))

    def chan(rows):
        return pl.BlockSpec((ns, rows, tm), lambda b, i: (b, 0, i + off))

    def full(a):
        return pl.BlockSpec(a.shape, lambda b, i: (0,) * a.ndim)

    def otok(width):
        return pl.BlockSpec((ns, tm, width), lambda b, i: (b, i, 0))

    a_i = lax.broadcasted_iota(jnp.int32, (TOP_K * tm, TOP_K * tm), 0)
    b_i = lax.broadcasted_iota(jnp.int32, (TOP_K * tm, TOP_K * tm), 1)
    upper = (a_i < b_i).astype(BF16)
    consts = [conv_w, norm_w, w_gate, w_pa, w_pb, w_pc, w_o, ln_g, ln_b, w_route, b_route, upper]
    return pl.pallas_call(
        functools.partial(_merge_kernel, n_src=len(streams), off=off, n_ctx_tiles=nct, alpha=alpha,
                          n_batch=B),
        grid=(B // ns, nt),
        in_specs=_stream_specs(streams, tm, nct, off, ns) + [
            tok(2 * CONV_DIM),
            pl.BlockSpec((ns, HALO, 2 * CONV_DIM),
                         lambda b, i: (b, jnp.maximum((i + off) * hpt - 1, 0), 0)),
            pl.BlockSpec((ns, HALO, 2 * CONV_DIM),
                         lambda b, i: (b, jnp.minimum((i + off + 1) * hpt, nhalo - 1), 0)),
            tok(ATT_Q_DIM), chan(MLSTM_DIM), chan(MLSTM_DIM),
            pl.BlockSpec((ns, MLSTM_DIM, tm), lambda b, i: (b, 1, i + off)),
            full(mod),
        ] + [full(a) for a in consts],
        out_specs=[otok(D),
                   pl.BlockSpec((ns, tm * ROW_TILE, LANES), lambda b, i: (b, i, 0)),
                   otok(ROW_TILE),
                   pl.BlockSpec((ns, 1, 2, TOP_K * tm), lambda b, i: (b, i, 0, 0)),
                   pl.BlockSpec((N_EXPERTS, 1), lambda b, i: (0, 0))],
        out_shape=[jax.ShapeDtypeStruct((B, tn, D), F32),
                   jax.ShapeDtypeStruct((B, tn * ROW_TILE, LANES), F32),
                   jax.ShapeDtypeStruct((B, tn, ROW_TILE), F32),
                   jax.ShapeDtypeStruct((B, nt, 2, TOP_K * tm), jnp.int32),
                   jax.ShapeDtypeStruct((N_EXPERTS, 1), F32)],
        scratch_shapes=[pltpu.VMEM((N_EXPERTS, 1), F32)],
        compiler_params=_cparams(("arbitrary", "arbitrary")),
        name="merge",
    )(*streams, a0u, a0u, a0u, yb, hf_t, hb_t, mvo_t, mod, *consts)


def _dispatch_kernel(pos_ref, pend_ref, nblk_ref, h_ref, xs_hbm, zbuf, sem, zsem):
    s = pl.program_id(0)
    tm = h_ref.shape[0] // ROW_TILE
    zrows = zbuf.shape[0]
    last_blk = xs_hbm.shape[0] // zrows - 1

    @pl.when(s == 0)
    def _():
        zbuf[...] = jnp.zeros_like(zbuf)

        def zero_copy(start):
            return pltpu.make_async_copy(
                zbuf, xs_hbm.at[pl.ds(pl.multiple_of(start, ROW_TILE), zrows), :], zsem)

        jobs = []
        for e in range(N_EXPERTS):
            before = pend_ref[e - 1] if e > 0 else 0
            jobs.append((pend_ref[e] * ROW_TILE - zrows, pend_ref[e] > before))
        for e in range(N_EXPERTS):
            jobs.append(((nblk_ref[0] + e) * zrows, nblk_ref[0] + e <= last_blk))
        for st, needed in jobs:
            @pl.when(needed)
            def _(st=st):
                zero_copy(st).start()
        for st, needed in jobs:
            @pl.when(needed)
            def _(st=st):
                zero_copy(st).wait()

    def row_copy(r, k):
        src = pl.multiple_of(r * ROW_TILE, ROW_TILE)
        dst = pl.multiple_of(pos_ref[(s * TOP_K + k) * tm + r] * ROW_TILE, ROW_TILE)
        return pltpu.make_async_copy(h_ref.at[pl.ds(src, ROW_TILE), :],
                                     xs_hbm.at[pl.ds(dst, ROW_TILE), :], sem)

    def body(r, carry):
        for k in range(TOP_K):
            row_copy(r, k).start()
        return carry
    lax.fori_loop(0, tm, body, 0, unroll=GATHER_UNROLL)
    for k in range(TOP_K):
        pltpu.make_async_copy(h_ref, xs_hbm.at[pl.ds(0, tm * ROW_TILE), :], sem).wait()


def _dispatch(h2v, pos, pend, nblk, p_rows, tm):
    rows = h2v.shape[0]
    nsteps = rows // (tm * ROW_TILE)
    grid_spec = pltpu.PrefetchScalarGridSpec(
        num_scalar_prefetch=3,
        grid=(nsteps,),
        in_specs=[pl.BlockSpec((tm * ROW_TILE, LANES), lambda s, pos, pend, nbk: (s, 0))],
        out_specs=pl.BlockSpec(memory_space=pl.ANY),
        scratch_shapes=[
            pltpu.VMEM((EXP_BLK * ROW_TILE, LANES), F32),
            pltpu.SemaphoreType.DMA,
            pltpu.SemaphoreType.DMA,
        ],
    )
    return pl.pallas_call(
        _dispatch_kernel,
        grid_spec=grid_spec,
        out_shape=jax.ShapeDtypeStruct((p_rows * ROW_TILE, LANES), F32),
        compiler_params=_cparams(("arbitrary",)),
        name="dispatch",
    )(pos, pend, nblk, h2v)


def _expert_kernel(blke_ref, nblk_ref, xs_ref, wi_ref, wo_ref, y_ref, wi_bf, wo_bf):
    i = pl.program_id(0)
    nb = nblk_ref[0]
    blk = xs_ref.shape[0] // ROW_TILE
    nsl = wi_ref.shape[2] // LANES

    @pl.when(i < nb)
    def _():
        e_now = blke_ref[i]
        e_before = blke_ref[jnp.maximum(i - 1, 0)]

        @pl.when(jnp.logical_or(i == 0, e_now != e_before))
        def _():
            wi_bf[...] = wi_ref[0, 0].astype(BF16)
            wo_bf[...] = wo_ref[0, 0].astype(BF16)

        def rows_part(part):
            r0 = part * EXP_ROWS * ROW_TILE
            xin = jnp.concatenate(
                [xs_ref[pl.ds(r0 + s, EXP_ROWS, stride=ROW_TILE), :] for s in range(nsl)],
                axis=1).astype(BF16)
            yield
            mid = _dot(xin, wi_bf[...])
            yield
            gt = mid[:, 0:D_EXPERT]
            up = mid[:, D_EXPERT:]
            act = ((gt * _sigmoid(gt)) * up).astype(BF16)
            yield
            y = _dot(act, wo_bf[...])
            yield
            for s in range(nsl):
                y_ref[pl.ds(r0 + s, EXP_ROWS, stride=ROW_TILE), :] = y[:, s * LANES:(s + 1) * LANES]

        parts = [rows_part(p) for p in range(blk // EXP_ROWS)]
        for t in range(len(parts) + 4):
            for p, gen in enumerate(parts):
                if 0 <= t - p <= 4:
                    next(gen, None)

    @pl.when(i >= nb)
    def _():
        y_ref[...] = jnp.zeros_like(y_ref)


def _experts(xs, blk_e, nblk, w_ei, w_eo, layer):
    d = w_ei.shape[2]
    blk = EXP_BLK
    nb = xs.shape[0] // (blk * ROW_TILE)
    grid_spec = pltpu.PrefetchScalarGridSpec(
        num_scalar_prefetch=2,
        grid=(nb,),
        in_specs=[
            pl.BlockSpec((blk * ROW_TILE, LANES),
                         lambda i, be, nbk: (jnp.minimum(i, jnp.maximum(nbk[0] - 1, 0)), 0)),
            pl.BlockSpec((1, 1, d, 2 * D_EXPERT), lambda i, be, nbk: (layer, be[i], 0, 0)),
            pl.BlockSpec((1, 1, D_EXPERT, d), lambda i, be, nbk: (layer, be[i], 0, 0)),
        ],
        out_specs=pl.BlockSpec((blk * ROW_TILE, LANES), lambda i, be, nbk: (i, 0)),
        scratch_shapes=[
            pltpu.VMEM((d, 2 * D_EXPERT), BF16),
            pltpu.VMEM((D_EXPERT, d), BF16),
        ],
    )
    return pl.pallas_call(
        _expert_kernel,
        grid_spec=grid_spec,
        out_shape=jax.ShapeDtypeStruct(xs.shape, F32),
        compiler_params=_cparams(("arbitrary",)),
        name="experts",
    )(blk_e, nblk, xs, w_ei, w_eo)


def _combine_kernel(pos_ref, x_ref, mod_ref, wt_ref, ln_g_ref, ln_b_ref, y_hbm, o_ref,
                    ybuf, sem, *, alpha):
    b = pl.program_id(0)
    i = pl.program_id(1)
    nt = pl.num_programs(1)
    tm = x_ref.shape[1]
    step = b * nt + i
    nsteps = pl.num_programs(0) * nt

    def start_gather(s, slot):
        def body(r, carry):
            dst = pl.multiple_of(r * ROW_TILE, ROW_TILE)
            for k in range(TOP_K):
                src = pl.multiple_of(pos_ref[(s * TOP_K + k) * tm + r] * ROW_TILE, ROW_TILE)
                pltpu.make_async_copy(y_hbm.at[pl.ds(src, ROW_TILE), :],
                                      ybuf.at[slot, k, pl.ds(dst, ROW_TILE), :], sem.at[slot]).start()
            return carry
        lax.fori_loop(0, tm, body, 0, unroll=GATHER_UNROLL)

    @pl.when(step == 0)
    def _():
        start_gather(0, 0)

    @pl.when(step + 1 < nsteps)
    def _():
        start_gather(step + 1, (step + 1) % 2)

    slot = step % 2
    for k in range(TOP_K):
        pltpu.make_async_copy(y_hbm.at[pl.ds(0, tm * ROW_TILE), :], ybuf.at[slot, k],
                              sem.at[slot]).wait()
    wt = wt_ref[0]
    w0 = wt[:, 2:3]
    w1 = wt[:, 3:4]
    f = jnp.concatenate(
        [w0 * ybuf[slot, 0, pl.ds(j, tm, stride=ROW_TILE), :]
         + w1 * ybuf[slot, 1, pl.ds(j, tm, stride=ROW_TILE), :]
         for j in range(x_ref.shape[2] // LANES)], axis=1)
    x = x_ref[0]
    o_ref[0] = _ln(alpha * x + mod_ref[0, 5:6, :] * f) * ln_g_ref[...] + ln_b_ref[...]


def _combine(x1, mod, wts, pos, y, ln_g, ln_b, n_ctx_tiles, alpha):
    B, tn, D = x1.shape
    tm = TOK_TILE
    nt = tn // tm
    grid_spec = pltpu.PrefetchScalarGridSpec(
        num_scalar_prefetch=1,
        grid=(B, nt),
        in_specs=[
            pl.BlockSpec((1, tm, D), lambda b, i, pos: (b, i, 0)),
            pl.BlockSpec((1, 6, D), lambda b, i, pos: (jnp.where(i < n_ctx_tiles, B, b), 0, 0)),
            pl.BlockSpec((1, tm, ROW_TILE), lambda b, i, pos: (b, i, 0)),
            pl.BlockSpec((1, D), lambda b, i, pos: (0, 0)),
            pl.BlockSpec((1, D), lambda b, i, pos: (0, 0)),
            pl.BlockSpec(memory_space=pl.ANY),
        ],
        out_specs=pl.BlockSpec((1, tm, D), lambda b, i, pos: (b, i, 0)),
        scratch_shapes=[
            pltpu.VMEM((2, TOP_K, tm * ROW_TILE, LANES), F32),
            pltpu.SemaphoreType.DMA((2,)),
        ],
    )
    return pl.pallas_call(
        functools.partial(_combine_kernel, alpha=alpha),
        grid_spec=grid_spec,
        out_shape=jax.ShapeDtypeStruct((B, tn, D), F32),
        compiler_params=_cparams(("arbitrary", "arbitrary")),
        name="combine",
    )(pos, x1, mod, wts, ln_g, ln_b, y)


def _segments(counts, assign):
    counts = counts.reshape(N_EXPERTS).astype(jnp.int32)
    n_assign = assign.shape[0] * assign.shape[2]
    padded = (counts + EXP_BLK - 1) // EXP_BLK * EXP_BLK
    pad_end = jnp.cumsum(padded)
    pad_start = pad_end - padded
    p_rows = n_assign + N_EXPERTS * EXP_BLK
    nb = p_rows // EXP_BLK
    blk_first = jnp.arange(nb, dtype=jnp.int32) * EXP_BLK
    blk_e = jnp.minimum(jnp.sum((pad_end[None, :] <= blk_first[:, None]).astype(jnp.int32), axis=1),
                        N_EXPERTS - 1).astype(jnp.int32)
    nblk = (pad_end[-1] // EXP_BLK).astype(jnp.int32).reshape(1)
    eid, rank = assign[:, 0, :], assign[:, 1, :]
    onehot = eid[:, :, None] == jnp.arange(N_EXPERTS, dtype=jnp.int32)
    pos = rank + jnp.sum(jnp.where(onehot, pad_start.astype(jnp.int32), 0), axis=-1)
    return pos.reshape(-1).astype(jnp.int32), pad_end.astype(jnp.int32), blk_e, nblk, p_rows


def _rope_tables(n_ctx, n_lat):
    nf = ATT_HEAD_DIM // 4
    inv = ROPE_BASE ** (-jnp.arange(nf, dtype=F32) / nf)
    rows = n_lat // GRID_W
    pos_r = jnp.repeat(jnp.arange(rows, dtype=F32), GRID_W)
    pos_c = jnp.tile(jnp.arange(GRID_W, dtype=F32), rows)
    ang_r = pos_r[:, None] * inv
    ang_c = pos_c[:, None] * inv
    cos_h = jnp.concatenate([jnp.cos(ang_r)] * 2 + [jnp.cos(ang_c)] * 2, axis=-1)
    sin_h = jnp.concatenate([-jnp.sin(ang_r), jnp.sin(ang_r),
                             -jnp.sin(ang_c), jnp.sin(ang_c)], axis=-1)
    reps = LANES // ATT_HEAD_DIM
    cos_l = jnp.tile(cos_h, (1, reps))
    sin_l = jnp.tile(sin_h, (1, reps))
    cos_t = jnp.concatenate([jnp.ones((n_ctx, LANES), F32), cos_l], axis=0)
    sin_t = jnp.concatenate([jnp.zeros((n_ctx, LANES), F32), sin_l], axis=0)
    return cos_t, sin_t


def _projection_weights(w_in_l):
    offs = np.cumsum((0, 3 * CONV_DIM, ATT_Q_DIM, ATT_KV_DIM, ATT_KV_DIM,
                      MLSTM_DIM, MLSTM_DIM, MLSTM_DIM, MLSTM_DIM, N_GATE)).tolist()
    a, q, k, v, mq, mk, mv, mo, g = [w_in_l[:, offs[n]:offs[n + 1]] for n in range(9)]
    hd = ATT_HEAD_DIM

    def swap(w):
        return jnp.concatenate([w[:, hd:], w[:, :hd]], axis=1)

    w_tok = jnp.concatenate([a, q, k, swap(k), v, swap(v), mq, mk], axis=1).astype(BF16)
    w_chan = jnp.concatenate([mv, mo, g], axis=1).T.astype(BF16)
    w_gate = w_in_l[:, offs[9]:].astype(BF16)
    return w_tok, w_chan, w_gate


def kernel(x, c, ctx, c_ctx, w_ada, b_ada, w_in, conv_w, attn_sink, mlstm_gate_b, mlstm_norm_w,
           w_proj_a, w_proj_b, w_proj_c, w_out, ln1_g, ln1_b, w_route_group, b_route_group,
           w_route_expert, b_route_expert, w_expert_in, w_expert_out, ln2_g, ln2_b):
    B, L, D = x.shape
    n_ctx = ctx.shape[1]
    depth = w_ada.shape[0]
    T = n_ctx + L
    alpha = (2 * depth) ** 0.25
    assert D == D_MODEL and n_ctx % TOK_TILE == 0 and L % TOK_TILE == 0 and L % GRID_W == 0
    assert MLSTM_CH == LANES and ATT_QB == LANES
    nct = n_ctx // TOK_TILE

    nrows = -(-(B + 1) // 8) * 8
    cond = jnp.concatenate([c, c_ctx[None, :], jnp.zeros((nrows - B - 1, D), F32)], axis=0)
    mod_all = _ada(cond, w_ada, b_ada).reshape(depth, nrows, 6, D)

    cos_t, sin_t = _rope_tables(n_ctx, L)
    streams = (ctx, x)

    for i in range(depth):
        need_ctx = i < depth - 1
        mod = mod_all[i]
        w_tok, w_chan, w_gate = _projection_weights(w_in[i])

        a0u, q, kv, mqk, mvo_t, gcol, grow = _inproj(streams, mod, w_tok, w_chan, cos_t, sin_t, n_ctx)
        yb = _attention(q, kv, attn_sink[i], n_ctx)
        hf_t, hb_t = _mlstm(mqk, mvo_t, gcol, grow, mlstm_gate_b[i], n_ctx)

        off = 0 if need_ctx else nct
        w_route = jnp.pad(jnp.concatenate([w_route_group[i], w_route_expert[i]], axis=1).T,
                          ((0, LANES - N_ROUTE), (0, 0)))
        b_route = jnp.pad(jnp.concatenate([b_route_group[i], b_route_expert[i]]),
                          (0, LANES - N_ROUTE)).reshape(LANES, 1)
        x1, h2v, wts, assign, counts = _merge(
            streams, mod, a0u, yb, hf_t, hb_t, mvo_t, conv_w[i], mlstm_norm_w[i].reshape(MLSTM_DIM, 1),
            w_gate, w_proj_a[i].astype(BF16), w_proj_b[i].astype(BF16), w_proj_c[i].astype(BF16),
            w_out[i].astype(BF16), ln1_g[i].reshape(1, D), ln1_b[i].reshape(1, D),
            w_route, b_route, n_ctx, off, alpha)

        tn = x1.shape[1]
        pos, pend, blk_e, nblk, p_rows = _segments(counts, assign.reshape(-1, 2, TOP_K * TOK_TILE))
        xs = _dispatch(h2v.reshape(-1, LANES), pos, pend, nblk, p_rows, TOK_TILE)
        y = _experts(xs, blk_e, nblk, w_expert_in, w_expert_out, i)
        streams = (_combine(x1, mod, wts, pos, y,
                            ln2_g[i].reshape(1, D), ln2_b[i].reshape(1, D),
                            nct if need_ctx else 0, alpha),)
    return streams[0]
```

```python
import functools

import jax
import jax.numpy as jnp
import numpy as np
from jax import lax
from jax.experimental import pallas as pl
from jax.experimental.pallas import tpu as pltpu

D_MODEL = 1024
GRID_W = 64
CONV_DIM = 256
ATT_HEADS = 8
ATT_KV_HEADS = 2
ATT_HEAD_DIM = 64
ATT_WINDOW = 128
ROPE_BASE = 10000.0
MLSTM_HEADS = 4
MLSTM_HEAD_DIM = 64
MLSTM_DIM = MLSTM_HEADS * MLSTM_HEAD_DIM
N_GROUPS = 4
EXPERTS_PER_GROUP = 8
N_EXPERTS = N_GROUPS * EXPERTS_PER_GROUP
TOP_K = 2
D_EXPERT = D_MODEL // 2
LN_EPS = 1e-6
NEG_INF = -1e30

ATT_Q_DIM = ATT_HEADS * ATT_HEAD_DIM
ATT_KV_DIM = ATT_KV_HEADS * ATT_HEAD_DIM
N_GATE = 4 * MLSTM_HEADS
MIX_COLS = 3 * CONV_DIM + ATT_Q_DIM + 2 * ATT_KV_DIM + 4 * MLSTM_DIM
N_ROUTE = N_GROUPS + N_EXPERTS

LANES = 128
VMEM_LIMIT = 56 * 1024 * 1024
TOK_TILE = 256
PAIR_SAMPLES = 4
MERGE_SAMPLES = 2
ATT_QB = 128
ATT_STEP = 256
ATT_LAG_SOFTMAX = 1
ATT_LAG_VALUES = 2
MLSTM_CH = 128
MLSTM_STEP = 256
EXP_BLK = 512
EXP_ROWS = 256
HALO = 16
ROW_TILE = 8
GATHER_UNROLL = 8

F32 = jnp.float32
BF16 = jnp.bfloat16


def _cparams(sem):
    return pltpu.CompilerParams(dimension_semantics=sem, vmem_limit_bytes=VMEM_LIMIT)


def _ln(x, axis=-1):
    mu = jnp.mean(x, axis=axis, keepdims=True)
    xc = x - mu
    var = jnp.mean(xc * xc, axis=axis, keepdims=True)
    return xc * lax.rsqrt(var + LN_EPS)


def _sigmoid(x):
    return 0.5 * jnp.tanh(0.5 * x) + 0.5


def _split3(x):
    hi = x.astype(BF16)
    r1 = x - hi.astype(F32)
    mid = r1.astype(BF16)
    lo = (r1 - mid.astype(F32)).astype(BF16)
    return hi, mid, lo


def _dot(a, b):
    return jnp.dot(a, b, preferred_element_type=F32)


def _dot_nt(a, b):
    return lax.dot_general(a, b, (((1,), (1,)), ((), ())), preferred_element_type=F32)


def _dot3(x, rhs_b):
    return sum(_dot(part, rhs_b) for part in _split3(x))


def _dot3_l(lhs_b, x):
    return sum(_dot(lhs_b, part) for part in _split3(x))


def _ada_kernel(c_ref, w_ref, b_ref, o_ref):
    cv = c_ref[...]
    s = cv * _sigmoid(cv)
    o_ref[0] = _dot(s.astype(BF16), w_ref[0].astype(BF16)) + b_ref[0]


def _ada(cond, w_ada, b_ada):
    depth, d, n6 = w_ada.shape
    rows = cond.shape[0]
    nt = n6 // d
    return pl.pallas_call(
        _ada_kernel,
        grid=(depth, nt),
        in_specs=[
            pl.BlockSpec((rows, d), lambda l, j: (0, 0)),
            pl.BlockSpec((1, d, d), lambda l, j: (l, 0, j)),
            pl.BlockSpec((1, 1, d), lambda l, j: (l, 0, j)),
        ],
        out_specs=pl.BlockSpec((1, rows, d), lambda l, j: (l, 0, j)),
        out_shape=jax.ShapeDtypeStruct((depth, rows, n6), F32),
        compiler_params=_cparams(("arbitrary", "arbitrary")),
        name="ada_mod",
    )(cond, w_ada, b_ada.reshape(depth, 1, n6))


def _rope(x, cos, sin_signed, lane):
    swapped = jnp.where(lane % 32 < 16,
                        pltpu.roll(x, LANES - 16, axis=1),
                        pltpu.roll(x, 16, axis=1))
    return x * cos + swapped * sin_signed


_OFF_A = 0
_OFF_Q = _OFF_A + 3 * CONV_DIM
_OFF_K = _OFF_Q + ATT_Q_DIM
_OFF_V = _OFF_K + 2 * ATT_KV_DIM
_OFF_MQK = _OFF_V + 2 * ATT_KV_DIM
_W_TOK_COLS = _OFF_MQK + 2 * MLSTM_DIM


def _stream_specs(streams, tm, nct, off=0, nb=1):
    d = streams[0].shape[2]
    if len(streams) == 1:
        return [pl.BlockSpec((nb, tm, d), lambda b, i, *_: (b, i + off, 0))]
    return [pl.BlockSpec((nb, tm, d), lambda b, i, *_: (b, jnp.minimum(i + off, nct - 1), 0)),
            pl.BlockSpec((nb, tm, d), lambda b, i, *_: (b, jnp.maximum(i + off - nct, 0), 0))]


def _stream_tile(refs, i, nct):
    if len(refs) == 1:
        return refs[0][0]
    return jnp.where(i < nct, refs[0][0], refs[1][0])


def _inproj_kernel(*refs, n_src, nct, n_batch):
    mod_ref = refs[n_src]
    shared = refs[n_src + 1:n_src + 5]
    outs = refs[n_src + 5:]
    is_ctx = pl.program_id(1) < nct
    tiles = []
    for smp in range(PAIR_SAMPLES):
        row = jnp.where(is_ctx, n_batch, pl.program_id(0) * PAIR_SAMPLES + smp)
        tiles.append(_inproj_tile(
            *[r.at[pl.ds(smp, 1)] for r in refs[:n_src]], mod_ref.at[pl.ds(row, 1)], *shared,
            *[r.at[pl.ds(smp, 1)] for r in outs], n_src=n_src, nct=nct))
    live = list(tiles)
    while live:
        for g in list(live):
            if next(g, live) is live:
                live.remove(g)


def _inproj_tile(*refs, n_src, nct):
    (mod_ref, w_ref, wt_ref, cos_ref, sin_ref,
     a0u_ref, q_ref, kv_ref, mqk_ref, mvo_t_ref, gcol_ref, grow_ref) = refs[n_src:]
    x = _stream_tile(refs[:n_src], pl.program_id(1), nct)
    shift = mod_ref[0, 0:1, :]
    scale = mod_ref[0, 1:2, :]
    h = (_ln(x) * (1.0 + scale) + shift).astype(BF16)
    cos = cos_ref[...]
    sin = sin_ref[...]
    lane = lax.broadcasted_iota(jnp.int32, cos.shape, 1)
    yield

    def cols(lo, n):
        return _dot(h, w_ref[:, lo:lo + n])

    za = cols(_OFF_A, 3 * CONV_DIM)
    yield
    a0u_ref[0, :, 0:CONV_DIM] = za[:, 0:CONV_DIM].astype(BF16)
    a0u_ref[0, :, CONV_DIM:2 * CONV_DIM] = (
        za[:, CONV_DIM:2 * CONV_DIM] * za[:, 2 * CONV_DIM:3 * CONV_DIM]).astype(BF16)
    zq = cols(_OFF_Q, ATT_Q_DIM)
    yield
    qscale = ATT_HEAD_DIM ** -0.5
    for j in range(ATT_Q_DIM // LANES):
        piece = _rope(zq[:, j * LANES:(j + 1) * LANES], cos, sin, lane)
        q_ref[0, :, j * LANES:(j + 1) * LANES] = (piece * qscale).astype(BF16)
    zk = cols(_OFF_K, 2 * ATT_KV_DIM)
    yield
    for j in range(2):
        kv_ref[0, :, j * LANES:(j + 1) * LANES] = _rope(
            zk[:, j * LANES:(j + 1) * LANES], cos, sin, lane).astype(BF16)
    kv_ref[0, :, 2 * LANES:] = cols(_OFF_V, 2 * ATT_KV_DIM).astype(BF16)
    yield
    zm = cols(_OFF_MQK, 2 * MLSTM_DIM)
    yield
    mqk_ref[0, :, 0:MLSTM_DIM] = zm[:, 0:MLSTM_DIM].astype(BF16)
    mqk_ref[0, :, MLSTM_DIM:] = (zm[:, MLSTM_DIM:] * (MLSTM_HEAD_DIM ** -0.5)).astype(BF16)
    nchan = mvo_t_ref.shape[1]
    zt = _dot_nt(wt_ref[...], h)
    yield
    mvo_t_ref[0] = zt[0:nchan, :].astype(BF16)
    grow = zt[nchan:, :]
    grow_ref[0] = grow
    gcol_ref[0] = grow.T


def _inproj(streams, mod, w_tok, w_chan, cos_t, sin_t, n_ctx):
    B, _, D = streams[0].shape
    T = sum(s.shape[1] for s in streams) if len(streams) > 1 else streams[0].shape[1]
    tm = TOK_TILE
    nct = n_ctx // tm
    nchan = w_chan.shape[0] - N_GATE
    ns = PAIR_SAMPLES
    assert B % ns == 0

    def tok(width):
        return pl.BlockSpec((ns, tm, width), lambda b, i: (b, i, 0))

    def chan(rows):
        return pl.BlockSpec((ns, rows, tm), lambda b, i: (b, 0, i))

    outs = [(2 * CONV_DIM, BF16), (ATT_Q_DIM, BF16), (4 * ATT_KV_DIM, BF16), (2 * MLSTM_DIM, BF16)]
    return pl.pallas_call(
        functools.partial(_inproj_kernel, n_src=len(streams), nct=nct, n_batch=B),
        grid=(B // ns, T // tm),
        in_specs=_stream_specs(streams, tm, nct, 0, ns) + [
            pl.BlockSpec(mod.shape, lambda b, i: (0, 0, 0), pipeline_mode=pl.Buffered(1)),
            pl.BlockSpec(w_tok.shape, lambda b, i: (0, 0), pipeline_mode=pl.Buffered(1)),
            pl.BlockSpec(w_chan.shape, lambda b, i: (0, 0), pipeline_mode=pl.Buffered(1)),
            pl.BlockSpec((tm, LANES), lambda b, i: (i, 0)),
            pl.BlockSpec((tm, LANES), lambda b, i: (i, 0)),
        ],
        out_specs=[tok(w) for w, _ in outs] + [
            chan(nchan), tok(N_GATE), chan(N_GATE)],
        out_shape=[jax.ShapeDtypeStruct((B, T, w), dt) for w, dt in outs] + [
            jax.ShapeDtypeStruct((B, nchan, T), BF16),
            jax.ShapeDtypeStruct((B, T, N_GATE), F32), jax.ShapeDtypeStruct((B, N_GATE, T), F32)],
        compiler_params=_cparams(("parallel", "arbitrary")),
        name="in_proj",
    )(*streams, mod, w_tok, w_chan, cos_t, sin_t)


def _attn_kernel(sink_ref, q_ref, kvc_ref, kvp_ref, kvm_ref, kvn_ref, o_ref, *, n_ctx_blk, n_blk):
    n_ctx = kvc_ref.shape[1]
    qb = kvp_ref.shape[1]
    nsub = q_ref.shape[1] // qb
    nk = n_ctx + 3 * qb
    half = LANES // 2

    lane_row = lax.broadcasted_iota(jnp.int32, (1, LANES), 1)
    keep = [(lane_row < half).astype(F32).astype(BF16), (lane_row >= half).astype(F32).astype(BF16)]
    pad_v = jnp.zeros((qb, LANES), BF16)
    qi = lax.broadcasted_iota(jnp.int32, (qb, qb), 0)
    ki = lax.broadcasted_iota(jnp.int32, (qb, qb), 1)
    lane_q = lax.broadcasted_iota(jnp.int32, (qb, LANES), 1)
    neg = jnp.full((qb, LANES), NEG_INF, F32)
    group = ATT_HEADS // ATT_KV_HEADS
    blocks = []
    for blk_i in range(q_ref.shape[0] * nsub):
        smp, sb = divmod(blk_i, nsub)
        around = ([kvp_ref[smp]] + [kvm_ref[smp, n * qb:(n + 1) * qb, :] for n in range(nsub)]
                  + [kvn_ref[smp]])
        g = pl.program_id(1) * nsub + sb
        is_lat = g >= n_ctx_blk
        has_prev = g >= n_ctx_blk + 1
        has_next = g < n_blk - 1
        kv_all = jnp.concatenate([kvc_ref[smp]] + around[sb:sb + 3], axis=0)
        k_ext, v_ext = {}, {}
        for tile in range(2):
            for par in range(2):
                k_ext[tile, par] = kv_all[:, tile * LANES:(tile + 1) * LANES] * keep[par]
                v = jnp.concatenate([kv_all[:, (2 + tile) * LANES:(3 + tile) * LANES], pad_v], axis=0)
                v_ext[tile, par] = jnp.concatenate(
                    [v * keep[par], jnp.broadcast_to(keep[par], v.shape)], axis=1)
        blocks.append(dict(
            is_lat=is_lat, ok_prev=jnp.logical_and(ki >= qi, has_prev),
            ok_next=jnp.logical_and(ki <= qi, jnp.logical_and(has_next, is_lat)),
            k_ext=k_ext, v_ext=v_ext, q=q_ref[smp, sb * qb:(sb + 1) * qb, :]))

    jobs = [(sb, pair, par) for sb in range(len(blocks)) for pair in range(ATT_HEADS // 2)
            for par in range(2)]
    scores, probs, accs, outs = {}, {}, {}, {}

    def operands(job):
        sb, pair, par = job
        kvh = (2 * pair) // group
        tile = 0 if kvh == par else 1
        return blocks[sb], tile

    def stage_scores(job):
        sb, pair, par = job
        blk, tile = operands(job)
        scores[job] = _dot_nt(blk["q"][:, pair * LANES:(pair + 1) * LANES], blk["k_ext"][tile, par])

    def stage_softmax(job):
        sb, pair, par = job
        blk, _ = operands(job)
        s = scores.pop(job)
        s_ext = jnp.concatenate([
            s[:, 0:n_ctx],
            jnp.where(blk["ok_prev"], s[:, n_ctx:n_ctx + qb], neg),
            jnp.where(blk["is_lat"], s[:, n_ctx + qb:n_ctx + 2 * qb], neg),
            jnp.where(blk["ok_next"], s[:, n_ctx + 2 * qb:nk], neg),
            jnp.where(lane_q == 0, sink_ref[2 * pair + par], neg)], axis=1)
        m = jnp.max(s_ext, axis=-1, keepdims=True)
        probs[job] = jnp.exp(s_ext - m).astype(BF16)

    def stage_values(job):
        sb, pair, par = job
        blk, tile = operands(job)
        part = _dot(probs.pop(job), blk["v_ext"][tile, par])
        if par == 0:
            accs[sb, pair] = part
        else:
            acc = accs.pop((sb, pair)) + part
            outs[sb, pair] = acc[:, 0:LANES] / acc[:, LANES:]

    for t in range(len(jobs) + ATT_LAG_VALUES):
        if t < len(jobs):
            stage_scores(jobs[t])
        if 0 <= t - ATT_LAG_SOFTMAX < len(jobs):
            stage_softmax(jobs[t - ATT_LAG_SOFTMAX])
        if 0 <= t - ATT_LAG_VALUES < len(jobs):
            stage_values(jobs[t - ATT_LAG_VALUES])
    for blk_i in range(len(blocks)):
        smp, sb = divmod(blk_i, nsub)
        o_ref[smp, sb * qb:(sb + 1) * qb, :] = jnp.concatenate(
            [outs[blk_i, pair] for pair in range(ATT_HEADS // 2)], axis=-1).astype(o_ref.dtype)


def _attention(q, kv, sink, n_ctx):
    B, T, _ = q.shape
    qb = ATT_QB
    nsub = ATT_STEP // qb
    n_blk = T // qb
    nq = T // ATT_STEP
    n_ctx_blk = n_ctx // qb
    kvw = kv.shape[2]
    ns = PAIR_SAMPLES
    assert B % ns == 0
    grid_spec = pltpu.PrefetchScalarGridSpec(
        num_scalar_prefetch=1,
        grid=(B // ns, nq),
        in_specs=[
            pl.BlockSpec((ns, ATT_STEP, ATT_Q_DIM), lambda b, j, sk: (b, j, 0)),
            pl.BlockSpec((ns, n_ctx, kvw), lambda b, j, sk: (b, 0, 0)),
            pl.BlockSpec((ns, qb, kvw), lambda b, j, sk: (b, jnp.maximum(j * nsub - 1, 0), 0)),
            pl.BlockSpec((ns, ATT_STEP, kvw), lambda b, j, sk: (b, j, 0)),
            pl.BlockSpec((ns, qb, kvw), lambda b, j, sk: (b, jnp.minimum((j + 1) * nsub, n_blk - 1), 0)),
        ],
        out_specs=pl.BlockSpec((ns, ATT_STEP, ATT_Q_DIM), lambda b, j, sk: (b, j, 0)),
    )
    return pl.pallas_call(
        functools.partial(_attn_kernel, n_ctx_blk=n_ctx_blk, n_blk=n_blk),
        grid_spec=grid_spec,
        out_shape=jax.ShapeDtypeStruct((B, T, ATT_Q_DIM), BF16),
        compiler_params=_cparams(("parallel", "arbitrary")),
        name="window_attn",
    )(sink, q, kv, kv, kv, kv)


def _log_sigmoid(x):
    return jnp.minimum(x, 0.0) - jnp.log1p(jnp.exp(-jnp.abs(x)))


def _rows_to_lanes(a, base):
    return jnp.concatenate([a[base + h:base + h + 1, :] for h in range(MLSTM_HEADS)], axis=1)


def _mlstm_gates(d, gcol, grow):
    ch = gcol.shape[0]
    nh = MLSTM_HEADS
    wide = nh * ch
    fwd = d == 0

    r_i = lax.broadcasted_iota(jnp.int32, (ch, ch), 0)
    c_i = lax.broadcasted_iota(jnp.int32, (ch, ch), 1)
    seen_t = (r_i <= c_i) if fwd else (r_i >= c_i)
    seen_tt = (c_i <= r_i) if fwd else (c_i >= r_i)
    r_w = lax.broadcasted_iota(jnp.int32, (ch, wide), 0)
    s_w = lax.broadcasted_iota(jnp.int32, (ch, wide), 1) % ch
    seen_w = (r_w <= s_w) if fwd else (r_w >= s_w)

    base_i = 2 * d * nh
    base_f = base_i + nh

    lf_col = _log_sigmoid(gcol)
    cum_col = _dot3_l(seen_tt.astype(BF16), lf_col)
    lane16 = lax.broadcasted_iota(jnp.int32, gcol.shape, 1)
    z = jnp.where(jnp.logical_and(lane16 >= base_i, lane16 < base_f), gcol, -cum_col)
    ch16 = lax.broadcasted_iota(jnp.int32, (N_GATE, wide), 0)
    hd16 = lax.broadcasted_iota(jnp.int32, (N_GATE, wide), 1) // ch
    sel = jnp.logical_or(ch16 == base_i + hd16, ch16 == base_f + hd16).astype(BF16)
    x_t = _dot3(z, sel)

    lf_row = _log_sigmoid(grow)
    rhs2 = jnp.concatenate([seen_t.astype(BF16), jnp.ones((ch, ch), BF16)], axis=1)
    rows = _dot3(lf_row, rhs2)
    b_all = _rows_to_lanes(rows[:, 0:ch], base_f)
    g_all = _rows_to_lanes(rows[:, ch:], base_f)
    li_all = _rows_to_lanes(grow, base_i)
    yield
    dmat = jnp.where(seen_w, x_t + b_all, -jnp.inf)
    return (dmat, jnp.max(dmat, axis=0, keepdims=True), jnp.max(x_t, axis=0, keepdims=True),
            b_all, g_all, li_all)


def _mlstm_dir(qk, v_t, gates, ct_bd, n_bd, m_prev):
    dmat, dmat_max, x_max, b_all, g_all, li_all = gates
    ch = qk.shape[0]
    nh = MLSTM_HEADS
    dh = MLSTM_HEAD_DIM
    a = b_all + m_prev
    m = jnp.maximum(a, dmat_max)
    w_intra = jnp.exp(dmat - m)
    e_inter = jnp.exp(a - m)
    yield

    q = qk[:, 0:MLSTM_DIM]
    k = qk[:, MLSTM_DIM:]
    lb = lax.broadcasted_iota(jnp.int32, (1, MLSTM_DIM), 1) // dh
    q_bd = jnp.concatenate([q * (lb == h).astype(F32).astype(BF16) for h in range(nh)], axis=0)
    s_t = _dot_nt(k, q_bd) * w_intra
    nq = _rows_to_lanes(_dot_nt(n_bd.astype(BF16), q), 0)
    den = jnp.sum(s_t, axis=0, keepdims=True) + e_inter * nq
    inv = 1.0 / jnp.maximum(jnp.abs(den), jnp.exp(-m))
    inter_t = _dot_nt(ct_bd.astype(BF16), q)
    s_b = s_t.astype(BF16)
    yield
    outs = []
    for h in range(nh):
        seg = slice(h * ch, (h + 1) * ch)
        blk = slice(h * dh, (h + 1) * dh)
        num = _dot(v_t[blk, :], s_b[:, seg]) + e_inter[:, seg] * inter_t[blk, :]
        outs.append(num * inv[:, seg])
    h_t = jnp.concatenate(outs, axis=0)
    yield

    m_loc = g_all + x_max
    m_new = jnp.maximum(g_all + m_prev, m_loc)
    sa = jnp.exp(g_all + m_prev - m_new)
    sb = jnp.exp(m_loc - m_new)
    e_loc = jnp.exp(g_all - b_all + li_all - m_loc)
    v_e = jnp.concatenate(
        [v_t[h * dh:(h + 1) * dh, :].astype(F32) * e_loc[:, h * ch:(h + 1) * ch] for h in range(nh)],
        axis=0).astype(BF16)
    ct_loc = _dot(v_e, k)
    yield
    reps = MLSTM_DIM // ch

    def per_head_rows(row, nrows):
        return jnp.concatenate(
            [jnp.broadcast_to(jnp.concatenate([row[:, h * ch:(h + 1) * ch]] * reps, axis=1),
                              (nrows, MLSTM_DIM)) for h in range(nh)], axis=0)

    eb = lax.broadcasted_iota(jnp.int32, (MLSTM_DIM, MLSTM_DIM), 0) // dh
    db = lax.broadcasted_iota(jnp.int32, (MLSTM_DIM, MLSTM_DIM), 1) // dh
    ct_new = jnp.where(eb == db, per_head_rows(sa, dh) * ct_bd + per_head_rows(sb, dh) * ct_loc, 0.0)
    nrow = n_bd.shape[0]
    e_rows = jnp.concatenate([e_loc[:, h * ch:(h + 1) * ch] for h in range(nh)]
                             + [jnp.zeros((nrow - nh, ch), F32)], axis=0).astype(BF16)
    n_loc = _dot(e_rows, k)
    pad = jnp.zeros((nrow - nh, MLSTM_DIM), F32)
    sa8 = jnp.concatenate([per_head_rows(sa, 1), pad], axis=0)
    sb8 = jnp.concatenate([per_head_rows(sb, 1), pad], axis=0)
    hb8 = lax.broadcasted_iota(jnp.int32, (nrow, MLSTM_DIM), 0)
    db8 = lax.broadcasted_iota(jnp.int32, (nrow, MLSTM_DIM), 1) // dh
    n_new = jnp.where(hb8 == db8, sa8 * n_bd + sb8 * n_loc, 0.0)
    return h_t, ct_new, n_new, m_new


def _mlstm_kernel(qk_f_ref, qk_b_ref, vt_f_ref, vt_b_ref, gc_f_ref, gc_b_ref, gr_f_ref, gr_b_ref,
                  bcol_ref, brow_ref, hf_ref, hb_ref, c_scr, n_scr, m_scr):
    j = pl.program_id(1)

    @pl.when(j == 0)
    def _():
        c_scr[...] = jnp.zeros_like(c_scr)
        n_scr[...] = jnp.zeros_like(n_scr)
        m_scr[...] = jnp.zeros_like(m_scr)

    nscan = c_scr.shape[0]
    states = [(c_scr[k], n_scr[k], m_scr[k]) for k in range(nscan)]
    dirs = ((qk_f_ref, vt_f_ref, gc_f_ref, gr_f_ref, hf_ref),
            (qk_b_ref, vt_b_ref, gc_b_ref, gr_b_ref, hb_ref))
    ch = MLSTM_CH
    nsub = qk_f_ref.shape[1] // ch

    def scan(k):
        d = k % 2
        qk_ref, vt_ref, gc_ref, gr_ref, out_ref = [r.at[pl.ds(k // 2, 1)] for r in dirs[d]]
        st = states[k]
        order = [slice(sc * ch, (sc + 1) * ch)
                 for sc in (range(nsub) if d == 0 else reversed(range(nsub)))]
        gates = []
        for ts in order:
            gates.append((yield from _mlstm_gates(
                d, gc_ref[0, ts, :] + bcol_ref[...], gr_ref[0, :, ts] + brow_ref[...])))
        for ts, gt in zip(order, gates):
            h_t, *st = yield from _mlstm_dir(qk_ref[0, ts, :], vt_ref[0, :, ts], gt, *st)
            out_ref[0, :, ts] = h_t
        states[k] = st

    live = [scan(k) for k in range(nscan)]
    while live:
        for g in list(live):
            if next(g, live) is live:
                live.remove(g)
    for k in range(nscan):
        c_scr[k] = states[k][0]
        n_scr[k] = states[k][1]
        m_scr[k] = states[k][2]


def _mlstm(mqk, mvo_t, gcol, grow, gate_b, n_ctx):
    B, T, _ = mqk.shape
    ch = MLSTM_STEP
    nc = T // ch
    ncc = n_ctx // ch
    ns = PAIR_SAMPLES
    assert B % ns == 0

    def rev(j):
        return jnp.where(j < ncc, ncc - 1 - j, nc - 1 - (j - ncc))

    def tok(width, order):
        return pl.BlockSpec((ns, ch, width), lambda b, j: (b, order(j), 0))

    def chan(rows, order):
        return pl.BlockSpec((ns, rows, ch), lambda b, j: (b, 0, order(j)))

    ident = lambda j: j
    return pl.pallas_call(
        _mlstm_kernel,
        grid=(B // ns, nc),
        in_specs=[
            tok(2 * MLSTM_DIM, ident), tok(2 * MLSTM_DIM, rev),
            chan(MLSTM_DIM, ident), chan(MLSTM_DIM, rev),
            tok(N_GATE, ident), tok(N_GATE, rev),
            chan(N_GATE, ident), chan(N_GATE, rev),
            pl.BlockSpec((1, N_GATE), lambda b, j: (0, 0)),
            pl.BlockSpec((N_GATE, 1), lambda b, j: (0, 0)),
        ],
        out_specs=[chan(MLSTM_DIM, ident), chan(MLSTM_DIM, rev)],
        out_shape=[jax.ShapeDtypeStruct((B, MLSTM_DIM, T), F32)] * 2,
        scratch_shapes=[
            pltpu.VMEM((2 * ns, MLSTM_DIM, MLSTM_DIM), F32),
            pltpu.VMEM((2 * ns, 8, MLSTM_DIM), F32),
            pltpu.VMEM((2 * ns, 1, MLSTM_HEADS * MLSTM_CH), F32),
        ],
        compiler_params=_cparams(("parallel", "arbitrary")),
        name="mlstm",
    )(mqk, mqk, mvo_t, mvo_t, gcol, gcol, grow, grow,
      gate_b.reshape(1, N_GATE), gate_b.reshape(N_GATE, 1))


def _merge_kernel(*refs, n_src, off, n_ctx_tiles, alpha, n_batch):
    n_in = n_src + 7
    mod_ref = refs[n_in]
    consts = refs[n_in + 1:n_in + 13]
    outs = refs[n_in + 13:n_in + 17]
    counts_ref, carry = refs[n_in + 17], refs[n_in + 18]

    @pl.when(jnp.logical_and(pl.program_id(0) == 0, pl.program_id(1) == 0))
    def _():
        carry[...] = jnp.zeros_like(carry)

    is_ctx = pl.program_id(1) + off < n_ctx_tiles
    tiles = []
    for smp in range(MERGE_SAMPLES):
        row = jnp.where(is_ctx, n_batch, pl.program_id(0) * MERGE_SAMPLES + smp)
        tiles.append(_merge_tile(
            *[r.at[pl.ds(smp, 1)] for r in refs[:n_in]], mod_ref.at[pl.ds(row, 1)], *consts,
            *[r.at[pl.ds(smp, 1)] for r in outs], counts_ref, carry,
            n_src=n_src, off=off, n_ctx_tiles=n_ctx_tiles, alpha=alpha))
    live = list(tiles)
    while live:
        for g in list(live):
            if next(g, live) is live:
                live.remove(g)


def _merge_tile(*refs, n_src, off, n_ctx_tiles, alpha):
    (a0u_ref, up_ref, un_ref, yb_ref, hf_ref, hb_ref, mvo_t_ref, mod_ref,
     convw_ref, normw_ref, wg_ref, wpa_ref, wpb_ref, wpc_ref, wo_ref,
     ln_g_ref, ln_b_ref, wr_ref, br_ref, upper_ref,
     x1_ref, h2_ref, wt_ref, assign_ref, counts_ref, carry) = refs[n_src:]
    i = pl.program_id(1) + off
    nt = pl.num_programs(1) + off
    x_tile = _stream_tile(refs[:n_src], i, n_ctx_tiles)
    tm = x_tile.shape[0]
    d = x_tile.shape[1]

    a0 = a0u_ref[0, :, 0:CONV_DIM].astype(F32)
    u = a0u_ref[0, :, CONV_DIM:].astype(F32)
    prev_ok = jnp.logical_and(i != 0, i != n_ctx_tiles)
    next_ok = jnp.logical_and(i != n_ctx_tiles - 1, i != nt - 1)
    u_prev = jnp.where(prev_ok, up_ref[0, HALO - 1:HALO, CONV_DIM:].astype(F32), 0.0)
    u_next = jnp.where(next_ok, un_ref[0, 0:1, CONV_DIM:].astype(F32), 0.0)
    row = lax.broadcasted_iota(jnp.int32, u.shape, 0)
    u_dn = jnp.where(row == 0, u_prev, pltpu.roll(u, 1, axis=0))
    u_up = jnp.where(row == tm - 1, u_next, pltpu.roll(u, tm - 1, axis=0))
    cw = convw_ref[...]
    ya_all = (a0 * (u_dn * cw[0:1, :] + u * cw[1:2, :] + u_up * cw[2:3, :])).astype(BF16)

    w_hi = wr_ref[...].astype(BF16)
    sub = lax.broadcasted_iota(jnp.int32, (w_hi.shape[0], tm), 0)
    big = jnp.int32(2 * LANES)
    is_g = sub < N_GROUPS
    yield

    x = x_tile
    h = (_ln(x) * (1.0 + mod_ref[0, 1:2, :]) + mod_ref[0, 0:1, :]).astype(BF16)

    hm = hf_ref[0] + hb_ref[0]
    normed = jnp.concatenate(
        [_ln(hm[hd * MLSTM_HEAD_DIM:(hd + 1) * MLSTM_HEAD_DIM, :], axis=0)
         for hd in range(MLSTM_HEADS)], axis=0)
    yc_t = _sigmoid(mvo_t_ref[0].astype(F32)) * (normed * normw_ref[...])
    yc = yc_t.T
    yield

    pa = _dot(ya_all, wpa_ref[...])
    pb = _dot(yb_ref[0], wpb_ref[...])
    pc = _dot(yc.astype(BF16), wpc_ref[...])
    yield
    merged = None
    for n, proj in enumerate((pa, pb, pc)):
        gated = _sigmoid(_dot(h, wg_ref[:, n * d:(n + 1) * d])) * proj
        merged = gated if merged is None else merged + gated
        yield
    yl = _dot(merged.astype(BF16), wo_ref[...])
    yield

    x1 = _ln(alpha * x + mod_ref[0, 2:3, :] * yl) * ln_g_ref[...] + ln_b_ref[...]
    x1_ref[0] = x1
    yield
    h2 = _ln(x1) * (1.0 + mod_ref[0, 4:5, :]) + mod_ref[0, 3:4, :]
    for s in range(d // LANES):
        h2_ref[0, pl.ds(s, tm, stride=ROW_TILE), :] = h2[:, s * LANES:(s + 1) * LANES]
    yield

    lg = _dot_nt(w_hi, h2.astype(BF16)) + br_ref[...]
    gl = jnp.where(is_g, lg, -jnp.inf)
    g_max = jnp.max(gl, axis=0, keepdims=True)
    g_sel = jnp.min(jnp.where(gl == g_max, sub, big), axis=0, keepdims=True)
    g_p = 1.0 / jnp.sum(jnp.where(is_g, jnp.exp(gl - g_max), 0.0), axis=0, keepdims=True)
    lo = N_GROUPS + EXPERTS_PER_GROUP * g_sel
    el = jnp.where(jnp.logical_and(sub >= lo, sub < lo + EXPERTS_PER_GROUP), lg, -jnp.inf)
    e1 = jnp.max(el, axis=0, keepdims=True)
    i1 = jnp.min(jnp.where(el == e1, sub, big), axis=0, keepdims=True)
    el2 = jnp.where(sub == i1, -jnp.inf, el)
    e2 = jnp.max(el2, axis=0, keepdims=True)
    i2 = jnp.min(jnp.where(el2 == e2, sub, big), axis=0, keepdims=True)
    t = jnp.exp(e2 - e1)
    w1 = g_p / (1.0 + t)
    w2 = w1 * t
    eid1 = (i1 - N_GROUPS).astype(F32)
    eid2 = (i2 - N_GROUPS).astype(F32)
    rows = jnp.concatenate([eid1, eid2, w1, w2, jnp.zeros((4, tm), F32)], axis=0)
    wt_ref[0] = rows.T
    yield
    eid = jnp.concatenate([eid1, eid2], axis=1)
    sub_e = lax.broadcasted_iota(jnp.int32, (N_EXPERTS, eid.shape[1]), 0).astype(F32)
    onehot = jnp.where(sub_e == eid, 1.0, 0.0)
    earlier = _dot(onehot.astype(BF16), upper_ref[...])
    seen = carry[...]
    rank = jnp.sum(onehot * (earlier + seen), axis=0, keepdims=True)
    assign_ref[0, 0] = jnp.concatenate([eid, rank], axis=0).astype(jnp.int32)
    seen = seen + jnp.sum(onehot, axis=1, keepdims=True)
    carry[...] = seen
    counts_ref[...] = seen


def _merge(streams, mod, a0u, yb, hf_t, hb_t, mvo_t, conv_w, norm_w, w_gate, w_pa, w_pb, w_pc, w_o,
           ln_g, ln_b, w_route, b_route, n_ctx, off, alpha):
    B, _, D = streams[0].shape
    T = a0u.shape[1]
    tm = TOK_TILE
    nct = n_ctx // tm
    nt = T // tm - off
    tn = nt * tm
    hpt = tm // HALO
    nhalo = T // HALO

    ns = MERGE_SAMPLES
    assert B % ns == 0

    def tok(width):
        return pl.BlockSpec((ns, tm, width), lambda b, i: (b, i + off, 0))

    def chan(rows):
        return pl.BlockSpec((ns, rows, tm), lambda b, i: (b, 0, i + off))

    def full(a):
        return pl.BlockSpec(a.shape, lambda b, i: (0,) * a.ndim, pipeline_mode=pl.Buffered(1))

    def otok(width):
        return pl.BlockSpec((ns, tm, width), lambda b, i: (b, i, 0))

    a_i = lax.broadcasted_iota(jnp.int32, (TOP_K * tm, TOP_K * tm), 0)
    b_i = lax.broadcasted_iota(jnp.int32, (TOP_K * tm, TOP_K * tm), 1)
    upper = (a_i < b_i).astype(BF16)
    consts = [conv_w, norm_w, w_gate, w_pa, w_pb, w_pc, w_o, ln_g, ln_b, w_route, b_route, upper]
    return pl.pallas_call(
        functools.partial(_merge_kernel, n_src=len(streams), off=off, n_ctx_tiles=nct, alpha=alpha,
                          n_batch=B),
        grid=(B // ns, nt),
        in_specs=_stream_specs(streams, tm, nct, off, ns) + [
            tok(2 * CONV_DIM),
            pl.BlockSpec((ns, HALO, 2 * CONV_DIM),
                         lambda b, i: (b, jnp.maximum((i + off) * hpt - 1, 0), 0)),
            pl.BlockSpec((ns, HALO, 2 * CONV_DIM),
                         lambda b, i: (b, jnp.minimum((i + off + 1) * hpt, nhalo - 1), 0)),
            tok(ATT_Q_DIM), chan(MLSTM_DIM), chan(MLSTM_DIM),
            pl.BlockSpec((ns, MLSTM_DIM, tm), lambda b, i: (b, 1, i + off)),
            full(mod),
        ] + [full(a) for a in consts],
        out_specs=[otok(D),
                   pl.BlockSpec((ns, tm * ROW_TILE, LANES), lambda b, i: (b, i, 0)),
                   otok(ROW_TILE),
                   pl.BlockSpec((ns, 1, 2, TOP_K * tm), lambda b, i: (b, i, 0, 0)),
                   pl.BlockSpec((N_EXPERTS, 1), lambda b, i: (0, 0))],
        out_shape=[jax.ShapeDtypeStruct((B, tn, D), F32),
                   jax.ShapeDtypeStruct((B, tn * ROW_TILE, LANES), F32),
                   jax.ShapeDtypeStruct((B, tn, ROW_TILE), F32),
                   jax.ShapeDtypeStruct((B, nt, 2, TOP_K * tm), jnp.int32),
                   jax.ShapeDtypeStruct((N_EXPERTS, 1), F32)],
        scratch_shapes=[pltpu.VMEM((N_EXPERTS, 1), F32)],
        compiler_params=_cparams(("arbitrary", "arbitrary")),
        name="merge",
    )(*streams, a0u, a0u, a0u, yb, hf_t, hb_t, mvo_t, mod, *consts)


def _dispatch_kernel(pos_ref, pend_ref, nblk_ref, h_ref, xs_hbm, zbuf, sem, zsem):
    s = pl.program_id(0)
    tm = h_ref.shape[0] // ROW_TILE
    zrows = zbuf.shape[0]
    last_blk = xs_hbm.shape[0] // zrows - 1

    @pl.when(s == 0)
    def _():
        zbuf[...] = jnp.zeros_like(zbuf)

        def zero_copy(start):
            return pltpu.make_async_copy(
                zbuf, xs_hbm.at[pl.ds(pl.multiple_of(start, ROW_TILE), zrows), :], zsem)

        jobs = []
        for e in range(N_EXPERTS):
            before = pend_ref[e - 1] if e > 0 else 0
            jobs.append((pend_ref[e] * ROW_TILE - zrows, pend_ref[e] > before))
        for e in range(N_EXPERTS):
            jobs.append(((nblk_ref[0] + e) * zrows, nblk_ref[0] + e <= last_blk))
        for st, needed in jobs:
            @pl.when(needed)
            def _(st=st):
                zero_copy(st).start()
        for st, needed in jobs:
            @pl.when(needed)
            def _(st=st):
                zero_copy(st).wait()

    def row_copy(r, k):
        src = pl.multiple_of(r * ROW_TILE, ROW_TILE)
        dst = pl.multiple_of(pos_ref[(s * TOP_K + k) * tm + r] * ROW_TILE, ROW_TILE)
        return pltpu.make_async_copy(h_ref.at[pl.ds(src, ROW_TILE), :],
                                     xs_hbm.at[pl.ds(dst, ROW_TILE), :], sem)

    def body(r, carry):
        for k in range(TOP_K):
            row_copy(r, k).start()
        return carry
    lax.fori_loop(0, tm, body, 0, unroll=GATHER_UNROLL)
    for k in range(TOP_K):
        pltpu.make_async_copy(h_ref, xs_hbm.at[pl.ds(0, tm * ROW_TILE), :], sem).wait()


def _dispatch(h2v, pos, pend, nblk, p_rows, tm):
    rows = h2v.shape[0]
    nsteps = rows // (tm * ROW_TILE)
    grid_spec = pltpu.PrefetchScalarGridSpec(
        num_scalar_prefetch=3,
        grid=(nsteps,),
        in_specs=[pl.BlockSpec((tm * ROW_TILE, LANES), lambda s, pos, pend, nbk: (s, 0))],
        out_specs=pl.BlockSpec(memory_space=pl.ANY),
        scratch_shapes=[
            pltpu.VMEM((EXP_BLK * ROW_TILE, LANES), F32),
            pltpu.SemaphoreType.DMA,
            pltpu.SemaphoreType.DMA,
        ],
    )
    return pl.pallas_call(
        _dispatch_kernel,
        grid_spec=grid_spec,
        out_shape=jax.ShapeDtypeStruct((p_rows * ROW_TILE, LANES), F32),
        compiler_params=_cparams(("arbitrary",)),
        name="dispatch",
    )(pos, pend, nblk, h2v)


def _expert_kernel(blke_ref, nblk_ref, xs_ref, wi_ref, wo_ref, y_ref, wi_bf, wo_bf):
    i = pl.program_id(0)
    nb = nblk_ref[0]
    blk = xs_ref.shape[0] // ROW_TILE
    nsl = wi_ref.shape[2] // LANES

    @pl.when(i < nb)
    def _():
        e_now = blke_ref[i]
        e_before = blke_ref[jnp.maximum(i - 1, 0)]

        @pl.when(jnp.logical_or(i == 0, e_now != e_before))
        def _():
            wi_bf[...] = wi_ref[0, 0].astype(BF16)
            wo_bf[...] = wo_ref[0, 0].astype(BF16)

        def rows_part(part):
            r0 = part * EXP_ROWS * ROW_TILE
            xin = jnp.concatenate(
                [xs_ref[pl.ds(r0 + s, EXP_ROWS, stride=ROW_TILE), :] for s in range(nsl)],
                axis=1).astype(BF16)
            yield
            mid = _dot(xin, wi_bf[...])
            yield
            gt = mid[:, 0:D_EXPERT]
            up = mid[:, D_EXPERT:]
            act = ((gt * _sigmoid(gt)) * up).astype(BF16)
            yield
            y = _dot(act, wo_bf[...])
            yield
            for s in range(nsl):
                y_ref[pl.ds(r0 + s, EXP_ROWS, stride=ROW_TILE), :] = y[:, s * LANES:(s + 1) * LANES]

        parts = [rows_part(p) for p in range(blk // EXP_ROWS)]
        for t in range(len(parts) + 4):
            for p, gen in enumerate(parts):
                if 0 <= t - p <= 4:
                    next(gen, None)

    @pl.when(i >= nb)
    def _():
        y_ref[...] = jnp.zeros_like(y_ref)


def _experts(xs, blk_e, nblk, w_ei, w_eo, layer):
    d = w_ei.shape[2]
    blk = EXP_BLK
    nb = xs.shape[0] // (blk * ROW_TILE)
    grid_spec = pltpu.PrefetchScalarGridSpec(
        num_scalar_prefetch=2,
        grid=(nb,),
        in_specs=[
            pl.BlockSpec((blk * ROW_TILE, LANES),
                         lambda i, be, nbk: (jnp.minimum(i, jnp.maximum(nbk[0] - 1, 0)), 0)),
            pl.BlockSpec((1, 1, d, 2 * D_EXPERT), lambda i, be, nbk: (layer, be[i], 0, 0)),
            pl.BlockSpec((1, 1, D_EXPERT, d), lambda i, be, nbk: (layer, be[i], 0, 0)),
        ],
        out_specs=pl.BlockSpec((blk * ROW_TILE, LANES), lambda i, be, nbk: (i, 0)),
        scratch_shapes=[
            pltpu.VMEM((d, 2 * D_EXPERT), BF16),
            pltpu.VMEM((D_EXPERT, d), BF16),
        ],
    )
    return pl.pallas_call(
        _expert_kernel,
        grid_spec=grid_spec,
        out_shape=jax.ShapeDtypeStruct(xs.shape, F32),
        compiler_params=_cparams(("arbitrary",)),
        name="experts",
    )(blk_e, nblk, xs, w_ei, w_eo)


def _combine_kernel(pos_ref, x_ref, mod_ref, wt_ref, ln_g_ref, ln_b_ref, y_hbm, o_ref,
                    ybuf, sem, *, alpha):
    b = pl.program_id(0)
    i = pl.program_id(1)
    nt = pl.num_programs(1)
    tm = x_ref.shape[1]
    step = b * nt + i
    nsteps = pl.num_programs(0) * nt

    def start_gather(s, slot):
        def body(r, carry):
            dst = pl.multiple_of(r * ROW_TILE, ROW_TILE)
            for k in range(TOP_K):
                src = pl.multiple_of(pos_ref[(s * TOP_K + k) * tm + r] * ROW_TILE, ROW_TILE)
                pltpu.make_async_copy(y_hbm.at[pl.ds(src, ROW_TILE), :],
                                      ybuf.at[slot, k, pl.ds(dst, ROW_TILE), :], sem.at[slot]).start()
            return carry
        lax.fori_loop(0, tm, body, 0, unroll=GATHER_UNROLL)

    @pl.when(step == 0)
    def _():
        start_gather(0, 0)

    @pl.when(step + 1 < nsteps)
    def _():
        start_gather(step + 1, (step + 1) % 2)

    slot = step % 2
    for k in range(TOP_K):
        pltpu.make_async_copy(y_hbm.at[pl.ds(0, tm * ROW_TILE), :], ybuf.at[slot, k],
                              sem.at[slot]).wait()
    wt = wt_ref[0]
    w0 = wt[:, 2:3]
    w1 = wt[:, 3:4]
    f = jnp.concatenate(
        [w0 * ybuf[slot, 0, pl.ds(j, tm, stride=ROW_TILE), :]
         + w1 * ybuf[slot, 1, pl.ds(j, tm, stride=ROW_TILE), :]
         for j in range(x_ref.shape[2] // LANES)], axis=1)
    x = x_ref[0]
    o_ref[0] = _ln(alpha * x + mod_ref[0, 5:6, :] * f) * ln_g_ref[...] + ln_b_ref[...]


def _combine(x1, mod, wts, pos, y, ln_g, ln_b, n_ctx_tiles, alpha):
    B, tn, D = x1.shape
    tm = TOK_TILE
    nt = tn // tm
    grid_spec = pltpu.PrefetchScalarGridSpec(
        num_scalar_prefetch=1,
        grid=(B, nt),
        in_specs=[
            pl.BlockSpec((1, tm, D), lambda b, i, pos: (b, i, 0)),
            pl.BlockSpec((1, 6, D), lambda b, i, pos: (jnp.where(i < n_ctx_tiles, B, b), 0, 0)),
            pl.BlockSpec((1, tm, ROW_TILE), lambda b, i, pos: (b, i, 0)),
            pl.BlockSpec((1, D), lambda b, i, pos: (0, 0)),
            pl.BlockSpec((1, D), lambda b, i, pos: (0, 0)),
            pl.BlockSpec(memory_space=pl.ANY),
        ],
        out_specs=pl.BlockSpec((1, tm, D), lambda b, i, pos: (b, i, 0)),
        scratch_shapes=[
            pltpu.VMEM((2, TOP_K, tm * ROW_TILE, LANES), F32),
            pltpu.SemaphoreType.DMA((2,)),
        ],
    )
    return pl.pallas_call(
        functools.partial(_combine_kernel, alpha=alpha),
        grid_spec=grid_spec,
        out_shape=jax.ShapeDtypeStruct((B, tn, D), F32),
        compiler_params=_cparams(("arbitrary", "arbitrary")),
        name="combine",
    )(pos, x1, mod, wts, ln_g, ln_b, y)


def _segments(counts, assign):
    counts = counts.reshape(N_EXPERTS).astype(jnp.int32)
    n_assign = assign.shape[0] * assign.shape[2]
    padded = (counts + EXP_BLK - 1) // EXP_BLK * EXP_BLK
    pad_end = jnp.cumsum(padded)
    pad_start = pad_end - padded
    p_rows = n_assign + N_EXPERTS * EXP_BLK
    nb = p_rows // EXP_BLK
    blk_first = jnp.arange(nb, dtype=jnp.int32) * EXP_BLK
    blk_e = jnp.minimum(jnp.sum((pad_end[None, :] <= blk_first[:, None]).astype(jnp.int32), axis=1),
                        N_EXPERTS - 1).astype(jnp.int32)
    nblk = (pad_end[-1] // EXP_BLK).astype(jnp.int32).reshape(1)
    eid, rank = assign[:, 0, :], assign[:, 1, :]
    onehot = eid[:, :, None] == jnp.arange(N_EXPERTS, dtype=jnp.int32)
    pos = rank + jnp.sum(jnp.where(onehot, pad_start.astype(jnp.int32), 0), axis=-1)
    return pos.reshape(-1).astype(jnp.int32), pad_end.astype(jnp.int32), blk_e, nblk, p_rows


def _rope_tables(n_ctx, n_lat):
    nf = ATT_HEAD_DIM // 4
    inv = ROPE_BASE ** (-jnp.arange(nf, dtype=F32) / nf)
    rows = n_lat // GRID_W
    pos_r = jnp.repeat(jnp.arange(rows, dtype=F32), GRID_W)
    pos_c = jnp.tile(jnp.arange(GRID_W, dtype=F32), rows)
    ang_r = pos_r[:, None] * inv
    ang_c = pos_c[:, None] * inv
    cos_h = jnp.concatenate([jnp.cos(ang_r)] * 2 + [jnp.cos(ang_c)] * 2, axis=-1)
    sin_h = jnp.concatenate([-jnp.sin(ang_r), jnp.sin(ang_r),
                             -jnp.sin(ang_c), jnp.sin(ang_c)], axis=-1)
    reps = LANES // ATT_HEAD_DIM
    cos_l = jnp.tile(cos_h, (1, reps))
    sin_l = jnp.tile(sin_h, (1, reps))
    cos_t = jnp.concatenate([jnp.ones((n_ctx, LANES), F32), cos_l], axis=0)
    sin_t = jnp.concatenate([jnp.zeros((n_ctx, LANES), F32), sin_l], axis=0)
    return cos_t, sin_t


def _projection_weights(w_in_l):
    offs = np.cumsum((0, 3 * CONV_DIM, ATT_Q_DIM, ATT_KV_DIM, ATT_KV_DIM,
                      MLSTM_DIM, MLSTM_DIM, MLSTM_DIM, MLSTM_DIM, N_GATE)).tolist()
    a, q, k, v, mq, mk, mv, mo, g = [w_in_l[:, offs[n]:offs[n + 1]] for n in range(9)]
    hd = ATT_HEAD_DIM

    def swap(w):
        return jnp.concatenate([w[:, hd:], w[:, :hd]], axis=1)

    w_tok = jnp.concatenate([a, q, k, swap(k), v, swap(v), mq, mk], axis=1).astype(BF16)
    w_chan = jnp.concatenate([mv, mo, g], axis=1).T.astype(BF16)
    w_gate = w_in_l[:, offs[9]:].astype(BF16)
    return w_tok, w_chan, w_gate


def kernel(x, c, ctx, c_ctx, w_ada, b_ada, w_in, conv_w, attn_sink, mlstm_gate_b, mlstm_norm_w,
           w_proj_a, w_proj_b, w_proj_c, w_out, ln1_g, ln1_b, w_route_group, b_route_group,
           w_route_expert, b_route_expert, w_expert_in, w_expert_out, ln2_g, ln2_b):
    B, L, D = x.shape
    n_ctx = ctx.shape[1]
    depth = w_ada.shape[0]
    T = n_ctx + L
    alpha = (2 * depth) ** 0.25
    assert D == D_MODEL and n_ctx % TOK_TILE == 0 and L % TOK_TILE == 0 and L % GRID_W == 0
    assert MLSTM_CH == LANES and ATT_QB == LANES
    nct = n_ctx // TOK_TILE

    nrows = -(-(B + 1) // 8) * 8
    cond = jnp.concatenate([c, c_ctx[None, :], jnp.zeros((nrows - B - 1, D), F32)], axis=0)
    mod_all = _ada(cond, w_ada, b_ada).reshape(depth, nrows, 6, D)

    cos_t, sin_t = _rope_tables(n_ctx, L)
    streams = (ctx, x)

    for i in range(depth):
        need_ctx = i < depth - 1
        mod = mod_all[i]
        w_tok, w_chan, w_gate = _projection_weights(w_in[i])

        a0u, q, kv, mqk, mvo_t, gcol, grow = _inproj(streams, mod, w_tok, w_chan, cos_t, sin_t, n_ctx)
        yb = _attention(q, kv, attn_sink[i], n_ctx)
        hf_t, hb_t = _mlstm(mqk, mvo_t, gcol, grow, mlstm_gate_b[i], n_ctx)

        off = 0 if need_ctx else nct
        w_route = jnp.pad(jnp.concatenate([w_route_group[i], w_route_expert[i]], axis=1).T,
                          ((0, LANES - N_ROUTE), (0, 0)))
        b_route = jnp.pad(jnp.concatenate([b_route_group[i], b_route_expert[i]]),
                          (0, LANES - N_ROUTE)).reshape(LANES, 1)
        x1, h2v, wts, assign, counts = _merge(
            streams, mod, a0u, yb, hf_t, hb_t, mvo_t, conv_w[i], mlstm_norm_w[i].reshape(MLSTM_DIM, 1),
            w_gate, w_proj_a[i].astype(BF16), w_proj_b[i].astype(BF16), w_proj_c[i].astype(BF16),
            w_out[i].astype(BF16), ln1_g[i].reshape(1, D), ln1_b[i].reshape(1, D),
            w_route, b_route, n_ctx, off, alpha)

        tn = x1.shape[1]
        pos, pend, blk_e, nblk, p_rows = _segments(counts, assign.reshape(-1, 2, TOP_K * TOK_TILE))
        xs = _dispatch(h2v.reshape(-1, LANES), pos, pend, nblk, p_rows, TOK_TILE)
        y = _experts(xs, blk_e, nblk, w_expert_in, w_expert_out, i)
        streams = (_combine(x1, mod, wts, pos, y,
                            ln2_g[i].reshape(1, D), ln2_b[i].reshape(1, D),
                            nct if need_ctx else 0, alpha),)
    return streams[0]
```

```python
import functools

import jax
import jax.numpy as jnp
import numpy as np
from jax import lax
from jax.experimental import pallas as pl
from jax.experimental.pallas import tpu as pltpu

D_MODEL = 1024
GRID_W = 64
CONV_DIM = 256
ATT_HEADS = 8
ATT_KV_HEADS = 2
ATT_HEAD_DIM = 64
ATT_WINDOW = 128
ROPE_BASE = 10000.0
MLSTM_HEADS = 4
MLSTM_HEAD_DIM = 64
MLSTM_DIM = MLSTM_HEADS * MLSTM_HEAD_DIM
N_GROUPS = 4
EXPERTS_PER_GROUP = 8
N_EXPERTS = N_GROUPS * EXPERTS_PER_GROUP
TOP_K = 2
D_EXPERT = D_MODEL // 2
LN_EPS = 1e-6
NEG_INF = -1e30

ATT_Q_DIM = ATT_HEADS * ATT_HEAD_DIM
ATT_KV_DIM = ATT_KV_HEADS * ATT_HEAD_DIM
N_GATE = 4 * MLSTM_HEADS
MIX_COLS = 3 * CONV_DIM + ATT_Q_DIM + 2 * ATT_KV_DIM + 4 * MLSTM_DIM
N_ROUTE = N_GROUPS + N_EXPERTS

LANES = 128
VMEM_LIMIT = 56 * 1024 * 1024
TOK_TILE = 256
PAIR_SAMPLES = 4
MERGE_SAMPLES = 2
ATT_QB = 128
ATT_STEP = 256
ATT_LAG_SOFTMAX = 1
ATT_LAG_VALUES = 2
MLSTM_CH = 128
MLSTM_STEP = 256
EXP_BLK = 512
EXP_ROWS = 256
HALO = 16
ROW_TILE = 8
GATHER_UNROLL = 8

F32 = jnp.float32
BF16 = jnp.bfloat16


def _cparams(sem):
    return pltpu.CompilerParams(dimension_semantics=sem, vmem_limit_bytes=VMEM_LIMIT)


def _ln(x, axis=-1):
    mu = jnp.mean(x, axis=axis, keepdims=True)
    xc = x - mu
    var = jnp.mean(xc * xc, axis=axis, keepdims=True)
    return xc * lax.rsqrt(var + LN_EPS)


def _sigmoid(x):
    return 0.5 * jnp.tanh(0.5 * x) + 0.5


def _split3(x):
    hi = x.astype(BF16)
    r1 = x - hi.astype(F32)
    mid = r1.astype(BF16)
    lo = (r1 - mid.astype(F32)).astype(BF16)
    return hi, mid, lo


def _dot(a, b):
    return jnp.dot(a, b, preferred_element_type=F32)


def _dot_nt(a, b):
    return lax.dot_general(a, b, (((1,), (1,)), ((), ())), preferred_element_type=F32)


def _dot3(x, rhs_b):
    return sum(_dot(part, rhs_b) for part in _split3(x))


def _dot3_l(lhs_b, x):
    return sum(_dot(lhs_b, part) for part in _split3(x))


def _ada_kernel(c_ref, w_ref, b_ref, o_ref):
    cv = c_ref[...]
    s = cv * _sigmoid(cv)
    o_ref[0] = _dot(s.astype(BF16), w_ref[0].astype(BF16)) + b_ref[0]


def _ada(cond, w_ada, b_ada):
    depth, d, n6 = w_ada.shape
    rows = cond.shape[0]
    nt = n6 // d
    return pl.pallas_call(
        _ada_kernel,
        grid=(depth, nt),
        in_specs=[
            pl.BlockSpec((rows, d), lambda l, j: (0, 0)),
            pl.BlockSpec((1, d, d), lambda l, j: (l, 0, j)),
            pl.BlockSpec((1, 1, d), lambda l, j: (l, 0, j)),
        ],
        out_specs=pl.BlockSpec((1, rows, d), lambda l, j: (l, 0, j)),
        out_shape=jax.ShapeDtypeStruct((depth, rows, n6), F32),
        compiler_params=_cparams(("arbitrary", "arbitrary")),
        name="ada_mod",
    )(cond, w_ada, b_ada.reshape(depth, 1, n6))


def _rope(x, cos, sin_signed, lane):
    swapped = jnp.where(lane % 32 < 16,
                        pltpu.roll(x, LANES - 16, axis=1),
                        pltpu.roll(x, 16, axis=1))
    return x * cos + swapped * sin_signed


_OFF_A = 0
_OFF_Q = _OFF_A + 3 * CONV_DIM
_OFF_K = _OFF_Q + ATT_Q_DIM
_OFF_V = _OFF_K + 2 * ATT_KV_DIM
_OFF_MQK = _OFF_V + 2 * ATT_KV_DIM
_W_TOK_COLS = _OFF_MQK + 2 * MLSTM_DIM


def _stream_specs(streams, tm, nct, off=0, nb=1):
    d = streams[0].shape[2]
    if len(streams) == 1:
        return [pl.BlockSpec((nb, tm, d), lambda b, i, *_: (b, i + off, 0))]
    return [pl.BlockSpec((nb, tm, d), lambda b, i, *_: (b, jnp.minimum(i + off, nct - 1), 0)),
            pl.BlockSpec((nb, tm, d), lambda b, i, *_: (b, jnp.maximum(i + off - nct, 0), 0))]


def _stream_tile(refs, i, nct):
    if len(refs) == 1:
        return refs[0][0]
    return jnp.where(i < nct, refs[0][0], refs[1][0])


def _inproj_kernel(*refs, n_src, nct, n_batch):
    mod_ref = refs[n_src]
    shared = refs[n_src + 1:n_src + 5]
    outs = refs[n_src + 5:]
    is_ctx = pl.program_id(1) < nct
    tiles = []
    for smp in range(PAIR_SAMPLES):
        row = jnp.where(is_ctx, n_batch, pl.program_id(0) * PAIR_SAMPLES + smp)
        tiles.append(_inproj_tile(
            *[r.at[pl.ds(smp, 1)] for r in refs[:n_src]], mod_ref.at[pl.ds(row, 1)], *shared,
            *[r.at[pl.ds(smp, 1)] for r in outs], n_src=n_src, nct=nct))
    live = list(tiles)
    while live:
        for g in list(live):
            if next(g, live) is live:
                live.remove(g)


def _inproj_tile(*refs, n_src, nct):
    (mod_ref, w_ref, wt_ref, cos_ref, sin_ref,
     a0u_ref, q_ref, kv_ref, mqk_ref, mvo_t_ref, gcol_ref, grow_ref) = refs[n_src:]
    x = _stream_tile(refs[:n_src], pl.program_id(1), nct)
    shift = mod_ref[0, 0:1, :]
    scale = mod_ref[0, 1:2, :]
    h = (_ln(x) * (1.0 + scale) + shift).astype(BF16)
    cos = cos_ref[...]
    sin = sin_ref[...]
    lane = lax.broadcasted_iota(jnp.int32, cos.shape, 1)
    yield

    def cols(lo, n):
        return _dot(h, w_ref[:, lo:lo + n])

    za = cols(_OFF_A, 3 * CONV_DIM)
    yield
    a0u_ref[0, :, 0:CONV_DIM] = za[:, 0:CONV_DIM].astype(BF16)
    a0u_ref[0, :, CONV_DIM:2 * CONV_DIM] = (
        za[:, CONV_DIM:2 * CONV_DIM] * za[:, 2 * CONV_DIM:3 * CONV_DIM]).astype(BF16)
    zq = cols(_OFF_Q, ATT_Q_DIM)
    yield
    qscale = ATT_HEAD_DIM ** -0.5
    for j in range(ATT_Q_DIM // LANES):
        piece = _rope(zq[:, j * LANES:(j + 1) * LANES], cos, sin, lane)
        q_ref[0, :, j * LANES:(j + 1) * LANES] = (piece * qscale).astype(BF16)
    zk = cols(_OFF_K, 2 * ATT_KV_DIM)
    yield
    for j in range(2):
        kv_ref[0, :, j * LANES:(j + 1) * LANES] = _rope(
            zk[:, j * LANES:(j + 1) * LANES], cos, sin, lane).astype(BF16)
    kv_ref[0, :, 2 * LANES:] = cols(_OFF_V, 2 * ATT_KV_DIM).astype(BF16)
    yield
    zm = cols(_OFF_MQK, 2 * MLSTM_DIM)
    yield
    mqk_ref[0, :, 0:MLSTM_DIM] = zm[:, 0:MLSTM_DIM].astype(BF16)
    mqk_ref[0, :, MLSTM_DIM:] = (zm[:, MLSTM_DIM:] * (MLSTM_HEAD_DIM ** -0.5)).astype(BF16)
    nchan = mvo_t_ref.shape[1]
    zt = _dot_nt(wt_ref[...], h)
    yield
    mvo_t_ref[0] = zt[0:nchan, :].astype(BF16)
    grow = zt[nchan:, :]
    grow_ref[0] = grow
    gcol_ref[0] = grow.T


def _inproj(streams, mod, w_tok, w_chan, cos_t, sin_t, n_ctx):
    B, _, D = streams[0].shape
    T = sum(s.shape[1] for s in streams) if len(streams) > 1 else streams[0].shape[1]
    tm = TOK_TILE
    nct = n_ctx // tm
    nchan = w_chan.shape[0] - N_GATE
    ns = PAIR_SAMPLES
    assert B % ns == 0

    def tok(width):
        return pl.BlockSpec((ns, tm, width), lambda b, i: (b, i, 0))

    def chan(rows):
        return pl.BlockSpec((ns, rows, tm), lambda b, i: (b, 0, i))

    outs = [(2 * CONV_DIM, BF16), (ATT_Q_DIM, BF16), (4 * ATT_KV_DIM, BF16), (2 * MLSTM_DIM, BF16)]
    return pl.pallas_call(
        functools.partial(_inproj_kernel, n_src=len(streams), nct=nct, n_batch=B),
        grid=(B // ns, T // tm),
        in_specs=_stream_specs(streams, tm, nct, 0, ns) + [
            pl.BlockSpec(mod.shape, lambda b, i: (0, 0, 0), pipeline_mode=pl.Buffered(1)),
            pl.BlockSpec(w_tok.shape, lambda b, i: (0, 0), pipeline_mode=pl.Buffered(1)),
            pl.BlockSpec(w_chan.shape, lambda b, i: (0, 0), pipeline_mode=pl.Buffered(1)),
            pl.BlockSpec((tm, LANES), lambda b, i: (i, 0)),
            pl.BlockSpec((tm, LANES), lambda b, i: (i, 0)),
        ],
        out_specs=[tok(w) for w, _ in outs] + [
            chan(nchan), tok(N_GATE), chan(N_GATE)],
        out_shape=[jax.ShapeDtypeStruct((B, T, w), dt) for w, dt in outs] + [
            jax.ShapeDtypeStruct((B, nchan, T), BF16),
            jax.ShapeDtypeStruct((B, T, N_GATE), F32), jax.ShapeDtypeStruct((B, N_GATE, T), F32)],
        compiler_params=_cparams(("parallel", "arbitrary")),
        name="in_proj",
    )(*streams, mod, w_tok, w_chan, cos_t, sin_t)


def _attn_kernel(sink_ref, q_ref, kvc_ref, kvp_ref, kvm_ref, kvn_ref, o_ref, *, n_ctx_blk, n_blk):
    n_ctx = kvc_ref.shape[1]
    qb = kvp_ref.shape[1]
    nsub = q_ref.shape[1] // qb
    nk = n_ctx + 3 * qb
    half = LANES // 2

    lane_row = lax.broadcasted_iota(jnp.int32, (1, LANES), 1)
    keep = [(lane_row < half).astype(F32).astype(BF16), (lane_row >= half).astype(F32).astype(BF16)]
    pad_v = jnp.zeros((qb, LANES), BF16)
    qi = lax.broadcasted_iota(jnp.int32, (qb, qb), 0)
    ki = lax.broadcasted_iota(jnp.int32, (qb, qb), 1)
    lane_q = lax.broadcasted_iota(jnp.int32, (qb, LANES), 1)
    neg = jnp.full((qb, LANES), NEG_INF, F32)
    group = ATT_HEADS // ATT_KV_HEADS
    blocks = []
    for blk_i in range(q_ref.shape[0] * nsub):
        smp, sb = divmod(blk_i, nsub)
        around = ([kvp_ref[smp]] + [kvm_ref[smp, n * qb:(n + 1) * qb, :] for n in range(nsub)]
                  + [kvn_ref[smp]])
        g = pl.program_id(1) * nsub + sb
        is_lat = g >= n_ctx_blk
        has_prev = g >= n_ctx_blk + 1
        has_next = g < n_blk - 1
        kv_all = jnp.concatenate([kvc_ref[smp]] + around[sb:sb + 3], axis=0)
        k_ext, v_ext = {}, {}
        for tile in range(2):
            for par in range(2):
                k_ext[tile, par] = kv_all[:, tile * LANES:(tile + 1) * LANES] * keep[par]
                v = jnp.concatenate([kv_all[:, (2 + tile) * LANES:(3 + tile) * LANES], pad_v], axis=0)
                v_ext[tile, par] = jnp.concatenate(
                    [v * keep[par], jnp.broadcast_to(keep[par], v.shape)], axis=1)
        blocks.append(dict(
            is_lat=is_lat, ok_prev=jnp.logical_and(ki >= qi, has_prev),
            ok_next=jnp.logical_and(ki <= qi, jnp.logical_and(has_next, is_lat)),
            k_ext=k_ext, v_ext=v_ext, q=q_ref[smp, sb * qb:(sb + 1) * qb, :]))

    jobs = [(sb, pair, par) for sb in range(len(blocks)) for pair in range(ATT_HEADS // 2)
            for par in range(2)]
    scores, probs, accs, outs = {}, {}, {}, {}

    def operands(job):
        sb, pair, par = job
        kvh = (2 * pair) // group
        tile = 0 if kvh == par else 1
        return blocks[sb], tile

    def stage_scores(job):
        sb, pair, par = job
        blk, tile = operands(job)
        scores[job] = _dot_nt(blk["q"][:, pair * LANES:(pair + 1) * LANES], blk["k_ext"][tile, par])

    def stage_softmax(job):
        sb, pair, par = job
        blk, _ = operands(job)
        s = scores.pop(job)
        s_ext = jnp.concatenate([
            s[:, 0:n_ctx],
            jnp.where(blk["ok_prev"], s[:, n_ctx:n_ctx + qb], neg),
            jnp.where(blk["is_lat"], s[:, n_ctx + qb:n_ctx + 2 * qb], neg),
            jnp.where(blk["ok_next"], s[:, n_ctx + 2 * qb:nk], neg),
            jnp.where(lane_q == 0, sink_ref[2 * pair + par], neg)], axis=1)
        m = jnp.max(s_ext, axis=-1, keepdims=True)
        probs[job] = jnp.exp(s_ext - m).astype(BF16)

    def stage_values(job):
        sb, pair, par = job
        blk, tile = operands(job)
        part = _dot(probs.pop(job), blk["v_ext"][tile, par])
        if par == 0:
            accs[sb, pair] = part
        else:
            acc = accs.pop((sb, pair)) + part
            outs[sb, pair] = acc[:, 0:LANES] / acc[:, LANES:]

    for t in range(len(jobs) + ATT_LAG_VALUES):
        if t < len(jobs):
            stage_scores(jobs[t])
        if 0 <= t - ATT_LAG_SOFTMAX < len(jobs):
            stage_softmax(jobs[t - ATT_LAG_SOFTMAX])
        if 0 <= t - ATT_LAG_VALUES < len(jobs):
            stage_values(jobs[t - ATT_LAG_VALUES])
    for blk_i in range(len(blocks)):
        smp, sb = divmod(blk_i, nsub)
        o_ref[smp, sb * qb:(sb + 1) * qb, :] = jnp.concatenate(
            [outs[blk_i, pair] for pair in range(ATT_HEADS // 2)], axis=-1).astype(o_ref.dtype)


def _attention(q, kv, sink, n_ctx):
    B, T, _ = q.shape
    qb = ATT_QB
    nsub = ATT_STEP // qb
    n_blk = T // qb
    nq = T // ATT_STEP
    n_ctx_blk = n_ctx // qb
    kvw = kv.shape[2]
    ns = PAIR_SAMPLES
    assert B % ns == 0
    grid_spec = pltpu.PrefetchScalarGridSpec(
        num_scalar_prefetch=1,
        grid=(B // ns, nq),
        in_specs=[
            pl.BlockSpec((ns, ATT_STEP, ATT_Q_DIM), lambda b, j, sk: (b, j, 0)),
            pl.BlockSpec((ns, n_ctx, kvw), lambda b, j, sk: (b, 0, 0)),
            pl.BlockSpec((ns, qb, kvw), lambda b, j, sk: (b, jnp.maximum(j * nsub - 1, 0), 0)),
            pl.BlockSpec((ns, ATT_STEP, kvw), lambda b, j, sk: (b, j, 0)),
            pl.BlockSpec((ns, qb, kvw), lambda b, j, sk: (b, jnp.minimum((j + 1) * nsub, n_blk - 1), 0)),
        ],
        out_specs=pl.BlockSpec((ns, ATT_STEP, ATT_Q_DIM), lambda b, j, sk: (b, j, 0)),
    )
    return pl.pallas_call(
        functools.partial(_attn_kernel, n_ctx_blk=n_ctx_blk, n_blk=n_blk),
        grid_spec=grid_spec,
        out_shape=jax.ShapeDtypeStruct((B, T, ATT_Q_DIM), BF16),
        compiler_params=_cparams(("parallel", "arbitrary")),
        name="window_attn",
    )(sink, q, kv, kv, kv, kv)


def _log_sigmoid(x):
    return jnp.minimum(x, 0.0) - jnp.log1p(jnp.exp(-jnp.abs(x)))


def _rows_to_lanes(a, base):
    return jnp.concatenate([a[base + h:base + h + 1, :] for h in range(MLSTM_HEADS)], axis=1)


def _mlstm_gates(d, gcol, grow):
    ch = gcol.shape[0]
    nh = MLSTM_HEADS
    wide = nh * ch
    fwd = d == 0

    r_i = lax.broadcasted_iota(jnp.int32, (ch, ch), 0)
    c_i = lax.broadcasted_iota(jnp.int32, (ch, ch), 1)
    seen_t = (r_i <= c_i) if fwd else (r_i >= c_i)
    seen_tt = (c_i <= r_i) if fwd else (c_i >= r_i)
    r_w = lax.broadcasted_iota(jnp.int32, (ch, wide), 0)
    s_w = lax.broadcasted_iota(jnp.int32, (ch, wide), 1) % ch
    seen_w = (r_w <= s_w) if fwd else (r_w >= s_w)

    base_i = 2 * d * nh
    base_f = base_i + nh

    lf_col = _log_sigmoid(gcol)
    cum_col = _dot3_l(seen_tt.astype(BF16), lf_col)
    lane16 = lax.broadcasted_iota(jnp.int32, gcol.shape, 1)
    z = jnp.where(jnp.logical_and(lane16 >= base_i, lane16 < base_f), gcol, -cum_col)
    ch16 = lax.broadcasted_iota(jnp.int32, (N_GATE, wide), 0)
    hd16 = lax.broadcasted_iota(jnp.int32, (N_GATE, wide), 1) // ch
    sel = jnp.logical_or(ch16 == base_i + hd16, ch16 == base_f + hd16).astype(BF16)
    x_t = _dot3(z, sel)

    lf_row = _log_sigmoid(grow)
    rhs2 = jnp.concatenate([seen_t.astype(BF16), jnp.ones((ch, ch), BF16)], axis=1)
    rows = _dot3(lf_row, rhs2)
    b_all = _rows_to_lanes(rows[:, 0:ch], base_f)
    g_all = _rows_to_lanes(rows[:, ch:], base_f)
    li_all = _rows_to_lanes(grow, base_i)
    yield
    dmat = jnp.where(seen_w, x_t + b_all, -jnp.inf)
    return (dmat, jnp.max(dmat, axis=0, keepdims=True), jnp.max(x_t, axis=0, keepdims=True),
            b_all, g_all, li_all)


def _mlstm_dir(qk, v_t, gates, ct_bd, n_bd, m_prev):
    dmat, dmat_max, x_max, b_all, g_all, li_all = gates
    ch = qk.shape[0]
    nh = MLSTM_HEADS
    dh = MLSTM_HEAD_DIM
    a = b_all + m_prev
    m = jnp.maximum(a, dmat_max)
    w_intra = jnp.exp(dmat - m)
    e_inter = jnp.exp(a - m)
    yield

    q = qk[:, 0:MLSTM_DIM]
    k = qk[:, MLSTM_DIM:]
    lb = lax.broadcasted_iota(jnp.int32, (1, MLSTM_DIM), 1) // dh
    q_bd = jnp.concatenate([q * (lb == h).astype(F32).astype(BF16) for h in range(nh)], axis=0)
    s_t = _dot_nt(k, q_bd) * w_intra
    nq = _rows_to_lanes(_dot_nt(n_bd.astype(BF16), q), 0)
    den = jnp.sum(s_t, axis=0, keepdims=True) + e_inter * nq
    inv = 1.0 / jnp.maximum(jnp.abs(den), jnp.exp(-m))
    inter_t = _dot_nt(ct_bd.astype(BF16), q)
    s_b = s_t.astype(BF16)
    yield
    outs = []
    for h in range(nh):
        seg = slice(h * ch, (h + 1) * ch)
        blk = slice(h * dh, (h + 1) * dh)
        num = _dot(v_t[blk, :], s_b[:, seg]) + e_inter[:, seg] * inter_t[blk, :]
        outs.append(num * inv[:, seg])
    h_t = jnp.concatenate(outs, axis=0)
    yield

    m_loc = g_all + x_max
    m_new = jnp.maximum(g_all + m_prev, m_loc)
    sa = jnp.exp(g_all + m_prev - m_new)
    sb = jnp.exp(m_loc - m_new)
    e_loc = jnp.exp(g_all - b_all + li_all - m_loc)
    v_e = jnp.concatenate(
        [v_t[h * dh:(h + 1) * dh, :].astype(F32) * e_loc[:, h * ch:(h + 1) * ch] for h in range(nh)],
        axis=0).astype(BF16)
    ct_loc = _dot(v_e, k)
    yield
    reps = MLSTM_DIM // ch

    def per_head_rows(row, nrows):
        return jnp.concatenate(
            [jnp.broadcast_to(jnp.concatenate([row[:, h * ch:(h + 1) * ch]] * reps, axis=1),
                              (nrows, MLSTM_DIM)) for h in range(nh)], axis=0)

    eb = lax.broadcasted_iota(jnp.int32, (MLSTM_DIM, MLSTM_DIM), 0) // dh
    db = lax.broadcasted_iota(jnp.int32, (MLSTM_DIM, MLSTM_DIM), 1) // dh
    ct_new = jnp.where(eb == db, per_head_rows(sa, dh) * ct_bd + per_head_rows(sb, dh) * ct_loc, 0.0)
    nrow = n_bd.shape[0]
    e_rows = jnp.concatenate([e_loc[:, h * ch:(h + 1) * ch] for h in range(nh)]
                             + [jnp.zeros((nrow - nh, ch), F32)], axis=0).astype(BF16)
    n_loc = _dot(e_rows, k)
    pad = jnp.zeros((nrow - nh, MLSTM_DIM), F32)
    sa8 = jnp.concatenate([per_head_rows(sa, 1), pad], axis=0)
    sb8 = jnp.concatenate([per_head_rows(sb, 1), pad], axis=0)
    hb8 = lax.broadcasted_iota(jnp.int32, (nrow, MLSTM_DIM), 0)
    db8 = lax.broadcasted_iota(jnp.int32, (nrow, MLSTM_DIM), 1) // dh
    n_new = jnp.where(hb8 == db8, sa8 * n_bd + sb8 * n_loc, 0.0)
    return h_t, ct_new, n_new, m_new


def _mlstm_kernel(qk_f_ref, qk_b_ref, vt_f_ref, vt_b_ref, gc_f_ref, gc_b_ref, gr_f_ref, gr_b_ref,
                  bcol_ref, brow_ref, hf_ref, hb_ref, c_scr, n_scr, m_scr):
    j = pl.program_id(1)

    @pl.when(j == 0)
    def _():
        c_scr[...] = jnp.zeros_like(c_scr)
        n_scr[...] = jnp.zeros_like(n_scr)
        m_scr[...] = jnp.zeros_like(m_scr)

    nscan = c_scr.shape[0]
    states = [(c_scr[k], n_scr[k], m_scr[k]) for k in range(nscan)]
    dirs = ((qk_f_ref, vt_f_ref, gc_f_ref, gr_f_ref, hf_ref),
            (qk_b_ref, vt_b_ref, gc_b_ref, gr_b_ref, hb_ref))
    ch = MLSTM_CH
    nsub = qk_f_ref.shape[1] // ch

    def scan(k):
        d = k % 2
        qk_ref, vt_ref, gc_ref, gr_ref, out_ref = [r.at[pl.ds(k // 2, 1)] for r in dirs[d]]
        st = states[k]
        order = [slice(sc * ch, (sc + 1) * ch)
                 for sc in (range(nsub) if d == 0 else reversed(range(nsub)))]
        gates = []
        for ts in order:
            gates.append((yield from _mlstm_gates(
                d, gc_ref[0, ts, :] + bcol_ref[...], gr_ref[0, :, ts] + brow_ref[...])))
        for ts, gt in zip(order, gates):
            h_t, *st = yield from _mlstm_dir(qk_ref[0, ts, :], vt_ref[0, :, ts], gt, *st)
            out_ref[0, :, ts] = h_t
        states[k] = st

    live = [scan(k) for k in range(nscan)]
    while live:
        for g in list(live):
            if next(g, live) is live:
                live.remove(g)
    for k in range(nscan):
        c_scr[k] = states[k][0]
        n_scr[k] = states[k][1]
        m_scr[k] = states[k][2]


def _mlstm(mqk, mvo_t, gcol, grow, gate_b, n_ctx):
    B, T, _ = mqk.shape
    ch = MLSTM_STEP
    nc = T // ch
    ncc = n_ctx // ch
    ns = PAIR_SAMPLES
    assert B % ns == 0

    def rev(j):
        return jnp.where(j < ncc, ncc - 1 - j, nc - 1 - (j - ncc))

    def tok(width, order):
        return pl.BlockSpec((ns, ch, width), lambda b, j: (b, order(j), 0))

    def chan(rows, order):
        return pl.BlockSpec((ns, rows, ch), lambda b, j: (b, 0, order(j)))

    ident = lambda j: j
    return pl.pallas_call(
        _mlstm_kernel,
        grid=(B // ns, nc),
        in_specs=[
            tok(2 * MLSTM_DIM, ident), tok(2 * MLSTM_DIM, rev),
            chan(MLSTM_DIM, ident), chan(MLSTM_DIM, rev),
            tok(N_GATE, ident), tok(N_GATE, rev),
            chan(N_GATE, ident), chan(N_GATE, rev),
            pl.BlockSpec((1, N_GATE), lambda b, j: (0, 0)),
            pl.BlockSpec((N_GATE, 1), lambda b, j: (0, 0)),
        ],
        out_specs=[chan(MLSTM_DIM, ident), chan(MLSTM_DIM, rev)],
        out_shape=[jax.ShapeDtypeStruct((B, MLSTM_DIM, T), F32)] * 2,
        scratch_shapes=[
            pltpu.VMEM((2 * ns, MLSTM_DIM, MLSTM_DIM), F32),
            pltpu.VMEM((2 * ns, 8, MLSTM_DIM), F32),
            pltpu.VMEM((2 * ns, 1, MLSTM_HEADS * MLSTM_CH), F32),
        ],
        compiler_params=_cparams(("parallel", "arbitrary")),
        name="mlstm",
    )(mqk, mqk, mvo_t, mvo_t, gcol, gcol, grow, grow,
      gate_b.reshape(1, N_GATE), gate_b.reshape(N_GATE, 1))


def _merge_kernel(*refs, n_src, off, n_ctx_tiles, alpha, n_batch):
    n_in = n_src + 7
    mod_ref = refs[n_in]
    consts = refs[n_in + 1:n_in + 13]
    outs = refs[n_in + 13:n_in + 17]
    counts_ref, carry = refs[n_in + 17], refs[n_in + 18]

    @pl.when(jnp.logical_and(pl.program_id(0) == 0, pl.program_id(1) == 0))
    def _():
        carry[...] = jnp.zeros_like(carry)

    is_ctx = pl.program_id(1) + off < n_ctx_tiles
    tiles = []
    for smp in range(MERGE_SAMPLES):
        row = jnp.where(is_ctx, n_batch, pl.program_id(0) * MERGE_SAMPLES + smp)
        tiles.append(_merge_tile(
            *[r.at[pl.ds(smp, 1)] for r in refs[:n_in]], mod_ref.at[pl.ds(row, 1)], *consts,
            *[r.at[pl.ds(smp, 1)] for r in outs], counts_ref, carry,
            n_src=n_src, off=off, n_ctx_tiles=n_ctx_tiles, alpha=alpha))
    live = list(tiles)
    while live:
        for g in list(live):
            if next(g, live) is live:
                live.remove(g)


def _merge_tile(*refs, n_src, off, n_ctx_tiles, alpha):
    (a0u_ref, up_ref, un_ref, yb_ref, hf_ref, hb_ref, mvo_t_ref, mod_ref,
     convw_ref, normw_ref, wg_ref, wpa_ref, wpb_ref, wpc_ref, wo_ref,
     ln_g_ref, ln_b_ref, wr_ref, br_ref, upper_ref,
     x1_ref, h2_ref, wt_ref, assign_ref, counts_ref, carry) = refs[n_src:]
    i = pl.program_id(1) + off
    nt = pl.num_programs(1) + off
    x_tile = _stream_tile(refs[:n_src], i, n_ctx_tiles)
    tm = x_tile.shape[0]
    d = x_tile.shape[1]

    a0 = a0u_ref[0, :, 0:CONV_DIM].astype(F32)
    u = a0u_ref[0, :, CONV_DIM:].astype(F32)
    prev_ok = jnp.logical_and(i != 0, i != n_ctx_tiles)
    next_ok = jnp.logical_and(i != n_ctx_tiles - 1, i != nt - 1)
    u_prev = jnp.where(prev_ok, up_ref[0, HALO - 1:HALO, CONV_DIM:].astype(F32), 0.0)
    u_next = jnp.where(next_ok, un_ref[0, 0:1, CONV_DIM:].astype(F32), 0.0)
    row = lax.broadcasted_iota(jnp.int32, u.shape, 0)
    u_dn = jnp.where(row == 0, u_prev, pltpu.roll(u, 1, axis=0))
    u_up = jnp.where(row == tm - 1, u_next, pltpu.roll(u, tm - 1, axis=0))
    cw = convw_ref[...]
    ya_all = (a0 * (u_dn * cw[0:1, :] + u * cw[1:2, :] + u_up * cw[2:3, :])).astype(BF16)

    w_hi = wr_ref[...].astype(BF16)
    sub = lax.broadcasted_iota(jnp.int32, (w_hi.shape[0], tm), 0)
    big = jnp.int32(2 * LANES)
    is_g = sub < N_GROUPS
    yield

    x = x_tile
    h = (_ln(x) * (1.0 + mod_ref[0, 1:2, :]) + mod_ref[0, 0:1, :]).astype(BF16)

    hm = hf_ref[0] + hb_ref[0]
    normed = jnp.concatenate(
        [_ln(hm[hd * MLSTM_HEAD_DIM:(hd + 1) * MLSTM_HEAD_DIM, :], axis=0)
         for hd in range(MLSTM_HEADS)], axis=0)
    yc_t = _sigmoid(mvo_t_ref[0].astype(F32)) * (normed * normw_ref[...])
    yc = yc_t.T
    yield

    pa = _dot(ya_all, wpa_ref[...])
    pb = _dot(yb_ref[0], wpb_ref[...])
    pc = _dot(yc.astype(BF16), wpc_ref[...])
    yield
    merged = None
    for n, proj in enumerate((pa, pb, pc)):
        gated = _sigmoid(_dot(h, wg_ref[:, n * d:(n + 1) * d])) * proj
        merged = gated if merged is None else merged + gated
        yield
    yl = _dot(merged.astype(BF16), wo_ref[...])
    yield

    x1 = _ln(alpha * x + mod_ref[0, 2:3, :] * yl) * ln_g_ref[...] + ln_b_ref[...]
    x1_ref[0] = x1
    yield
    h2 = _ln(x1) * (1.0 + mod_ref[0, 4:5, :]) + mod_ref[0, 3:4, :]
    for s in range(d // LANES):
        h2_ref[0, pl.ds(s, tm, stride=ROW_TILE), :] = h2[:, s * LANES:(s + 1) * LANES]
    yield

    lg = _dot_nt(w_hi, h2.astype(BF16)) + br_ref[...]
    gl = jnp.where(is_g, lg, -jnp.inf)
    g_max = jnp.max(gl, axis=0, keepdims=True)
    g_sel = jnp.min(jnp.where(gl == g_max, sub, big), axis=0, keepdims=True)
    g_p = 1.0 / jnp.sum(jnp.where(is_g, jnp.exp(gl - g_max), 0.0), axis=0, keepdims=True)
    lo = N_GROUPS + EXPERTS_PER_GROUP * g_sel
    el = jnp.where(jnp.logical_and(sub >= lo, sub < lo + EXPERTS_PER_GROUP), lg, -jnp.inf)
    e1 = jnp.max(el, axis=0, keepdims=True)
    i1 = jnp.min(jnp.where(el == e1, sub, big), axis=0, keepdims=True)
    el2 = jnp.where(sub == i1, -jnp.inf, el)
    e2 = jnp.max(el2, axis=0, keepdims=True)
    i2 = jnp.min(jnp.where(el2 == e2, sub, big), axis=0, keepdims=True)
    t = jnp.exp(e2 - e1)
    w1 = g_p / (1.0 + t)
    w2 = w1 * t
    eid1 = (i1 - N_GROUPS).astype(F32)
    eid2 = (i2 - N_GROUPS).astype(F32)
    rows = jnp.concatenate([eid1, eid2, w1, w2, jnp.zeros((4, tm), F32)], axis=0)
    wt_ref[0] = rows.T
    yield
    eid = jnp.concatenate([eid1, eid2], axis=1)
    sub_e = lax.broadcasted_iota(jnp.int32, (N_EXPERTS, eid.shape[1]), 0).astype(F32)
    onehot = jnp.where(sub_e == eid, 1.0, 0.0)
    earlier = _dot(onehot.astype(BF16), upper_ref[...])
    seen = carry[...]
    rank = jnp.sum(onehot * (earlier + seen), axis=0, keepdims=True)
    assign_ref[0, 0] = jnp.concatenate([eid, rank], axis=0).astype(jnp.int32)
    seen = seen + jnp.sum(onehot, axis=1, keepdims=True)
    carry[...] = seen
    counts_ref[...] = seen


def _merge(streams, mod, a0u, yb, hf_t, hb_t, mvo_t, conv_w, norm_w, w_gate, w_pa, w_pb, w_pc, w_o,
           ln_g, ln_b, w_route, b_route, n_ctx, off, alpha):
    B, _, D = streams[0].shape
    T = a0u.shape[1]
    tm = TOK_TILE
    nct = n_ctx // tm
    nt = T // tm - off
    tn = nt * tm
    hpt = tm // HALO
    nhalo = T // HALO

    ns = MERGE_SAMPLES
    assert B % ns == 0

    def tok(width):
        return pl.BlockSpec((ns, tm, width), lambda b, i: (b, i + off, 0))

    def chan(rows):
        return pl.BlockSpec((ns, rows, tm), lambda b, i: (b, 0, i + off))

    def full(a):
        return pl.BlockSpec(a.shape, lambda b, i: (0,) * a.ndim, pipeline_mode=pl.Buffered(1))

    def otok(width):
        return pl.BlockSpec((ns, tm, width), lambda b, i: (b, i, 0))

    a_i = lax.broadcasted_iota(jnp.int32, (TOP_K * tm, TOP_K * tm), 0)
    b_i = lax.broadcasted_iota(jnp.int32, (TOP_K * tm, TOP_K * tm), 1)
    upper = (a_i < b_i).astype(BF16)
    consts = [conv_w, norm_w, w_gate, w_pa, w_pb, w_pc, w_o, ln_g, ln_b, w_route, b_route, upper]
    return pl.pallas_call(
        functools.partial(_merge_kernel, n_src=len(streams), off=off, n_ctx_tiles=nct, alpha=alpha,
                          n_batch=B),
        grid=(B // ns, nt),
        in_specs=_stream_specs(streams, tm, nct, off, ns) + [
            tok(2 * CONV_DIM),
            pl.BlockSpec((ns, HALO, 2 * CONV_DIM),
                         lambda b, i: (b, jnp.maximum((i + off) * hpt - 1, 0), 0)),
            pl.BlockSpec((ns, HALO, 2 * CONV_DIM),
                         lambda b, i: (b, jnp.minimum((i + off + 1) * hpt, nhalo - 1), 0)),
            tok(ATT_Q_DIM), chan(MLSTM_DIM), chan(MLSTM_DIM),
            pl.BlockSpec((ns, MLSTM_DIM, tm), lambda b, i: (b, 1, i + off)),
            full(mod),
        ] + [full(a) for a in consts],
        out_specs=[otok(D),
                   pl.BlockSpec((ns, tm * ROW_TILE, LANES), lambda b, i: (b, i, 0)),
                   otok(ROW_TILE),
                   pl.BlockSpec((ns, 1, 2, TOP_K * tm), lambda b, i: (b, i, 0, 0)),
                   pl.BlockSpec((N_EXPERTS, 1), lambda b, i: (0, 0))],
        out_shape=[jax.ShapeDtypeStruct((B, tn, D), F32),
                   jax.ShapeDtypeStruct((B, tn * ROW_TILE, LANES), F32),
                   jax.ShapeDtypeStruct((B, tn, ROW_TILE), F32),
                   jax.ShapeDtypeStruct((B, nt, 2, TOP_K * tm), jnp.int32),
                   jax.ShapeDtypeStruct((N_EXPERTS, 1), F32)],
        scratch_shapes=[pltpu.VMEM((N_EXPERTS, 1), F32)],
        compiler_params=_cparams(("arbitrary", "arbitrary")),
        name="merge",
    )(*streams, a0u, a0u, a0u, yb, hf_t, hb_t, mvo_t, mod, *consts)


def _dispatch_kernel(pos_ref, pend_ref, nblk_ref, h_hbm, xs_hbm, zbuf, sem, zsem, *, tm):
    s = pl.program_id(0)
    zrows = zbuf.shape[0]
    last_blk = xs_hbm.shape[0] // zrows - 1

    @pl.when(s == 0)
    def _():
        zbuf[...] = jnp.zeros_like(zbuf)

        def zero_copy(start):
            return pltpu.make_async_copy(
                zbuf, xs_hbm.at[pl.ds(pl.multiple_of(start, ROW_TILE), zrows), :], zsem)

        jobs = []
        for e in range(N_EXPERTS):
            before = pend_ref[e - 1] if e > 0 else 0
            jobs.append((pend_ref[e] * ROW_TILE - zrows, pend_ref[e] > before))
        for e in range(N_EXPERTS):
            jobs.append(((nblk_ref[0] + e) * zrows, nblk_ref[0] + e <= last_blk))
        for st, needed in jobs:
            @pl.when(needed)
            def _(st=st):
                zero_copy(st).start()
        for st, needed in jobs:
            @pl.when(needed)
            def _(st=st):
                zero_copy(st).wait()

    def row_copy(r, k):
        src = pl.multiple_of((s * tm + r) * ROW_TILE, ROW_TILE)
        dst = pl.multiple_of(pos_ref[(s * TOP_K + k) * tm + r] * ROW_TILE, ROW_TILE)
        return pltpu.make_async_copy(h_hbm.at[pl.ds(src, ROW_TILE), :],
                                     xs_hbm.at[pl.ds(dst, ROW_TILE), :], sem)

    def body(r, carry):
        for k in range(TOP_K):
            row_copy(r, k).start()
        return carry
    lax.fori_loop(0, tm, body, 0, unroll=GATHER_UNROLL)

    def drain_one_step():
        for k in range(TOP_K):
            pltpu.make_async_copy(h_hbm.at[pl.ds(0, tm * ROW_TILE), :],
                                  xs_hbm.at[pl.ds(0, tm * ROW_TILE), :], sem).wait()

    @pl.when(s > 0)
    def _():
        drain_one_step()

    @pl.when(s == pl.num_programs(0) - 1)
    def _():
        drain_one_step()


def _dispatch(h2v, pos, pend, nblk, p_rows, tm):
    rows = h2v.shape[0]
    nsteps = rows // (tm * ROW_TILE)
    grid_spec = pltpu.PrefetchScalarGridSpec(
        num_scalar_prefetch=3,
        grid=(nsteps,),
        in_specs=[pl.BlockSpec(memory_space=pl.ANY)],
        out_specs=pl.BlockSpec(memory_space=pl.ANY),
        scratch_shapes=[
            pltpu.VMEM((EXP_BLK * ROW_TILE, LANES), F32),
            pltpu.SemaphoreType.DMA,
            pltpu.SemaphoreType.DMA,
        ],
    )
    return pl.pallas_call(
        functools.partial(_dispatch_kernel, tm=tm),
        grid_spec=grid_spec,
        out_shape=jax.ShapeDtypeStruct((p_rows * ROW_TILE, LANES), F32),
        compiler_params=_cparams(("arbitrary",)),
        name="dispatch",
    )(pos, pend, nblk, h2v)


def _expert_kernel(blke_ref, nblk_ref, xs_ref, wi_ref, wo_ref, y_ref, wi_bf, wo_bf):
    i = pl.program_id(0)
    nb = nblk_ref[0]
    blk = xs_ref.shape[0] // ROW_TILE
    nsl = wi_ref.shape[2] // LANES

    @pl.when(i < nb)
    def _():
        e_now = blke_ref[i]
        e_before = blke_ref[jnp.maximum(i - 1, 0)]

        @pl.when(jnp.logical_or(i == 0, e_now != e_before))
        def _():
            wi_bf[...] = wi_ref[0, 0].astype(BF16)
            wo_bf[...] = wo_ref[0, 0].astype(BF16)

        def rows_part(part):
            r0 = part * EXP_ROWS * ROW_TILE
            xin = jnp.concatenate(
                [xs_ref[pl.ds(r0 + s, EXP_ROWS, stride=ROW_TILE), :] for s in range(nsl)],
                axis=1).astype(BF16)
            yield
            mid = _dot(xin, wi_bf[...])
            yield
            gt = mid[:, 0:D_EXPERT]
            up = mid[:, D_EXPERT:]
            act = ((gt * _sigmoid(gt)) * up).astype(BF16)
            yield
            y = _dot(act, wo_bf[...])
            yield
            for s in range(nsl):
                y_ref[pl.ds(r0 + s, EXP_ROWS, stride=ROW_TILE), :] = y[:, s * LANES:(s + 1) * LANES]

        parts = [rows_part(p) for p in range(blk // EXP_ROWS)]
        for t in range(len(parts) + 4):
            for p, gen in enumerate(parts):
                if 0 <= t - p <= 4:
                    next(gen, None)

    @pl.when(i >= nb)
    def _():
        y_ref[...] = jnp.zeros_like(y_ref)


def _experts(xs, blk_e, nblk, w_ei, w_eo, layer):
    d = w_ei.shape[2]
    blk = EXP_BLK
    nb = xs.shape[0] // (blk * ROW_TILE)
    grid_spec = pltpu.PrefetchScalarGridSpec(
        num_scalar_prefetch=2,
        grid=(nb,),
        in_specs=[
            pl.BlockSpec((blk * ROW_TILE, LANES),
                         lambda i, be, nbk: (jnp.minimum(i, jnp.maximum(nbk[0] - 1, 0)), 0)),
            pl.BlockSpec((1, 1, d, 2 * D_EXPERT), lambda i, be, nbk: (layer, be[i], 0, 0)),
            pl.BlockSpec((1, 1, D_EXPERT, d), lambda i, be, nbk: (layer, be[i], 0, 0)),
        ],
        out_specs=pl.BlockSpec((blk * ROW_TILE, LANES), lambda i, be, nbk: (i, 0)),
        scratch_shapes=[
            pltpu.VMEM((d, 2 * D_EXPERT), BF16),
            pltpu.VMEM((D_EXPERT, d), BF16),
        ],
    )
    return pl.pallas_call(
        _expert_kernel,
        grid_spec=grid_spec,
        out_shape=jax.ShapeDtypeStruct(xs.shape, F32),
        compiler_params=_cparams(("arbitrary",)),
        name="experts",
    )(blk_e, nblk, xs, w_ei, w_eo)


def _combine_kernel(pos_ref, x_ref, mod_ref, wt_ref, ln_g_ref, ln_b_ref, y_hbm, o_ref,
                    ybuf, sem, *, alpha):
    b = pl.program_id(0)
    i = pl.program_id(1)
    nt = pl.num_programs(1)
    tm = x_ref.shape[1]
    step = b * nt + i
    nsteps = pl.num_programs(0) * nt

    def start_gather(s, slot):
        def body(r, carry):
            dst = pl.multiple_of(r * ROW_TILE, ROW_TILE)
            for k in range(TOP_K):
                src = pl.multiple_of(pos_ref[(s * TOP_K + k) * tm + r] * ROW_TILE, ROW_TILE)
                pltpu.make_async_copy(y_hbm.at[pl.ds(src, ROW_TILE), :],
                                      ybuf.at[slot, k, pl.ds(dst, ROW_TILE), :], sem.at[slot]).start()
            return carry
        lax.fori_loop(0, tm, body, 0, unroll=GATHER_UNROLL)

    @pl.when(step == 0)
    def _():
        start_gather(0, 0)

    @pl.when(step + 1 < nsteps)
    def _():
        start_gather(step + 1, (step + 1) % 2)

    slot = step % 2
    for k in range(TOP_K):
        pltpu.make_async_copy(y_hbm.at[pl.ds(0, tm * ROW_TILE), :], ybuf.at[slot, k],
                              sem.at[slot]).wait()
    wt = wt_ref[0]
    w0 = wt[:, 2:3]
    w1 = wt[:, 3:4]
    f = jnp.concatenate(
        [w0 * ybuf[slot, 0, pl.ds(j, tm, stride=ROW_TILE), :]
         + w1 * ybuf[slot, 1, pl.ds(j, tm, stride=ROW_TILE), :]
         for j in range(x_ref.shape[2] // LANES)], axis=1)
    x = x_ref[0]
    o_ref[0] = _ln(alpha * x + mod_ref[0, 5:6, :] * f) * ln_g_ref[...] + ln_b_ref[...]


def _combine(x1, mod, wts, pos, y, ln_g, ln_b, n_ctx_tiles, alpha):
    B, tn, D = x1.shape
    tm = TOK_TILE
    nt = tn // tm
    grid_spec = pltpu.PrefetchScalarGridSpec(
        num_scalar_prefetch=1,
        grid=(B, nt),
        in_specs=[
            pl.BlockSpec((1, tm, D), lambda b, i, pos: (b, i, 0)),
            pl.BlockSpec((1, 6, D), lambda b, i, pos: (jnp.where(i < n_ctx_tiles, B, b), 0, 0)),
            pl.BlockSpec((1, tm, ROW_TILE), lambda b, i, pos: (b, i, 0)),
            pl.BlockSpec((1, D), lambda b, i, pos: (0, 0)),
            pl.BlockSpec((1, D), lambda b, i, pos: (0, 0)),
            pl.BlockSpec(memory_space=pl.ANY),
        ],
        out_specs=pl.BlockSpec((1, tm, D), lambda b, i, pos: (b, i, 0)),
        scratch_shapes=[
            pltpu.VMEM((2, TOP_K, tm * ROW_TILE, LANES), F32),
            pltpu.SemaphoreType.DMA((2,)),
        ],
    )
    return pl.pallas_call(
        functools.partial(_combine_kernel, alpha=alpha),
        grid_spec=grid_spec,
        out_shape=jax.ShapeDtypeStruct((B, tn, D), F32),
        compiler_params=_cparams(("arbitrary", "arbitrary")),
        name="combine",
    )(pos, x1, mod, wts, ln_g, ln_b, y)


def _segments(counts, assign):
    counts = counts.reshape(N_EXPERTS).astype(jnp.int32)
    n_assign = assign.shape[0] * assign.shape[2]
    padded = (counts + EXP_BLK - 1) // EXP_BLK * EXP_BLK
    pad_end = jnp.cumsum(padded)
    pad_start = pad_end - padded
    p_rows = n_assign + N_EXPERTS * EXP_BLK
    nb = p_rows // EXP_BLK
    blk_first = jnp.arange(nb, dtype=jnp.int32) * EXP_BLK
    blk_e = jnp.minimum(jnp.sum((pad_end[None, :] <= blk_first[:, None]).astype(jnp.int32), axis=1),
                        N_EXPERTS - 1).astype(jnp.int32)
    nblk = (pad_end[-1] // EXP_BLK).astype(jnp.int32).reshape(1)
    eid, rank = assign[:, 0, :], assign[:, 1, :]
    onehot = eid[:, :, None] == jnp.arange(N_EXPERTS, dtype=jnp.int32)
    pos = rank + jnp.sum(jnp.where(onehot, pad_start.astype(jnp.int32), 0), axis=-1)
    return pos.reshape(-1).astype(jnp.int32), pad_end.astype(jnp.int32), blk_e, nblk, p_rows


def _rope_tables(n_ctx, n_lat):
    nf = ATT_HEAD_DIM // 4
    inv = ROPE_BASE ** (-jnp.arange(nf, dtype=F32) / nf)
    rows = n_lat // GRID_W
    pos_r = jnp.repeat(jnp.arange(rows, dtype=F32), GRID_W)
    pos_c = jnp.tile(jnp.arange(GRID_W, dtype=F32), rows)
    ang_r = pos_r[:, None] * inv
    ang_c = pos_c[:, None] * inv
    cos_h = jnp.concatenate([jnp.cos(ang_r)] * 2 + [jnp.cos(ang_c)] * 2, axis=-1)
    sin_h = jnp.concatenate([-jnp.sin(ang_r), jnp.sin(ang_r),
                             -jnp.sin(ang_c), jnp.sin(ang_c)], axis=-1)
    reps = LANES // ATT_HEAD_DIM
    cos_l = jnp.tile(cos_h, (1, reps))
    sin_l = jnp.tile(sin_h, (1, reps))
    cos_t = jnp.concatenate([jnp.ones((n_ctx, LANES), F32), cos_l], axis=0)
    sin_t = jnp.concatenate([jnp.zeros((n_ctx, LANES), F32), sin_l], axis=0)
    return cos_t, sin_t


def _projection_weights(w_in_l):
    offs = np.cumsum((0, 3 * CONV_DIM, ATT_Q_DIM, ATT_KV_DIM, ATT_KV_DIM,
                      MLSTM_DIM, MLSTM_DIM, MLSTM_DIM, MLSTM_DIM, N_GATE)).tolist()
    a, q, k, v, mq, mk, mv, mo, g = [w_in_l[:, offs[n]:offs[n + 1]] for n in range(9)]
    hd = ATT_HEAD_DIM

    def swap(w):
        return jnp.concatenate([w[:, hd:], w[:, :hd]], axis=1)

    w_tok = jnp.concatenate([a, q, k, swap(k), v, swap(v), mq, mk], axis=1).astype(BF16)
    w_chan = jnp.concatenate([mv, mo, g], axis=1).T.astype(BF16)
    w_gate = w_in_l[:, offs[9]:].astype(BF16)
    return w_tok, w_chan, w_gate


def kernel(x, c, ctx, c_ctx, w_ada, b_ada, w_in, conv_w, attn_sink, mlstm_gate_b, mlstm_norm_w,
           w_proj_a, w_proj_b, w_proj_c, w_out, ln1_g, ln1_b, w_route_group, b_route_group,
           w_route_expert, b_route_expert, w_expert_in, w_expert_out, ln2_g, ln2_b):
    B, L, D = x.shape
    n_ctx = ctx.shape[1]
    depth = w_ada.shape[0]
    T = n_ctx + L
    alpha = (2 * depth) ** 0.25
    assert D == D_MODEL and n_ctx % TOK_TILE == 0 and L % TOK_TILE == 0 and L % GRID_W == 0
    assert MLSTM_CH == LANES and ATT_QB == LANES
    nct = n_ctx // TOK_TILE

    nrows = -(-(B + 1) // 8) * 8
    cond = jnp.concatenate([c, c_ctx[None, :], jnp.zeros((nrows - B - 1, D), F32)], axis=0)
    mod_all = _ada(cond, w_ada, b_ada).reshape(depth, nrows, 6, D)

    cos_t, sin_t = _rope_tables(n_ctx, L)
    streams = (ctx, x)

    for i in range(depth):
        need_ctx = i < depth - 1
        mod = mod_all[i]
        w_tok, w_chan, w_gate = _projection_weights(w_in[i])

        a0u, q, kv, mqk, mvo_t, gcol, grow = _inproj(streams, mod, w_tok, w_chan, cos_t, sin_t, n_ctx)
        yb = _attention(q, kv, attn_sink[i], n_ctx)
        hf_t, hb_t = _mlstm(mqk, mvo_t, gcol, grow, mlstm_gate_b[i], n_ctx)

        off = 0 if need_ctx else nct
        w_route = jnp.pad(jnp.concatenate([w_route_group[i], w_route_expert[i]], axis=1).T,
                          ((0, LANES - N_ROUTE), (0, 0)))
        b_route = jnp.pad(jnp.concatenate([b_route_group[i], b_route_expert[i]]),
                          (0, LANES - N_ROUTE)).reshape(LANES, 1)
        x1, h2v, wts, assign, counts = _merge(
            streams, mod, a0u, yb, hf_t, hb_t, mvo_t, conv_w[i], mlstm_norm_w[i].reshape(MLSTM_DIM, 1),
            w_gate, w_proj_a[i].astype(BF16), w_proj_b[i].astype(BF16), w_proj_c[i].astype(BF16),
            w_out[i].astype(BF16), ln1_g[i].reshape(1, D), ln1_b[i].reshape(1, D),
            w_route, b_route, n_ctx, off, alpha)

        tn = x1.shape[1]
        pos, pend, blk_e, nblk, p_rows = _segments(counts, assign.reshape(-1, 2, TOP_K * TOK_TILE))
        xs = _dispatch(h2v.reshape(-1, LANES), pos, pend, nblk, p_rows, TOK_TILE)
        y = _experts(xs, blk_e, nblk, w_expert_in, w_expert_out, i)
        streams = (_combine(x1, mod, wts, pos, y,
                            ln2_g[i].reshape(1, D), ln2_b[i].reshape(1, D),
                            nct if need_ctx else 0, alpha),)
    return streams[0]
```

```python
import functools

import jax
import jax.numpy as jnp
import numpy as np
from jax import lax
from jax.experimental import pallas as pl
from jax.experimental.pallas import tpu as pltpu

D_MODEL = 1024
GRID_W = 64
CONV_DIM = 256
ATT_HEADS = 8
ATT_KV_HEADS = 2
ATT_HEAD_DIM = 64
ATT_WINDOW = 128
ROPE_BASE = 10000.0
MLSTM_HEADS = 4
MLSTM_HEAD_DIM = 64
MLSTM_DIM = MLSTM_HEADS * MLSTM_HEAD_DIM
N_GROUPS = 4
EXPERTS_PER_GROUP = 8
N_EXPERTS = N_GROUPS * EXPERTS_PER_GROUP
TOP_K = 2
D_EXPERT = D_MODEL // 2
LN_EPS = 1e-6
NEG_INF = -1e30

ATT_Q_DIM = ATT_HEADS * ATT_HEAD_DIM
ATT_KV_DIM = ATT_KV_HEADS * ATT_HEAD_DIM
N_GATE = 4 * MLSTM_HEADS
MIX_COLS = 3 * CONV_DIM + ATT_Q_DIM + 2 * ATT_KV_DIM + 4 * MLSTM_DIM
N_ROUTE = N_GROUPS + N_EXPERTS

LANES = 128
VMEM_LIMIT = 56 * 1024 * 1024
TOK_TILE = 256
PAIR_SAMPLES = 4
MERGE_SAMPLES = 2
ATT_QB = 128
ATT_STEP = 256
ATT_LAG_SOFTMAX = 1
ATT_LAG_VALUES = 2
MLSTM_CH = 128
MLSTM_STEP = 256
EXP_BLK = 512
EXP_ROWS = 256
HALO = 16
ROW_TILE = 8
GATHER_UNROLL = 8
DISPATCH_TILES = 2

F32 = jnp.float32
BF16 = jnp.bfloat16


def _cparams(sem):
    return pltpu.CompilerParams(dimension_semantics=sem, vmem_limit_bytes=VMEM_LIMIT)


def _ln(x, axis=-1):
    mu = jnp.mean(x, axis=axis, keepdims=True)
    xc = x - mu
    var = jnp.mean(xc * xc, axis=axis, keepdims=True)
    return xc * lax.rsqrt(var + LN_EPS)


def _sigmoid(x):
    return 0.5 * jnp.tanh(0.5 * x) + 0.5


def _split3(x):
    hi = x.astype(BF16)
    r1 = x - hi.astype(F32)
    mid = r1.astype(BF16)
    lo = (r1 - mid.astype(F32)).astype(BF16)
    return hi, mid, lo


def _dot(a, b):
    return jnp.dot(a, b, preferred_element_type=F32)


def _dot_nt(a, b):
    return lax.dot_general(a, b, (((1,), (1,)), ((), ())), preferred_element_type=F32)


def _dot3(x, rhs_b):
    return sum(_dot(part, rhs_b) for part in _split3(x))


def _dot3_l(lhs_b, x):
    return sum(_dot(lhs_b, part) for part in _split3(x))


def _ada_kernel(c_ref, w_ref, b_ref, o_ref):
    cv = c_ref[...]
    s = cv * _sigmoid(cv)
    o_ref[0] = _dot(s.astype(BF16), w_ref[0].astype(BF16)) + b_ref[0]


def _ada(cond, w_ada, b_ada):
    depth, d, n6 = w_ada.shape
    rows = cond.shape[0]
    nt = n6 // d
    return pl.pallas_call(
        _ada_kernel,
        grid=(depth, nt),
        in_specs=[
            pl.BlockSpec((rows, d), lambda l, j: (0, 0)),
            pl.BlockSpec((1, d, d), lambda l, j: (l, 0, j)),
            pl.BlockSpec((1, 1, d), lambda l, j: (l, 0, j)),
        ],
        out_specs=pl.BlockSpec((1, rows, d), lambda l, j: (l, 0, j)),
        out_shape=jax.ShapeDtypeStruct((depth, rows, n6), F32),
        compiler_params=_cparams(("arbitrary", "arbitrary")),
        name="ada_mod",
    )(cond, w_ada, b_ada.reshape(depth, 1, n6))


def _rope(x, cos, sin_signed, lane):
    swapped = jnp.where(lane % 32 < 16,
                        pltpu.roll(x, LANES - 16, axis=1),
                        pltpu.roll(x, 16, axis=1))
    return x * cos + swapped * sin_signed


_OFF_A = 0
_OFF_Q = _OFF_A + 3 * CONV_DIM
_OFF_K = _OFF_Q + ATT_Q_DIM
_OFF_V = _OFF_K + 2 * ATT_KV_DIM
_OFF_MQK = _OFF_V + 2 * ATT_KV_DIM
_W_TOK_COLS = _OFF_MQK + 2 * MLSTM_DIM


def _stream_specs(streams, tm, nct, off=0, nb=1):
    d = streams[0].shape[2]
    if len(streams) == 1:
        return [pl.BlockSpec((nb, tm, d), lambda b, i, *_: (b, i + off, 0))]
    return [pl.BlockSpec((nb, tm, d), lambda b, i, *_: (b, jnp.minimum(i + off, nct - 1), 0)),
            pl.BlockSpec((nb, tm, d), lambda b, i, *_: (b, jnp.maximum(i + off - nct, 0), 0))]


def _stream_tile(refs, i, nct):
    if len(refs) == 1:
        return refs[0][0]
    return jnp.where(i < nct, refs[0][0], refs[1][0])


def _inproj_kernel(*refs, n_src, nct, n_batch):
    mod_ref = refs[n_src]
    shared = refs[n_src + 1:n_src + 5]
    outs = refs[n_src + 5:]
    is_ctx = pl.program_id(1) < nct
    tiles = []
    for smp in range(PAIR_SAMPLES):
        row = jnp.where(is_ctx, n_batch, pl.program_id(0) * PAIR_SAMPLES + smp)
        tiles.append(_inproj_tile(
            *[r.at[pl.ds(smp, 1)] for r in refs[:n_src]], mod_ref.at[pl.ds(row, 1)], *shared,
            *[r.at[pl.ds(smp, 1)] for r in outs], n_src=n_src, nct=nct))
    live = list(tiles)
    while live:
        for g in list(live):
            if next(g, live) is live:
                live.remove(g)


def _inproj_tile(*refs, n_src, nct):
    (mod_ref, w_ref, wt_ref, cos_ref, sin_ref,
     a0u_ref, q_ref, kv_ref, mqk_ref, mvo_t_ref, gcol_ref, grow_ref) = refs[n_src:]
    x = _stream_tile(refs[:n_src], pl.program_id(1), nct)
    shift = mod_ref[0, 0:1, :]
    scale = mod_ref[0, 1:2, :]
    h = (_ln(x) * (1.0 + scale) + shift).astype(BF16)
    cos = cos_ref[...]
    sin = sin_ref[...]
    lane = lax.broadcasted_iota(jnp.int32, cos.shape, 1)
    yield

    def cols(lo, n):
        return _dot(h, w_ref[:, lo:lo + n])

    za = cols(_OFF_A, 3 * CONV_DIM)
    yield
    a0u_ref[0, :, 0:CONV_DIM] = za[:, 0:CONV_DIM].astype(BF16)
    a0u_ref[0, :, CONV_DIM:2 * CONV_DIM] = (
        za[:, CONV_DIM:2 * CONV_DIM] * za[:, 2 * CONV_DIM:3 * CONV_DIM]).astype(BF16)
    zq = cols(_OFF_Q, ATT_Q_DIM)
    yield
    qscale = ATT_HEAD_DIM ** -0.5
    for j in range(ATT_Q_DIM // LANES):
        piece = _rope(zq[:, j * LANES:(j + 1) * LANES], cos, sin, lane)
        q_ref[0, :, j * LANES:(j + 1) * LANES] = (piece * qscale).astype(BF16)
    zk = cols(_OFF_K, 2 * ATT_KV_DIM)
    yield
    for j in range(2):
        kv_ref[0, :, j * LANES:(j + 1) * LANES] = _rope(
            zk[:, j * LANES:(j + 1) * LANES], cos, sin, lane).astype(BF16)
    kv_ref[0, :, 2 * LANES:] = cols(_OFF_V, 2 * ATT_KV_DIM).astype(BF16)
    yield
    zm = cols(_OFF_MQK, 2 * MLSTM_DIM)
    yield
    mqk_ref[0, :, 0:MLSTM_DIM] = zm[:, 0:MLSTM_DIM].astype(BF16)
    mqk_ref[0, :, MLSTM_DIM:] = (zm[:, MLSTM_DIM:] * (MLSTM_HEAD_DIM ** -0.5)).astype(BF16)
    nchan = mvo_t_ref.shape[1]
    zt = _dot_nt(wt_ref[...], h)
    yield
    mvo_t_ref[0] = zt[0:nchan, :].astype(BF16)
    grow = zt[nchan:, :]
    grow_ref[0] = grow
    gcol_ref[0] = grow.T


def _inproj(streams, mod, w_tok, w_chan, cos_t, sin_t, n_ctx):
    B, _, D = streams[0].shape
    T = sum(s.shape[1] for s in streams) if len(streams) > 1 else streams[0].shape[1]
    tm = TOK_TILE
    nct = n_ctx // tm
    nchan = w_chan.shape[0] - N_GATE
    ns = PAIR_SAMPLES
    assert B % ns == 0

    def tok(width):
        return pl.BlockSpec((ns, tm, width), lambda b, i: (b, i, 0))

    def chan(rows):
        return pl.BlockSpec((ns, rows, tm), lambda b, i: (b, 0, i))

    outs = [(2 * CONV_DIM, BF16), (ATT_Q_DIM, BF16), (4 * ATT_KV_DIM, BF16), (2 * MLSTM_DIM, BF16)]
    return pl.pallas_call(
        functools.partial(_inproj_kernel, n_src=len(streams), nct=nct, n_batch=B),
        grid=(B // ns, T // tm),
        in_specs=_stream_specs(streams, tm, nct, 0, ns) + [
            pl.BlockSpec(mod.shape, lambda b, i: (0, 0, 0), pipeline_mode=pl.Buffered(1)),
            pl.BlockSpec(w_tok.shape, lambda b, i: (0, 0), pipeline_mode=pl.Buffered(1)),
            pl.BlockSpec(w_chan.shape, lambda b, i: (0, 0), pipeline_mode=pl.Buffered(1)),
            pl.BlockSpec((tm, LANES), lambda b, i: (i, 0)),
            pl.BlockSpec((tm, LANES), lambda b, i: (i, 0)),
        ],
        out_specs=[tok(w) for w, _ in outs] + [
            chan(nchan), tok(N_GATE), chan(N_GATE)],
        out_shape=[jax.ShapeDtypeStruct((B, T, w), dt) for w, dt in outs] + [
            jax.ShapeDtypeStruct((B, nchan, T), BF16),
            jax.ShapeDtypeStruct((B, T, N_GATE), F32), jax.ShapeDtypeStruct((B, N_GATE, T), F32)],
        compiler_params=_cparams(("parallel", "arbitrary")),
        name="in_proj",
    )(*streams, mod, w_tok, w_chan, cos_t, sin_t)


def _attn_kernel(sink_ref, q_ref, kvc_ref, kvp_ref, kvm_ref, kvn_ref, o_ref, *, n_ctx_blk, n_blk):
    n_ctx = kvc_ref.shape[1]
    qb = kvp_ref.shape[1]
    nsub = q_ref.shape[1] // qb
    nk = n_ctx + 3 * qb
    half = LANES // 2

    lane_row = lax.broadcasted_iota(jnp.int32, (1, LANES), 1)
    keep = [(lane_row < half).astype(F32).astype(BF16), (lane_row >= half).astype(F32).astype(BF16)]
    pad_v = jnp.zeros((qb, LANES), BF16)
    qi = lax.broadcasted_iota(jnp.int32, (qb, qb), 0)
    ki = lax.broadcasted_iota(jnp.int32, (qb, qb), 1)
    lane_q = lax.broadcasted_iota(jnp.int32, (qb, LANES), 1)
    neg = jnp.full((qb, LANES), NEG_INF, F32)
    group = ATT_HEADS // ATT_KV_HEADS
    blocks = []
    for blk_i in range(q_ref.shape[0] * nsub):
        smp, sb = divmod(blk_i, nsub)
        around = ([kvp_ref[smp]] + [kvm_ref[smp, n * qb:(n + 1) * qb, :] for n in range(nsub)]
                  + [kvn_ref[smp]])
        g = pl.program_id(1) * nsub + sb
        is_lat = g >= n_ctx_blk
        has_prev = g >= n_ctx_blk + 1
        has_next = g < n_blk - 1
        kv_all = jnp.concatenate([kvc_ref[smp]] + around[sb:sb + 3], axis=0)
        k_ext, v_ext = {}, {}
        for tile in range(2):
            for par in range(2):
                k_ext[tile, par] = kv_all[:, tile * LANES:(tile + 1) * LANES] * keep[par]
                v = jnp.concatenate([kv_all[:, (2 + tile) * LANES:(3 + tile) * LANES], pad_v], axis=0)
                v_ext[tile, par] = jnp.concatenate(
                    [v * keep[par], jnp.broadcast_to(keep[par], v.shape)], axis=1)
        blocks.append(dict(
            is_lat=is_lat, ok_prev=jnp.logical_and(ki >= qi, has_prev),
            ok_next=jnp.logical_and(ki <= qi, jnp.logical_and(has_next, is_lat)),
            k_ext=k_ext, v_ext=v_ext, q=q_ref[smp, sb * qb:(sb + 1) * qb, :]))

    jobs = [(sb, pair, par) for sb in range(len(blocks)) for pair in range(ATT_HEADS // 2)
            for par in range(2)]
    scores, probs, accs, outs = {}, {}, {}, {}

    def operands(job):
        sb, pair, par = job
        kvh = (2 * pair) // group
        tile = 0 if kvh == par else 1
        return blocks[sb], tile

    def stage_scores(job):
        sb, pair, par = job
        blk, tile = operands(job)
        scores[job] = _dot_nt(blk["q"][:, pair * LANES:(pair + 1) * LANES], blk["k_ext"][tile, par])

    def stage_softmax(job):
        sb, pair, par = job
        blk, _ = operands(job)
        s = scores.pop(job)
        s_ext = jnp.concatenate([
            s[:, 0:n_ctx],
            jnp.where(blk["ok_prev"], s[:, n_ctx:n_ctx + qb], neg),
            jnp.where(blk["is_lat"], s[:, n_ctx + qb:n_ctx + 2 * qb], neg),
            jnp.where(blk["ok_next"], s[:, n_ctx + 2 * qb:nk], neg),
            jnp.where(lane_q == 0, sink_ref[2 * pair + par], neg)], axis=1)
        m = jnp.max(s_ext, axis=-1, keepdims=True)
        probs[job] = jnp.exp(s_ext - m).astype(BF16)

    def stage_values(job):
        sb, pair, par = job
        blk, tile = operands(job)
        part = _dot(probs.pop(job), blk["v_ext"][tile, par])
        if par == 0:
            accs[sb, pair] = part
        else:
            acc = accs.pop((sb, pair)) + part
            outs[sb, pair] = acc[:, 0:LANES] / acc[:, LANES:]

    for t in range(len(jobs) + ATT_LAG_VALUES):
        if t < len(jobs):
            stage_scores(jobs[t])
        if 0 <= t - ATT_LAG_SOFTMAX < len(jobs):
            stage_softmax(jobs[t - ATT_LAG_SOFTMAX])
        if 0 <= t - ATT_LAG_VALUES < len(jobs):
            stage_values(jobs[t - ATT_LAG_VALUES])
    for blk_i in range(len(blocks)):
        smp, sb = divmod(blk_i, nsub)
        o_ref[smp, sb * qb:(sb + 1) * qb, :] = jnp.concatenate(
            [outs[blk_i, pair] for pair in range(ATT_HEADS // 2)], axis=-1).astype(o_ref.dtype)


def _attention(q, kv, sink, n_ctx):
    B, T, _ = q.shape
    qb = ATT_QB
    nsub = ATT_STEP // qb
    n_blk = T // qb
    nq = T // ATT_STEP
    n_ctx_blk = n_ctx // qb
    kvw = kv.shape[2]
    ns = PAIR_SAMPLES
    assert B % ns == 0
    grid_spec = pltpu.PrefetchScalarGridSpec(
        num_scalar_prefetch=1,
        grid=(B // ns, nq),
        in_specs=[
            pl.BlockSpec((ns, ATT_STEP, ATT_Q_DIM), lambda b, j, sk: (b, j, 0)),
            pl.BlockSpec((ns, n_ctx, kvw), lambda b, j, sk: (b, 0, 0)),
            pl.BlockSpec((ns, qb, kvw), lambda b, j, sk: (b, jnp.maximum(j * nsub - 1, 0), 0)),
            pl.BlockSpec((ns, ATT_STEP, kvw), lambda b, j, sk: (b, j, 0)),
            pl.BlockSpec((ns, qb, kvw), lambda b, j, sk: (b, jnp.minimum((j + 1) * nsub, n_blk - 1), 0)),
        ],
        out_specs=pl.BlockSpec((ns, ATT_STEP, ATT_Q_DIM), lambda b, j, sk: (b, j, 0)),
    )
    return pl.pallas_call(
        functools.partial(_attn_kernel, n_ctx_blk=n_ctx_blk, n_blk=n_blk),
        grid_spec=grid_spec,
        out_shape=jax.ShapeDtypeStruct((B, T, ATT_Q_DIM), BF16),
        compiler_params=_cparams(("parallel", "arbitrary")),
        name="window_attn",
    )(sink, q, kv, kv, kv, kv)


def _log_sigmoid(x):
    return jnp.minimum(x, 0.0) - jnp.log1p(jnp.exp(-jnp.abs(x)))


def _rows_to_lanes(a, base):
    return jnp.concatenate([a[base + h:base + h + 1, :] for h in range(MLSTM_HEADS)], axis=1)


def _mlstm_gates(d, gcol, grow):
    ch = gcol.shape[0]
    nh = MLSTM_HEADS
    wide = nh * ch
    fwd = d == 0

    r_i = lax.broadcasted_iota(jnp.int32, (ch, ch), 0)
    c_i = lax.broadcasted_iota(jnp.int32, (ch, ch), 1)
    seen_t = (r_i <= c_i) if fwd else (r_i >= c_i)
    seen_tt = (c_i <= r_i) if fwd else (c_i >= r_i)
    r_w = lax.broadcasted_iota(jnp.int32, (ch, wide), 0)
    s_w = lax.broadcasted_iota(jnp.int32, (ch, wide), 1) % ch
    seen_w = (r_w <= s_w) if fwd else (r_w >= s_w)

    base_i = 2 * d * nh
    base_f = base_i + nh

    lf_col = _log_sigmoid(gcol)
    cum_col = _dot3_l(seen_tt.astype(BF16), lf_col)
    lane16 = lax.broadcasted_iota(jnp.int32, gcol.shape, 1)
    z = jnp.where(jnp.logical_and(lane16 >= base_i, lane16 < base_f), gcol, -cum_col)
    ch16 = lax.broadcasted_iota(jnp.int32, (N_GATE, wide), 0)
    hd16 = lax.broadcasted_iota(jnp.int32, (N_GATE, wide), 1) // ch
    sel = jnp.logical_or(ch16 == base_i + hd16, ch16 == base_f + hd16).astype(BF16)
    x_t = _dot3(z, sel)

    lf_row = _log_sigmoid(grow)
    rhs2 = jnp.concatenate([seen_t.astype(BF16), jnp.ones((ch, ch), BF16)], axis=1)
    rows = _dot3(lf_row, rhs2)
    b_all = _rows_to_lanes(rows[:, 0:ch], base_f)
    g_all = _rows_to_lanes(rows[:, ch:], base_f)
    li_all = _rows_to_lanes(grow, base_i)
    yield
    dmat = jnp.where(seen_w, x_t + b_all, -jnp.inf)
    return (dmat, jnp.max(dmat, axis=0, keepdims=True), jnp.max(x_t, axis=0, keepdims=True),
            b_all, g_all, li_all)


def _mlstm_dir(qk, v_t, gates, ct_bd, n_bd, m_prev):
    dmat, dmat_max, x_max, b_all, g_all, li_all = gates
    ch = qk.shape[0]
    nh = MLSTM_HEADS
    dh = MLSTM_HEAD_DIM
    a = b_all + m_prev
    m = jnp.maximum(a, dmat_max)
    w_intra = jnp.exp(dmat - m)
    e_inter = jnp.exp(a - m)
    yield

    q = qk[:, 0:MLSTM_DIM]
    k = qk[:, MLSTM_DIM:]
    lb = lax.broadcasted_iota(jnp.int32, (1, MLSTM_DIM), 1) // dh
    q_bd = jnp.concatenate([q * (lb == h).astype(F32).astype(BF16) for h in range(nh)], axis=0)
    s_t = _dot_nt(k, q_bd) * w_intra
    nq = _rows_to_lanes(_dot_nt(n_bd.astype(BF16), q), 0)
    den = jnp.sum(s_t, axis=0, keepdims=True) + e_inter * nq
    inv = 1.0 / jnp.maximum(jnp.abs(den), jnp.exp(-m))
    inter_t = _dot_nt(ct_bd.astype(BF16), q)
    s_b = s_t.astype(BF16)
    yield
    outs = []
    for h in range(nh):
        seg = slice(h * ch, (h + 1) * ch)
        blk = slice(h * dh, (h + 1) * dh)
        num = _dot(v_t[blk, :], s_b[:, seg]) + e_inter[:, seg] * inter_t[blk, :]
        outs.append(num * inv[:, seg])
    h_t = jnp.concatenate(outs, axis=0)
    yield

    m_loc = g_all + x_max
    m_new = jnp.maximum(g_all + m_prev, m_loc)
    sa = jnp.exp(g_all + m_prev - m_new)
    sb = jnp.exp(m_loc - m_new)
    e_loc = jnp.exp(g_all - b_all + li_all - m_loc)
    v_e = jnp.concatenate(
        [v_t[h * dh:(h + 1) * dh, :].astype(F32) * e_loc[:, h * ch:(h + 1) * ch] for h in range(nh)],
        axis=0).astype(BF16)
    ct_loc = _dot(v_e, k)
    yield
    reps = MLSTM_DIM // ch

    def per_head_rows(row, nrows):
        return jnp.concatenate(
            [jnp.broadcast_to(jnp.concatenate([row[:, h * ch:(h + 1) * ch]] * reps, axis=1),
                              (nrows, MLSTM_DIM)) for h in range(nh)], axis=0)

    eb = lax.broadcasted_iota(jnp.int32, (MLSTM_DIM, MLSTM_DIM), 0) // dh
    db = lax.broadcasted_iota(jnp.int32, (MLSTM_DIM, MLSTM_DIM), 1) // dh
    ct_new = jnp.where(eb == db, per_head_rows(sa, dh) * ct_bd + per_head_rows(sb, dh) * ct_loc, 0.0)
    nrow = n_bd.shape[0]
    e_rows = jnp.concatenate([e_loc[:, h * ch:(h + 1) * ch] for h in range(nh)]
                             + [jnp.zeros((nrow - nh, ch), F32)], axis=0).astype(BF16)
    n_loc = _dot(e_rows, k)
    pad = jnp.zeros((nrow - nh, MLSTM_DIM), F32)
    sa8 = jnp.concatenate([per_head_rows(sa, 1), pad], axis=0)
    sb8 = jnp.concatenate([per_head_rows(sb, 1), pad], axis=0)
    hb8 = lax.broadcasted_iota(jnp.int32, (nrow, MLSTM_DIM), 0)
    db8 = lax.broadcasted_iota(jnp.int32, (nrow, MLSTM_DIM), 1) // dh
    n_new = jnp.where(hb8 == db8, sa8 * n_bd + sb8 * n_loc, 0.0)
    return h_t, ct_new, n_new, m_new


def _mlstm_kernel(qk_f_ref, qk_b_ref, vt_f_ref, vt_b_ref, gc_f_ref, gc_b_ref, gr_f_ref, gr_b_ref,
                  bcol_ref, brow_ref, hf_ref, hb_ref, c_scr, n_scr, m_scr):
    j = pl.program_id(1)

    @pl.when(j == 0)
    def _():
        c_scr[...] = jnp.zeros_like(c_scr)
        n_scr[...] = jnp.zeros_like(n_scr)
        m_scr[...] = jnp.zeros_like(m_scr)

    nscan = c_scr.shape[0]
    states = [(c_scr[k], n_scr[k], m_scr[k]) for k in range(nscan)]
    dirs = ((qk_f_ref, vt_f_ref, gc_f_ref, gr_f_ref, hf_ref),
            (qk_b_ref, vt_b_ref, gc_b_ref, gr_b_ref, hb_ref))
    ch = MLSTM_CH
    nsub = qk_f_ref.shape[1] // ch

    def scan(k):
        d = k % 2
        qk_ref, vt_ref, gc_ref, gr_ref, out_ref = [r.at[pl.ds(k // 2, 1)] for r in dirs[d]]
        st = states[k]
        order = [slice(sc * ch, (sc + 1) * ch)
                 for sc in (range(nsub) if d == 0 else reversed(range(nsub)))]
        gates = []
        for ts in order:
            gates.append((yield from _mlstm_gates(
                d, gc_ref[0, ts, :] + bcol_ref[...], gr_ref[0, :, ts] + brow_ref[...])))
        for ts, gt in zip(order, gates):
            h_t, *st = yield from _mlstm_dir(qk_ref[0, ts, :], vt_ref[0, :, ts], gt, *st)
            out_ref[0, :, ts] = h_t
        states[k] = st

    live = [scan(k) for k in range(nscan)]
    while live:
        for g in list(live):
            if next(g, live) is live:
                live.remove(g)
    for k in range(nscan):
        c_scr[k] = states[k][0]
        n_scr[k] = states[k][1]
        m_scr[k] = states[k][2]


def _mlstm(mqk, mvo_t, gcol, grow, gate_b, n_ctx):
    B, T, _ = mqk.shape
    ch = MLSTM_STEP
    nc = T // ch
    ncc = n_ctx // ch
    ns = PAIR_SAMPLES
    assert B % ns == 0

    def rev(j):
        return jnp.where(j < ncc, ncc - 1 - j, nc - 1 - (j - ncc))

    def tok(width, order):
        return pl.BlockSpec((ns, ch, width), lambda b, j: (b, order(j), 0))

    def chan(rows, order):
        return pl.BlockSpec((ns, rows, ch), lambda b, j: (b, 0, order(j)))

    ident = lambda j: j
    return pl.pallas_call(
        _mlstm_kernel,
        grid=(B // ns, nc),
        in_specs=[
            tok(2 * MLSTM_DIM, ident), tok(2 * MLSTM_DIM, rev),
            chan(MLSTM_DIM, ident), chan(MLSTM_DIM, rev),
            tok(N_GATE, ident), tok(N_GATE, rev),
            chan(N_GATE, ident), chan(N_GATE, rev),
            pl.BlockSpec((1, N_GATE), lambda b, j: (0, 0)),
            pl.BlockSpec((N_GATE, 1), lambda b, j: (0, 0)),
        ],
        out_specs=[chan(MLSTM_DIM, ident), chan(MLSTM_DIM, rev)],
        out_shape=[jax.ShapeDtypeStruct((B, MLSTM_DIM, T), F32)] * 2,
        scratch_shapes=[
            pltpu.VMEM((2 * ns, MLSTM_DIM, MLSTM_DIM), F32),
            pltpu.VMEM((2 * ns, 8, MLSTM_DIM), F32),
            pltpu.VMEM((2 * ns, 1, MLSTM_HEADS * MLSTM_CH), F32),
        ],
        compiler_params=_cparams(("parallel", "arbitrary")),
        name="mlstm",
    )(mqk, mqk, mvo_t, mvo_t, gcol, gcol, grow, grow,
      gate_b.reshape(1, N_GATE), gate_b.reshape(N_GATE, 1))


def _merge_kernel(*refs, n_src, off, n_ctx_tiles, alpha, n_batch):
    n_in = n_src + 7
    mod_ref = refs[n_in]
    consts = refs[n_in + 1:n_in + 13]
    outs = refs[n_in + 13:n_in + 17]
    counts_ref, carry = refs[n_in + 17], refs[n_in + 18]

    @pl.when(jnp.logical_and(pl.program_id(0) == 0, pl.program_id(1) == 0))
    def _():
        carry[...] = jnp.zeros_like(carry)

    is_ctx = pl.program_id(1) + off < n_ctx_tiles
    tiles = []
    for smp in range(MERGE_SAMPLES):
        row = jnp.where(is_ctx, n_batch, pl.program_id(0) * MERGE_SAMPLES + smp)
        tiles.append(_merge_tile(
            *[r.at[pl.ds(smp, 1)] for r in refs[:n_in]], mod_ref.at[pl.ds(row, 1)], *consts,
            *[r.at[pl.ds(smp, 1)] for r in outs], counts_ref, carry,
            n_src=n_src, off=off, n_ctx_tiles=n_ctx_tiles, alpha=alpha))
    live = list(tiles)
    while live:
        for g in list(live):
            if next(g, live) is live:
                live.remove(g)


def _merge_tile(*refs, n_src, off, n_ctx_tiles, alpha):
    (a0u_ref, up_ref, un_ref, yb_ref, hf_ref, hb_ref, mvo_t_ref, mod_ref,
     convw_ref, normw_ref, wg_ref, wpa_ref, wpb_ref, wpc_ref, wo_ref,
     ln_g_ref, ln_b_ref, wr_ref, br_ref, upper_ref,
     x1_ref, h2_ref, wt_ref, assign_ref, counts_ref, carry) = refs[n_src:]
    i = pl.program_id(1) + off
    nt = pl.num_programs(1) + off
    x_tile = _stream_tile(refs[:n_src], i, n_ctx_tiles)
    tm = x_tile.shape[0]
    d = x_tile.shape[1]

    a0 = a0u_ref[0, :, 0:CONV_DIM].astype(F32)
    u = a0u_ref[0, :, CONV_DIM:].astype(F32)
    prev_ok = jnp.logical_and(i != 0, i != n_ctx_tiles)
    next_ok = jnp.logical_and(i != n_ctx_tiles - 1, i != nt - 1)
    u_prev = jnp.where(prev_ok, up_ref[0, HALO - 1:HALO, CONV_DIM:].astype(F32), 0.0)
    u_next = jnp.where(next_ok, un_ref[0, 0:1, CONV_DIM:].astype(F32), 0.0)
    row = lax.broadcasted_iota(jnp.int32, u.shape, 0)
    u_dn = jnp.where(row == 0, u_prev, pltpu.roll(u, 1, axis=0))
    u_up = jnp.where(row == tm - 1, u_next, pltpu.roll(u, tm - 1, axis=0))
    cw = convw_ref[...]
    ya_all = (a0 * (u_dn * cw[0:1, :] + u * cw[1:2, :] + u_up * cw[2:3, :])).astype(BF16)

    w_hi = wr_ref[...].astype(BF16)
    sub = lax.broadcasted_iota(jnp.int32, (w_hi.shape[0], tm), 0)
    big = jnp.int32(2 * LANES)
    is_g = sub < N_GROUPS
    yield

    x = x_tile
    h = (_ln(x) * (1.0 + mod_ref[0, 1:2, :]) + mod_ref[0, 0:1, :]).astype(BF16)

    hm = hf_ref[0] + hb_ref[0]
    normed = jnp.concatenate(
        [_ln(hm[hd * MLSTM_HEAD_DIM:(hd + 1) * MLSTM_HEAD_DIM, :], axis=0)
         for hd in range(MLSTM_HEADS)], axis=0)
    yc_t = _sigmoid(mvo_t_ref[0].astype(F32)) * (normed * normw_ref[...])
    yc = yc_t.T
    yield

    pa = _dot(ya_all, wpa_ref[...])
    pb = _dot(yb_ref[0], wpb_ref[...])
    pc = _dot(yc.astype(BF16), wpc_ref[...])
    yield
    merged = None
    for n, proj in enumerate((pa, pb, pc)):
        gated = _sigmoid(_dot(h, wg_ref[:, n * d:(n + 1) * d])) * proj
        merged = gated if merged is None else merged + gated
        yield
    yl = _dot(merged.astype(BF16), wo_ref[...])
    yield

    x1 = _ln(alpha * x + mod_ref[0, 2:3, :] * yl) * ln_g_ref[...] + ln_b_ref[...]
    x1_ref[0] = x1
    yield
    h2 = _ln(x1) * (1.0 + mod_ref[0, 4:5, :]) + mod_ref[0, 3:4, :]
    for s in range(d // LANES):
        h2_ref[0, pl.ds(s, tm, stride=ROW_TILE), :] = h2[:, s * LANES:(s + 1) * LANES]
    yield

    lg = _dot_nt(w_hi, h2.astype(BF16)) + br_ref[...]
    gl = jnp.where(is_g, lg, -jnp.inf)
    g_max = jnp.max(gl, axis=0, keepdims=True)
    g_sel = jnp.min(jnp.where(gl == g_max, sub, big), axis=0, keepdims=True)
    g_p = 1.0 / jnp.sum(jnp.where(is_g, jnp.exp(gl - g_max), 0.0), axis=0, keepdims=True)
    lo = N_GROUPS + EXPERTS_PER_GROUP * g_sel
    el = jnp.where(jnp.logical_and(sub >= lo, sub < lo + EXPERTS_PER_GROUP), lg, -jnp.inf)
    e1 = jnp.max(el, axis=0, keepdims=True)
    i1 = jnp.min(jnp.where(el == e1, sub, big), axis=0, keepdims=True)
    el2 = jnp.where(sub == i1, -jnp.inf, el)
    e2 = jnp.max(el2, axis=0, keepdims=True)
    i2 = jnp.min(jnp.where(el2 == e2, sub, big), axis=0, keepdims=True)
    t = jnp.exp(e2 - e1)
    w1 = g_p / (1.0 + t)
    w2 = w1 * t
    eid1 = (i1 - N_GROUPS).astype(F32)
    eid2 = (i2 - N_GROUPS).astype(F32)
    rows = jnp.concatenate([eid1, eid2, w1, w2, jnp.zeros((4, tm), F32)], axis=0)
    wt_ref[0] = rows.T
    yield
    eid = jnp.concatenate([eid1, eid2], axis=1)
    sub_e = lax.broadcasted_iota(jnp.int32, (N_EXPERTS, eid.shape[1]), 0).astype(F32)
    onehot = jnp.where(sub_e == eid, 1.0, 0.0)
    earlier = _dot(onehot.astype(BF16), upper_ref[...])
    seen = carry[...]
    rank = jnp.sum(onehot * (earlier + seen), axis=0, keepdims=True)
    assign_ref[0, 0] = jnp.concatenate([eid, rank], axis=0).astype(jnp.int32)
    seen = seen + jnp.sum(onehot, axis=1, keepdims=True)
    carry[...] = seen
    counts_ref[...] = seen


def _merge(streams, mod, a0u, yb, hf_t, hb_t, mvo_t, conv_w, norm_w, w_gate, w_pa, w_pb, w_pc, w_o,
           ln_g, ln_b, w_route, b_route, n_ctx, off, alpha):
    B, _, D = streams[0].shape
    T = a0u.shape[1]
    tm = TOK_TILE
    nct = n_ctx // tm
    nt = T // tm - off
    tn = nt * tm
    hpt = tm // HALO
    nhalo = T // HALO

    ns = MERGE_SAMPLES
    assert B % ns == 0

    def tok(width):
        return pl.BlockSpec((ns, tm, width), lambda b, i: (b, i + off, 0))

    def chan(rows):
        return pl.BlockSpec((ns, rows, tm), lambda b, i: (b, 0, i + off))

    def full(a):
        return pl.BlockSpec(a.shape, lambda b, i: (0,) * a.ndim, pipeline_mode=pl.Buffered(1))

    def otok(width):
        return pl.BlockSpec((ns, tm, width), lambda b, i: (b, i, 0))

    a_i = lax.broadcasted_iota(jnp.int32, (TOP_K * tm, TOP_K * tm), 0)
    b_i = lax.broadcasted_iota(jnp.int32, (TOP_K * tm, TOP_K * tm), 1)
    upper = (a_i < b_i).astype(BF16)
    consts = [conv_w, norm_w, w_gate, w_pa, w_pb, w_pc, w_o, ln_g, ln_b, w_route, b_route, upper]
    return pl.pallas_call(
        functools.partial(_merge_kernel, n_src=len(streams), off=off, n_ctx_tiles=nct, alpha=alpha,
                          n_batch=B),
        grid=(B // ns, nt),
        in_specs=_stream_specs(streams, tm, nct, off, ns) + [
            tok(2 * CONV_DIM),
            pl.BlockSpec((ns, HALO, 2 * CONV_DIM),
                         lambda b, i: (b, jnp.maximum((i + off) * hpt - 1, 0), 0)),
            pl.BlockSpec((ns, HALO, 2 * CONV_DIM),
                         lambda b, i: (b, jnp.minimum((i + off + 1) * hpt, nhalo - 1), 0)),
            tok(ATT_Q_DIM), chan(MLSTM_DIM), chan(MLSTM_DIM),
            pl.BlockSpec((ns, MLSTM_DIM, tm), lambda b, i: (b, 1, i + off)),
            full(mod),
        ] + [full(a) for a in consts],
        out_specs=[otok(D),
                   pl.BlockSpec((ns, tm * ROW_TILE, LANES), lambda b, i: (b, i, 0)),
                   otok(ROW_TILE),
                   pl.BlockSpec((ns, 1, 2, TOP_K * tm), lambda b, i: (b, i, 0, 0)),
                   pl.BlockSpec((N_EXPERTS, 1), lambda b, i: (0, 0))],
        out_shape=[jax.ShapeDtypeStruct((B, tn, D), F32),
                   jax.ShapeDtypeStruct((B, tn * ROW_TILE, LANES), F32),
                   jax.ShapeDtypeStruct((B, tn, ROW_TILE), F32),
                   jax.ShapeDtypeStruct((B, nt, 2, TOP_K * tm), jnp.int32),
                   jax.ShapeDtypeStruct((N_EXPERTS, 1), F32)],
        scratch_shapes=[pltpu.VMEM((N_EXPERTS, 1), F32)],
        compiler_params=_cparams(("arbitrary", "arbitrary")),
        name="merge",
    )(*streams, a0u, a0u, a0u, yb, hf_t, hb_t, mvo_t, mod, *consts)


def _dispatch_kernel(pos_ref, pend_ref, nblk_ref, h_ref, xs_hbm, zbuf, sem, zsem, *, tm):
    s = pl.program_id(0)
    tiles = h_ref.shape[0] // (tm * ROW_TILE)
    zrows = zbuf.shape[0]
    last_blk = xs_hbm.shape[0] // zrows - 1

    @pl.when(s == 0)
    def _():
        zbuf[...] = jnp.zeros_like(zbuf)

        def zero_copy(start):
            return pltpu.make_async_copy(
                zbuf, xs_hbm.at[pl.ds(pl.multiple_of(start, ROW_TILE), zrows), :], zsem)

        jobs = []
        for e in range(N_EXPERTS):
            before = pend_ref[e - 1] if e > 0 else 0
            jobs.append((pend_ref[e] * ROW_TILE - zrows, pend_ref[e] > before))
        for e in range(N_EXPERTS):
            jobs.append(((nblk_ref[0] + e) * zrows, nblk_ref[0] + e <= last_blk))
        for st, needed in jobs:
            @pl.when(needed)
            def _(st=st):
                zero_copy(st).start()
        for st, needed in jobs:
            @pl.when(needed)
            def _(st=st):
                zero_copy(st).wait()

    def row_copy(u, r, k):
        src = pl.multiple_of((u * tm + r) * ROW_TILE, ROW_TILE)
        dst = pl.multiple_of(pos_ref[((s * tiles + u) * TOP_K + k) * tm + r] * ROW_TILE, ROW_TILE)
        return pltpu.make_async_copy(h_ref.at[pl.ds(src, ROW_TILE), :],
                                     xs_hbm.at[pl.ds(dst, ROW_TILE), :], sem)

    for u in range(tiles):
        def body(r, carry, u=u):
            for k in range(TOP_K):
                row_copy(u, r, k).start()
            return carry
        lax.fori_loop(0, tm, body, 0, unroll=GATHER_UNROLL)
    for _ in range(tiles * TOP_K):
        pltpu.make_async_copy(h_ref.at[pl.ds(0, tm * ROW_TILE), :],
                              xs_hbm.at[pl.ds(0, tm * ROW_TILE), :], sem).wait()


def _dispatch(h2v, pos, pend, nblk, p_rows, tm):
    rows = h2v.shape[0]
    step_rows = DISPATCH_TILES * tm * ROW_TILE
    assert rows % step_rows == 0
    nsteps = rows // step_rows
    grid_spec = pltpu.PrefetchScalarGridSpec(
        num_scalar_prefetch=3,
        grid=(nsteps,),
        in_specs=[pl.BlockSpec((step_rows, LANES), lambda s, pos, pend, nbk: (s, 0))],
        out_specs=pl.BlockSpec(memory_space=pl.ANY),
        scratch_shapes=[
            pltpu.VMEM((EXP_BLK * ROW_TILE, LANES), F32),
            pltpu.SemaphoreType.DMA,
            pltpu.SemaphoreType.DMA,
        ],
    )
    return pl.pallas_call(
        functools.partial(_dispatch_kernel, tm=tm),
        grid_spec=grid_spec,
        out_shape=jax.ShapeDtypeStruct((p_rows * ROW_TILE, LANES), F32),
        compiler_params=_cparams(("arbitrary",)),
        name="dispatch",
    )(pos, pend, nblk, h2v)


def _expert_kernel(blke_ref, nblk_ref, xs_ref, wi_ref, wo_ref, y_ref, wi_bf, wo_bf):
    i = pl.program_id(0)
    nb = nblk_ref[0]
    blk = xs_ref.shape[0] // ROW_TILE
    nsl = wi_ref.shape[2] // LANES

    @pl.when(i < nb)
    def _():
        e_now = blke_ref[i]
        e_before = blke_ref[jnp.maximum(i - 1, 0)]

        @pl.when(jnp.logical_or(i == 0, e_now != e_before))
        def _():
            wi_bf[...] = wi_ref[0, 0].astype(BF16)
            wo_bf[...] = wo_ref[0, 0].astype(BF16)

        def rows_part(part):
            r0 = part * EXP_ROWS * ROW_TILE
            xin = jnp.concatenate(
                [xs_ref[pl.ds(r0 + s, EXP_ROWS, stride=ROW_TILE), :] for s in range(nsl)],
                axis=1).astype(BF16)
            yield
            mid = _dot(xin, wi_bf[...])
            yield
            gt = mid[:, 0:D_EXPERT]
            up = mid[:, D_EXPERT:]
            act = ((gt * _sigmoid(gt)) * up).astype(BF16)
            yield
            y = _dot(act, wo_bf[...])
            yield
            for s in range(nsl):
                y_ref[pl.ds(r0 + s, EXP_ROWS, stride=ROW_TILE), :] = y[:, s * LANES:(s + 1) * LANES]

        parts = [rows_part(p) for p in range(blk // EXP_ROWS)]
        for t in range(len(parts) + 4):
            for p, gen in enumerate(parts):
                if 0 <= t - p <= 4:
                    next(gen, None)

    @pl.when(i >= nb)
    def _():
        y_ref[...] = jnp.zeros_like(y_ref)


def _experts(xs, blk_e, nblk, w_ei, w_eo, layer):
    d = w_ei.shape[2]
    blk = EXP_BLK
    nb = xs.shape[0] // (blk * ROW_TILE)
    grid_spec = pltpu.PrefetchScalarGridSpec(
        num_scalar_prefetch=2,
        grid=(nb,),
        in_specs=[
            pl.BlockSpec((blk * ROW_TILE, LANES),
                         lambda i, be, nbk: (jnp.minimum(i, jnp.maximum(nbk[0] - 1, 0)), 0)),
            pl.BlockSpec((1, 1, d, 2 * D_EXPERT), lambda i, be, nbk: (layer, be[i], 0, 0)),
            pl.BlockSpec((1, 1, D_EXPERT, d), lambda i, be, nbk: (layer, be[i], 0, 0)),
        ],
        out_specs=pl.BlockSpec((blk * ROW_TILE, LANES), lambda i, be, nbk: (i, 0)),
        scratch_shapes=[
            pltpu.VMEM((d, 2 * D_EXPERT), BF16),
            pltpu.VMEM((D_EXPERT, d), BF16),
        ],
    )
    return pl.pallas_call(
        _expert_kernel,
        grid_spec=grid_spec,
        out_shape=jax.ShapeDtypeStruct(xs.shape, F32),
        compiler_params=_cparams(("arbitrary",)),
        name="experts",
    )(blk_e, nblk, xs, w_ei, w_eo)


def _combine_kernel(pos_ref, x_ref, mod_ref, wt_ref, ln_g_ref, ln_b_ref, y_hbm, o_ref,
                    ybuf, sem, *, alpha):
    b = pl.program_id(0)
    i = pl.program_id(1)
    nt = pl.num_programs(1)
    tm = x_ref.shape[1]
    step = b * nt + i
    nsteps = pl.num_programs(0) * nt

    def start_gather(s, slot):
        def body(r, carry):
            dst = pl.multiple_of(r * ROW_TILE, ROW_TILE)
            for k in range(TOP_K):
                src = pl.multiple_of(pos_ref[(s * TOP_K + k) * tm + r] * ROW_TILE, ROW_TILE)
                pltpu.make_async_copy(y_hbm.at[pl.ds(src, ROW_TILE), :],
                                      ybuf.at[slot, k, pl.ds(dst, ROW_TILE), :], sem.at[slot]).start()
            return carry
        lax.fori_loop(0, tm, body, 0, unroll=GATHER_UNROLL)

    @pl.when(step == 0)
    def _():
        start_gather(0, 0)

    @pl.when(step + 1 < nsteps)
    def _():
        start_gather(step + 1, (step + 1) % 2)

    slot = step % 2
    for k in range(TOP_K):
        pltpu.make_async_copy(y_hbm.at[pl.ds(0, tm * ROW_TILE), :], ybuf.at[slot, k],
                              sem.at[slot]).wait()
    wt = wt_ref[0]
    w0 = wt[:, 2:3]
    w1 = wt[:, 3:4]
    f = jnp.concatenate(
        [w0 * ybuf[slot, 0, pl.ds(j, tm, stride=ROW_TILE), :]
         + w1 * ybuf[slot, 1, pl.ds(j, tm, stride=ROW_TILE), :]
         for j in range(x_ref.shape[2] // LANES)], axis=1)
    x = x_ref[0]
    o_ref[0] = _ln(alpha * x + mod_ref[0, 5:6, :] * f) * ln_g_ref[...] + ln_b_ref[...]


def _combine(x1, mod, wts, pos, y, ln_g, ln_b, n_ctx_tiles, alpha):
    B, tn, D = x1.shape
    tm = TOK_TILE
    nt = tn // tm
    grid_spec = pltpu.PrefetchScalarGridSpec(
        num_scalar_prefetch=1,
        grid=(B, nt),
        in_specs=[
            pl.BlockSpec((1, tm, D), lambda b, i, pos: (b, i, 0)),
            pl.BlockSpec((1, 6, D), lambda b, i, pos: (jnp.where(i < n_ctx_tiles, B, b), 0, 0)),
            pl.BlockSpec((1, tm, ROW_TILE), lambda b, i, pos: (b, i, 0)),
            pl.BlockSpec((1, D), lambda b, i, pos: (0, 0)),
            pl.BlockSpec((1, D), lambda b, i, pos: (0, 0)),
            pl.BlockSpec(memory_space=pl.ANY),
        ],
        out_specs=pl.BlockSpec((1, tm, D), lambda b, i, pos: (b, i, 0)),
        scratch_shapes=[
            pltpu.VMEM((2, TOP_K, tm * ROW_TILE, LANES), F32),
            pltpu.SemaphoreType.DMA((2,)),
        ],
    )
    return pl.pallas_call(
        functools.partial(_combine_kernel, alpha=alpha),
        grid_spec=grid_spec,
        out_shape=jax.ShapeDtypeStruct((B, tn, D), F32),
        compiler_params=_cparams(("arbitrary", "arbitrary")),
        name="combine",
    )(pos, x1, mod, wts, ln_g, ln_b, y)


def _segments(counts, assign):
    counts = counts.reshape(N_EXPERTS).astype(jnp.int32)
    n_assign = assign.shape[0] * assign.shape[2]
    padded = (counts + EXP_BLK - 1) // EXP_BLK * EXP_BLK
    pad_end = jnp.cumsum(padded)
    pad_start = pad_end - padded
    p_rows = n_assign + N_EXPERTS * EXP_BLK
    nb = p_rows // EXP_BLK
    blk_first = jnp.arange(nb, dtype=jnp.int32) * EXP_BLK
    blk_e = jnp.minimum(jnp.sum((pad_end[None, :] <= blk_first[:, None]).astype(jnp.int32), axis=1),
                        N_EXPERTS - 1).astype(jnp.int32)
    nblk = (pad_end[-1] // EXP_BLK).astype(jnp.int32).reshape(1)
    eid, rank = assign[:, 0, :], assign[:, 1, :]
    onehot = eid[:, :, None] == jnp.arange(N_EXPERTS, dtype=jnp.int32)
    pos = rank + jnp.sum(jnp.where(onehot, pad_start.astype(jnp.int32), 0), axis=-1)
    return pos.reshape(-1).astype(jnp.int32), pad_end.astype(jnp.int32), blk_e, nblk, p_rows


def _rope_tables(n_ctx, n_lat):
    nf = ATT_HEAD_DIM // 4
    inv = ROPE_BASE ** (-jnp.arange(nf, dtype=F32) / nf)
    rows = n_lat // GRID_W
    pos_r = jnp.repeat(jnp.arange(rows, dtype=F32), GRID_W)
    pos_c = jnp.tile(jnp.arange(GRID_W, dtype=F32), rows)
    ang_r = pos_r[:, None] * inv
    ang_c = pos_c[:, None] * inv
    cos_h = jnp.concatenate([jnp.cos(ang_r)] * 2 + [jnp.cos(ang_c)] * 2, axis=-1)
    sin_h = jnp.concatenate([-jnp.sin(ang_r), jnp.sin(ang_r),
                             -jnp.sin(ang_c), jnp.sin(ang_c)], axis=-1)
    reps = LANES // ATT_HEAD_DIM
    cos_l = jnp.tile(cos_h, (1, reps))
    sin_l = jnp.tile(sin_h, (1, reps))
    cos_t = jnp.concatenate([jnp.ones((n_ctx, LANES), F32), cos_l], axis=0)
    sin_t = jnp.concatenate([jnp.zeros((n_ctx, LANES), F32), sin_l], axis=0)
    return cos_t, sin_t


def _projection_weights(w_in_l):
    offs = np.cumsum((0, 3 * CONV_DIM, ATT_Q_DIM, ATT_KV_DIM, ATT_KV_DIM,
                      MLSTM_DIM, MLSTM_DIM, MLSTM_DIM, MLSTM_DIM, N_GATE)).tolist()
    a, q, k, v, mq, mk, mv, mo, g = [w_in_l[:, offs[n]:offs[n + 1]] for n in range(9)]
    hd = ATT_HEAD_DIM

    def swap(w):
        return jnp.concatenate([w[:, hd:], w[:, :hd]], axis=1)

    w_tok = jnp.concatenate([a, q, k, swap(k), v, swap(v), mq, mk], axis=1).astype(BF16)
    w_chan = jnp.concatenate([mv, mo, g], axis=1).T.astype(BF16)
    w_gate = w_in_l[:, offs[9]:].astype(BF16)
    return w_tok, w_chan, w_gate


def kernel(x, c, ctx, c_ctx, w_ada, b_ada, w_in, conv_w, attn_sink, mlstm_gate_b, mlstm_norm_w,
           w_proj_a, w_proj_b, w_proj_c, w_out, ln1_g, ln1_b, w_route_group, b_route_group,
           w_route_expert, b_route_expert, w_expert_in, w_expert_out, ln2_g, ln2_b):
    B, L, D = x.shape
    n_ctx = ctx.shape[1]
    depth = w_ada.shape[0]
    T = n_ctx + L
    alpha = (2 * depth) ** 0.25
    assert D == D_MODEL and n_ctx % TOK_TILE == 0 and L % TOK_TILE == 0 and L % GRID_W == 0
    assert MLSTM_CH == LANES and ATT_QB == LANES
    nct = n_ctx // TOK_TILE

    nrows = -(-(B + 1) // 8) * 8
    cond = jnp.concatenate([c, c_ctx[None, :], jnp.zeros((nrows - B - 1, D), F32)], axis=0)
    mod_all = _ada(cond, w_ada, b_ada).reshape(depth, nrows, 6, D)

    cos_t, sin_t = _rope_tables(n_ctx, L)
    streams = (ctx, x)

    for i in range(depth):
        need_ctx = i < depth - 1
        mod = mod_all[i]
        w_tok, w_chan, w_gate = _projection_weights(w_in[i])

        a0u, q, kv, mqk, mvo_t, gcol, grow = _inproj(streams, mod, w_tok, w_chan, cos_t, sin_t, n_ctx)
        yb = _attention(q, kv, attn_sink[i], n_ctx)
        hf_t, hb_t = _mlstm(mqk, mvo_t, gcol, grow, mlstm_gate_b[i], n_ctx)

        off = 0 if need_ctx else nct
        w_route = jnp.pad(jnp.concatenate([w_route_group[i], w_route_expert[i]], axis=1).T,
                          ((0, LANES - N_ROUTE), (0, 0)))
        b_route = jnp.pad(jnp.concatenate([b_route_group[i], b_route_expert[i]]),
                          (0, LANES - N_ROUTE)).reshape(LANES, 1)
        x1, h2v, wts, assign, counts = _merge(
            streams, mod, a0u, yb, hf_t, hb_t, mvo_t, conv_w[i], mlstm_norm_w[i].reshape(MLSTM_DIM, 1),
            w_gate, w_proj_a[i].astype(BF16), w_proj_b[i].astype(BF16), w_proj_c[i].astype(BF16),
            w_out[i].astype(BF16), ln1_g[i].reshape(1, D), ln1_b[i].reshape(1, D),
            w_route, b_route, n_ctx, off, alpha)

        tn = x1.shape[1]
        pos, pend, blk_e, nblk, p_rows = _segments(counts, assign.reshape(-1, 2, TOP_K * TOK_TILE))
        xs = _dispatch(h2v.reshape(-1, LANES), pos, pend, nblk, p_rows, TOK_TILE)
        y = _experts(xs, blk_e, nblk, w_expert_in, w_expert_out, i)
        streams = (_combine(x1, mod, wts, pos, y,
                            ln2_g[i].reshape(1, D), ln2_b[i].reshape(1, D),
                            nct if need_ctx else 0, alpha),)
    return streams[0]
```
